```python
import jax, jax.numpy as jnp
from jax import lax
import numpy as np

D_MODEL = 1024
BATCH = 4
SEQ = 4096
DEPTH = 1
DEC_BATCH = 32
DEC_SEQ = 8
PAST_LEN = 16384
PAGE_SIZE = 128

N_HEADS = 8
N_KV = 2
HPG = N_HEADS // N_KV
HEAD_DIM = 64
ATTN_WIDTH = N_HEADS * HEAD_DIM
POOL_WIDTH = D_MODEL - ATTN_WIDTH
POOL_WINDOWS = (2, 4, 8, 16)
N_POOL_GROUPS = len(POOL_WINDOWS)
POOL_GROUP = POOL_WIDTH // N_POOL_GROUPS
POOL_HIST = max(POOL_WINDOWS) - 1
BLK = 64
N_SELECT = 16
WINDOW = 512
CMP_HIDDEN = 2 * HEAD_DIM
D_FF = 4 * D_MODEL
Q_BLOCK = 128
KV_WIDTH = N_KV * 2 * HEAD_DIM
PROJ_WIDTH = ATTN_WIDTH + 3 * KV_WIDTH + 3 * N_HEADS + POOL_WIDTH
SPLITS = (ATTN_WIDTH, ATTN_WIDTH + KV_WIDTH, ATTN_WIDTH + 2 * KV_WIDTH,
          ATTN_WIDTH + 3 * KV_WIDTH, ATTN_WIDTH + 3 * KV_WIDTH + 3 * N_HEADS)
BLOCKS_PER_PAGE = PAGE_SIZE // BLK
SCALE = HEAD_DIM ** -0.5
FORCE_SCORE = float(HPG + 1)
NORM_EPS = 1e-6
NEG = -1e30

kernel_name = "hymba_nsa_poolformer_decoder_step"


def rms_norm(x, g):
    xf = x.astype(jnp.float32)
    y = xf * lax.rsqrt(jnp.mean(xf * xf, axis=-1, keepdims=True) + NORM_EPS)
    return (y * g.astype(jnp.float32)).astype(x.dtype)


def alibi_slopes():
    h = jnp.arange(N_HEADS, dtype=jnp.float32) + 1.0
    return jnp.exp2(-8.0 * h / N_HEADS).reshape(N_KV, HPG)


def masked_softmax(s, mask):
    s = jnp.where(mask, s, NEG)
    m = jnp.max(s, axis=-1, keepdims=True)
    p = jnp.where(mask, jnp.exp(s - m), 0.0)
    return p / jnp.maximum(jnp.sum(p, axis=-1, keepdims=True), 1e-30)


def split_proj(hp):
    B, T, _ = hp.shape
    q, kc, ks, kw, gl, u = jnp.split(hp, list(SPLITS), axis=-1)
    kvs = lambda a: a.reshape(B, T, N_KV, 2, HEAD_DIM)
    gates = jax.nn.sigmoid(gl.astype(jnp.float32)).reshape(B, T, 3, N_KV, HPG)
    return q.reshape(B, T, N_KV, HPG, HEAD_DIM), kvs(kc), kvs(ks), kvs(kw), gates, u


def compress_blocks(kv, pe, w1, w2):
    B, L = kv.shape[:2]
    blocks = kv.reshape(B, L // BLK, BLK, N_KV, 2, HEAD_DIM) + pe[:, None]
    hid = jax.nn.gelu(jnp.einsum('bnlgcd,cldh->bngch', blocks, w1))
    return jnp.einsum('bngch,chd->bngcd', hid, w2)


def cmp_attend(q, pos_q, comp, slopes):
    nb = comp.shape[1]
    end = (jnp.arange(nb) + 1) * BLK - 1
    dist = (pos_q[:, None] - end[None, :]).astype(jnp.float32)
    s = jnp.einsum('btgpd,bngd->btgpn', q, comp[..., 0, :]).astype(jnp.float32) * SCALE
    s = s - slopes[None, None, :, :, None] * dist[None, :, None, None, :]
    p = masked_softmax(s, (dist >= 0)[None, :, None, None, :])
    o = jnp.einsum('btgpn,bngd->btgpd', p.astype(comp.dtype), comp[..., 1, :])
    return o, p


def select_blocks(imp, pos_q, nb):
    jj = jnp.arange(nb)[None, :]
    cur = (pos_q // BLK)[:, None]
    forced = (jj == 0) | (jj == cur) | (jj == cur - 1)
    score = jnp.where(forced[None, :, None, :], FORCE_SCORE, imp)
    score = jnp.where((jj > cur)[None, :, None, :], -1.0, score)
    _, idx = lax.top_k(score, min(N_SELECT, nb))
    return idx


def gather_local_blocks(blocks, idx):
    B = blocks.shape[0]
    return blocks[jnp.arange(B)[:, None, None, None], jnp.arange(N_KV)[None, None, :, None], idx]


def slc_attend(q, pos_q, kv_sel, idx, slopes):
    B, Tq, G, n = idx.shape
    S = n * BLK
    k = kv_sel[..., 0, :].reshape(B, Tq, G, S, HEAD_DIM)
    v = kv_sel[..., 1, :].reshape(B, Tq, G, S, HEAD_DIM)
    pos_k = (idx[..., None] * BLK + jnp.arange(BLK)).reshape(B, Tq, G, S)
    dist = (pos_q[None, :, None, None] - pos_k).astype(jnp.float32)
    s = jnp.einsum('btgpd,btgsd->btgps', q, k).astype(jnp.float32) * SCALE
    s = s - slopes[None, None, :, :, None] * dist[:, :, :, None, :]
    p = masked_softmax(s, (dist >= 0)[:, :, :, None, :])
    return jnp.einsum('btgps,btgsd->btgpd', p.astype(v.dtype), v)


def win_attend(q, pos_q, kv, pos_k, slopes):
    dist_i = pos_q[:, None] - pos_k[None, :]
    mask = (dist_i >= 0) & (dist_i < WINDOW) & (pos_k >= 0)[None, :]
    dist = dist_i.astype(jnp.float32)
    s = jnp.einsum('btgpd,bsgd->btgps', q, kv[..., 0, :]).astype(jnp.float32) * SCALE
    s = s - slopes[None, None, :, :, None] * dist[None, :, None, None, :]
    p = masked_softmax(s, mask[None, :, None, None, :])
    return jnp.einsum('btgps,bsgd->btgpd', p.astype(kv.dtype), kv[..., 1, :])


def nsa_heads(q, pos_q, comp, gather_sel, kv_win, pos_win, gates, slopes):
    o_c, p_c = cmp_attend(q, pos_q, comp, slopes)
    idx = select_blocks(jnp.sum(p_c, axis=3), pos_q, comp.shape[1])
    o_s = slc_attend(q, pos_q, gather_sel(idx), idx, slopes)
    o_w = win_attend(q, pos_q, kv_win, pos_win, slopes)
    o = (gates[:, :, 0, :, :, None] * o_c + gates[:, :, 1, :, :, None] * o_s
         + gates[:, :, 2, :, :, None] * o_w)
    return o.astype(q.dtype)


def pool_mix(u_ext, pos, w_pool, pool_scale):
    B, L, _ = u_ext.shape
    T = pos.shape[0]
    uf = u_ext.astype(jnp.float32)
    c = jnp.pad(jnp.cumsum(uf, axis=1), ((0, 0), (1, 0), (0, 0)))
    own = uf[:, L - T:]
    outs = []
    for gi, w in enumerate(POOL_WINDOWS):
        sl = slice(gi * POOL_GROUP, (gi + 1) * POOL_GROUP)
        hi = c[:, L - T + 1:, sl]
        lo = c[:, L - T + 1 - w: L + 1 - w, sl]
        cnt = jnp.minimum(w, pos + 1).astype(jnp.float32)[None, :, None]
        outs.append((hi - lo) / cnt - own[..., sl])
    d = jnp.stack(outs, axis=2).astype(u_ext.dtype)
    y = jnp.einsum('btgc,gcd->btgd', d, w_pool).reshape(B, T, POOL_WIDTH)
    return y * pool_scale


def merge_and_mlp(x, o, pool_out, w_out, g_mlp, w_up, w_down):
    B, T, _ = x.shape
    mix = jnp.concatenate([o.reshape(B, T, ATTN_WIDTH).astype(x.dtype),
                           pool_out.astype(x.dtype)], axis=-1)
    h = x + mix @ w_out
    a = jnp.square(jax.nn.relu(rms_norm(h, g_mlp) @ w_up))
    return h + a @ w_down


def prompt_layer(x, g_attn, w_in, cmp_pe, cmp_w1, cmp_w2, w_pool, pool_scale,
                 w_out, g_mlp, w_up, w_down, slopes):
    B, T, _ = x.shape
    q, kv_c, kv_s, kv_w, gates, u = split_proj(rms_norm(x, g_attn) @ w_in)
    comp = compress_blocks(kv_c, cmp_pe, cmp_w1, cmp_w2)
    blocks_s = kv_s.reshape(B, T // BLK, BLK, N_KV, 2, HEAD_DIM).transpose(0, 3, 1, 2, 4, 5)
    kv_w_pad = jnp.pad(kv_w, ((0, 0), (WINDOW, 0), (0, 0), (0, 0), (0, 0)))
    nqb = T // Q_BLOCK
    to_blocks = lambda a: a.reshape((B, nqb, Q_BLOCK) + a.shape[2:]).swapaxes(0, 1)

    def per_block(args):
        i, qb, gb = args
        start = i * Q_BLOCK
        pos_b = start + jnp.arange(Q_BLOCK)
        kvw = lax.dynamic_slice_in_dim(kv_w_pad, start, Q_BLOCK + WINDOW, axis=1)
        pos_w = start - WINDOW + jnp.arange(Q_BLOCK + WINDOW)
        return nsa_heads(qb, pos_b, comp, lambda idx: gather_local_blocks(blocks_s, idx),
                         kvw, pos_w, gb, slopes)

    o = lax.map(per_block, (jnp.arange(nqb), to_blocks(q), to_blocks(gates)))
    o = o.swapaxes(0, 1).reshape(B, T, N_KV, HPG, HEAD_DIM)
    pos = jnp.arange(T)
    p = pool_mix(jnp.pad(u, ((0, 0), (POOL_HIST, 0), (0, 0))), pos, w_pool, pool_scale)
    h = merge_and_mlp(x, o, p, w_out, g_mlp, w_up, w_down)
    wk = min(WINDOW, T)
    return h, (kv_c, kv_s, kv_w[:, T - wk:], u[:, T - POOL_HIST:])


def sample_layer(x, cache_cmp, cache_slc, state_win, state_pool, page_table,
                 g_attn, w_in, cmp_pe, cmp_w1, cmp_w2, w_pool, pool_scale,
                 w_out, g_mlp, w_up, w_down, slopes):
    B, T, _ = x.shape
    past = page_table.shape[1] * PAGE_SIZE
    nb_past = past // BLK
    nb_new = -(-T // BLK)
    pos = past + jnp.arange(T)
    q, kv_c, kv_s, kv_w, gates, u = split_proj(rms_norm(x, g_attn) @ w_in)
    pad_new = ((0, 0), (0, nb_new * BLK - T), (0, 0), (0, 0), (0, 0))
    past_c = cache_cmp[page_table].reshape(B, past, N_KV, 2, HEAD_DIM)
    comp = jnp.concatenate([compress_blocks(past_c, cmp_pe, cmp_w1, cmp_w2),
                            compress_blocks(jnp.pad(kv_c, pad_new), cmp_pe, cmp_w1, cmp_w2)], axis=1)
    new_blocks = jnp.pad(kv_s, pad_new).reshape(B, nb_new, BLK, N_KV, 2, HEAD_DIM).transpose(0, 3, 1, 2, 4, 5)

    def gather_sel(idx):
        in_past = idx < nb_past
        jp = jnp.minimum(idx, nb_past - 1)
        phys = page_table[jnp.arange(B)[:, None, None, None], jp // BLOCKS_PER_PAGE]
        offs = ((jp % BLOCKS_PER_PAGE) * BLK)[..., None] + jnp.arange(BLK)
        rows_past = cache_slc[phys[..., None], offs, jnp.arange(N_KV)[None, None, :, None, None]]
        rows_new = gather_local_blocks(new_blocks, jnp.clip(idx - nb_past, 0, nb_new - 1))
        return jnp.where(in_past[..., None, None, None], rows_past, rows_new)

    wk = state_win.shape[1]
    kv_wext = jnp.concatenate([state_win, kv_w], axis=1)
    pos_w = past - wk + jnp.arange(wk + T)
    o = nsa_heads(q, pos, comp, gather_sel, kv_wext, pos_w, gates, slopes)
    u_ext = jnp.concatenate([state_pool, u], axis=1)
    p = pool_mix(u_ext, pos, w_pool, pool_scale)
    h = merge_and_mlp(x, o, p, w_out, g_mlp, w_up, w_down)
    return h, (kv_c, kv_s, kv_wext[:, T:], u_ext[:, T:])


def setup_inputs(seed: int = 0) -> dict:
    key = jax.random.key(seed)
    ks = jax.random.split(key, 20)
    n_pages = PAST_LEN // PAGE_SIZE
    n_used = DEC_BATCH * n_pages
    n_pool = n_used + n_used // 4
    wk = min(WINDOW, PAST_LEN)
    nrm = lambda k, shape, scale=1.0: scale * jax.random.normal(k, shape, jnp.float32)
    page_table = jax.random.permutation(ks[0], n_pool)[:n_used].reshape(DEC_BATCH, n_pages).astype(jnp.int32)
    return {
        "x_prompt": nrm(ks[1], (BATCH, SEQ, D_MODEL)),
        "x_sample": nrm(ks[2], (DEC_BATCH, DEC_SEQ, D_MODEL)),
        "cache_cmp": nrm(ks[3], (DEPTH, n_pool, PAGE_SIZE, N_KV, 2, HEAD_DIM)),
        "cache_slc": nrm(ks[4], (DEPTH, n_pool, PAGE_SIZE, N_KV, 2, HEAD_DIM)),
        "state_win": nrm(ks[5], (DEPTH, DEC_BATCH, wk, N_KV, 2, HEAD_DIM)),
        "state_pool": nrm(ks[6], (DEPTH, DEC_BATCH, POOL_HIST, POOL_WIDTH)),
        "page_table": page_table,
        "g_attn": 1.0 + nrm(ks[7], (DEPTH, D_MODEL), 0.05),
        "w_in": nrm(ks[8], (DEPTH, D_MODEL, PROJ_WIDTH), D_MODEL ** -0.5),
        "cmp_pe": nrm(ks[9], (DEPTH, BLK, 2, HEAD_DIM), 0.1),
        "cmp_w1": nrm(ks[10], (DEPTH, 2, BLK, HEAD_DIM, CMP_HIDDEN), (BLK * HEAD_DIM) ** -0.5),
        "cmp_w2": nrm(ks[11], (DEPTH, 2, CMP_HIDDEN, HEAD_DIM), (2.0 / CMP_HIDDEN) ** 0.5),
        "w_pool": nrm(ks[12], (DEPTH, N_POOL_GROUPS, POOL_GROUP, POOL_GROUP), POOL_GROUP ** -0.5),
        "pool_scale": 1.0 + nrm(ks[13], (DEPTH, POOL_WIDTH), 0.1),
        "w_out": nrm(ks[14], (DEPTH, D_MODEL, D_MODEL), D_MODEL ** -0.5),
        "g_mlp": 1.0 + nrm(ks[15], (DEPTH, D_MODEL), 0.05),
        "w_up": nrm(ks[16], (DEPTH, D_MODEL, D_FF), D_MODEL ** -0.5),
        "w_down": nrm(ks[17], (DEPTH, D_FF, D_MODEL), D_FF ** -0.5),
        "g_final": 1.0 + nrm(ks[18], (D_MODEL,), 0.05),
    }


def reference(x_prompt, x_sample, cache_cmp, cache_slc, state_win, state_pool, page_table,
              g_attn, w_in, cmp_pe, cmp_w1, cmp_w2, w_pool, pool_scale, w_out,
              g_mlp, w_up, w_down, g_final):
    slopes = alibi_slopes()
    hp, hs = x_prompt, x_sample
    st_p, st_s = [], []
    for l in range(DEPTH):
        lw = (g_attn[l], w_in[l], cmp_pe[l], cmp_w1[l], cmp_w2[l], w_pool[l], pool_scale[l],
              w_out[l], g_mlp[l], w_up[l], w_down[l])
        hp, sp = prompt_layer(hp, *lw, slopes)
        hs, ss = sample_layer(hs, cache_cmp[l], cache_slc[l], state_win[l], state_pool[l],
                              page_table, *lw, slopes)
        st_p.append(sp)
        st_s.append(ss)
    y_prompt = rms_norm(hp, g_final)
    y_sample = rms_norm(hs, g_final)
    stk = lambda states, i: jnp.stack([s[i] for s in states], axis=0)
    return (y_prompt, y_sample,
            stk(st_p, 0), stk(st_p, 1), stk(st_p, 2), stk(st_p, 3),
            stk(st_s, 0), stk(st_s, 1), stk(st_s, 2), stk(st_s, 3))
```

```python
import functools
import math

import jax
import jax.numpy as jnp
from jax import lax
from jax.experimental import pallas as pl
from jax.experimental.pallas import tpu as pltpu

D_MODEL = 1024
N_HEADS = 8
N_KV = 2
HPG = N_HEADS // N_KV
HEAD_DIM = 64
ATTN_WIDTH = N_HEADS * HEAD_DIM
POOL_WIDTH = D_MODEL - ATTN_WIDTH
POOL_WINDOWS = (2, 4, 8, 16)
POOL_GROUP = POOL_WIDTH // len(POOL_WINDOWS)
POOL_HIST = max(POOL_WINDOWS) - 1
BLK = 64
N_SELECT = 16
WINDOW = 512
CMP_HIDDEN = 2 * HEAD_DIM
D_FF = 4 * D_MODEL
KV_WIDTH = N_KV * 2 * HEAD_DIM
PAGE_SIZE = 128
BLOCKS_PER_PAGE = PAGE_SIZE // BLK
SCALE = HEAD_DIM ** -0.5
FORCE_SCORE = float(HPG + 1)
NORM_EPS = 1e-6
NEG = -1e30
N_GATE_PAD = 128
HALO = 16

F32 = jnp.float32
BF16 = jnp.bfloat16
NT_DIMS = (((1,), (1,)), ((), ()))

V7X_VMEM_LIMIT = 56 * 1024 * 1024


def _slope(head):
    return float(2.0 ** (-8.0 * (head + 1) / N_HEADS))


def _cparams(sem, vmem=None):
    return pltpu.CompilerParams(dimension_semantics=sem, vmem_limit_bytes=vmem)


def _proj_kernel(x_ref, g_ref, wtok_ref, wkv_ref, q_ref, gate_ref, u_ref, kvt_ref, kvtb_ref, *rest,
                 tm, emit_tok_kv, emit_pages):
    x = x_ref[0]
    ms = jnp.mean(x * x, axis=-1, keepdims=True)
    xn = ((x * lax.rsqrt(ms + NORM_EPS)) * g_ref[...]).astype(BF16)
    tok = lax.dot_general(xn, wtok_ref[...], NT_DIMS, preferred_element_type=F32)
    q_ref[0] = (tok[:, :ATTN_WIDTH] * SCALE).astype(BF16)
    u_ref[0] = tok[:, ATTN_WIDTH:ATTN_WIDTH + POOL_WIDTH]
    gl = tok[:, ATTN_WIDTH + POOL_WIDTH:ATTN_WIDTH + POOL_WIDTH + N_GATE_PAD]
    gate_ref[0] = 1.0 / (1.0 + jnp.exp(-gl))
    kvt = lax.dot_general(wkv_ref[...], xn, NT_DIMS, preferred_element_type=F32)
    for br in range(3):
        kvt_ref[br, 0] = kvt[br * KV_WIDTH:(br + 1) * KV_WIDTH]
    for br in range(2):
        kvtb_ref[br, 0] = kvt[(br + 1) * KV_WIDTH:(br + 2) * KV_WIDTH].astype(BF16)
    k = 0
    if emit_tok_kv:
        base = ATTN_WIDTH + POOL_WIDTH + N_GATE_PAD
        rest[k][0] = tok[:, base:base + 3 * KV_WIDTH]
        k += 1
    if emit_pages:
        for pg in range(tm // PAGE_SIZE):
            rest[k][pg] = kvt[:KV_WIDTH, pg * PAGE_SIZE:(pg + 1) * PAGE_SIZE]


def _proj(x, g_attn, wtok, wkv, *, tm, emit_tok_kv, emit_pages):
    B, T, _ = x.shape
    nt = T // tm
    out_shape = [
        jax.ShapeDtypeStruct((B, T, ATTN_WIDTH), BF16),
        jax.ShapeDtypeStruct((B, T, N_GATE_PAD), F32),
        jax.ShapeDtypeStruct((B, T, POOL_WIDTH), F32),
        jax.ShapeDtypeStruct((3, B, KV_WIDTH, T), F32),
        jax.ShapeDtypeStruct((2, B, KV_WIDTH, T), BF16),
    ]
    out_specs = [
        pl.BlockSpec((1, tm, ATTN_WIDTH), lambda b, i: (b, i, 0)),
        pl.BlockSpec((1, tm, N_GATE_PAD), lambda b, i: (b, i, 0)),
        pl.BlockSpec((1, tm, POOL_WIDTH), lambda b, i: (b, i, 0)),
        pl.BlockSpec((3, 1, KV_WIDTH, tm), lambda b, i: (0, b, 0, i)),
        pl.BlockSpec((2, 1, KV_WIDTH, tm), lambda b, i: (0, b, 0, i)),
    ]
    if emit_tok_kv:
        out_shape.append(jax.ShapeDtypeStruct((B, T, 3 * KV_WIDTH), F32))
        out_specs.append(pl.BlockSpec((1, tm, 3 * KV_WIDTH), lambda b, i: (b, i, 0)))
    if emit_pages:
        ppt = tm // PAGE_SIZE
        out_shape.append(jax.ShapeDtypeStruct((B * T // PAGE_SIZE, KV_WIDTH, PAGE_SIZE), F32))
        out_specs.append(pl.BlockSpec((ppt, KV_WIDTH, PAGE_SIZE), lambda b, i: (b * nt + i, 0, 0)))
    return pl.pallas_call(
        functools.partial(_proj_kernel, tm=tm, emit_tok_kv=emit_tok_kv, emit_pages=emit_pages),
        grid=(B, nt),
        in_specs=[
            pl.BlockSpec((1, tm, D_MODEL), lambda b, i: (b, i, 0)),
            pl.BlockSpec((1, D_MODEL), lambda b, i: (0, 0)),
            pl.BlockSpec(wtok.shape, lambda b, i: (0, 0)),
            pl.BlockSpec(wkv.shape, lambda b, i: (0, 0)),
        ],
        out_specs=out_specs,
        out_shape=out_shape,
        compiler_params=_cparams(("arbitrary", "arbitrary"), V7X_VMEM_LIMIT),
        name="proj",
    )(x, g_attn, wtok, wkv)


def _gelu_tanh(x):
    c = math.sqrt(2.0 / math.pi)
    return 0.5 * x * (1.0 + jnp.tanh(c * (x + 0.044715 * (x * x * x))))


def _compress_kernel(ids_ref, src_ref, pet_ref, w1_ref, w2_ref, out_ref, buf, sem, *, P, n_steps):
    i = pl.program_id(0)

    def page_copy(step, slot, p):
        return pltpu.make_async_copy(src_ref.at[ids_ref[step * P + p]],
                                     buf.at[slot, pl.ds(p * KV_WIDTH, KV_WIDTH)], sem.at[slot])

    def start(step, slot):
        for p in range(P):
            page_copy(step, slot, p).start()

    @pl.when(i == 0)
    def _():
        start(0, 0)

    @pl.when(i + 1 < n_steps)
    def _():
        start(i + 1, (i + 1) % 2)

    slot = i % 2
    for p in range(P):
        page_copy(i, slot, p).wait()

    bref = buf.at[slot]
    for c in range(2):
        acc = jnp.zeros((N_KV * P, 2 * CMP_HIDDEN), F32)
        for dp in range(HEAD_DIM // 2):
            rows = []
            for g in range(N_KV):
                halves = []
                for dd in range(2):
                    r = g * 2 * HEAD_DIM + c * HEAD_DIM + 2 * dp + dd
                    pr = c * HEAD_DIM + 2 * dp + dd
                    halves.append(bref[pl.ds(r, P, stride=KV_WIDTH), :] + pet_ref[pr:pr + 1, :])
                rows.append(jnp.concatenate(halves, axis=1))
            lhs = jnp.concatenate(rows, axis=0).astype(BF16)
            acc = acc + jnp.dot(lhs, w1_ref[c, dp], preferred_element_type=F32)
        hid = _gelu_tanh(acc).astype(BF16)
        oc = jnp.dot(hid, w2_ref[c], preferred_element_type=F32).astype(BF16)
        for g in range(N_KV):
            out_ref[g, c] = oc[g * P:(g + 1) * P]


def _compress(ids, src, pet, w1bd, w2bd, *, P):
    n_total = ids.shape[0]
    n_steps = n_total // P
    grid_spec = pltpu.PrefetchScalarGridSpec(
        num_scalar_prefetch=1,
        grid=(n_steps,),
        in_specs=[
            pl.BlockSpec(memory_space=pl.ANY),
            pl.BlockSpec(pet.shape, lambda i, ids: (0, 0)),
            pl.BlockSpec(w1bd.shape, lambda i, ids: (0, 0, 0, 0)),
            pl.BlockSpec(w2bd.shape, lambda i, ids: (0, 0, 0)),
        ],
        out_specs=pl.BlockSpec((N_KV, 2, P, PAGE_SIZE), lambda i, ids: (0, 0, i, 0)),
        scratch_shapes=[
            pltpu.VMEM((2, P * KV_WIDTH, PAGE_SIZE), F32),
            pltpu.SemaphoreType.DMA((2,)),
        ],
    )
    return pl.pallas_call(
        functools.partial(_compress_kernel, P=P, n_steps=n_steps),
        grid_spec=grid_spec,
        out_shape=jax.ShapeDtypeStruct((N_KV, 2, n_total, PAGE_SIZE), BF16),
        compiler_params=_cparams(("arbitrary",), V7X_VMEM_LIMIT),
        name="compress",
    )(ids, src, pet, w1bd, w2bd)


def _compress_weights(cmp_pe, cmp_w1, cmp_w2):
    eye = jnp.eye(BLOCKS_PER_PAGE, dtype=F32)
    pet = jnp.tile(jnp.transpose(cmp_pe, (1, 2, 0)).reshape(2 * HEAD_DIM, BLK), (1, BLOCKS_PER_PAGE))
    w1 = cmp_w1.reshape(2, BLK, HEAD_DIM // 2, 2, CMP_HIDDEN)
    w1bd = jnp.einsum('cluzh,jk->cuzjlkh', w1, eye).reshape(
        2, HEAD_DIM // 2, 2 * PAGE_SIZE, BLOCKS_PER_PAGE * CMP_HIDDEN).astype(BF16)
    w2bd = jnp.einsum('chd,jk->cjhkd', cmp_w2, eye).reshape(
        2, BLOCKS_PER_PAGE * CMP_HIDDEN, BLOCKS_PER_PAGE * HEAD_DIM).astype(BF16)
    return pet, w1bd, w2bd


def _rank_select(score, idx_fn, n_select, period):
    W = score.shape[1]
    lane = lax.broadcasted_iota(jnp.int32, (1, W), 1)
    idx = idx_fn(lane)
    rank = jnp.zeros(score.shape, jnp.int32)
    for r in range(1, period):
        y = pltpu.roll(score, r, 1)
        yi = idx_fn((lane - r) & (W - 1))
        ahead = (y > score) | ((y == score) & (yi < idx))
        rank = rank + ahead.astype(jnp.int32)
    return rank < n_select


def _online_update(s, mask, m_ref, l_ref, rows):
    s = jnp.where(mask, s, NEG)
    m_old = m_ref[rows, :]
    m_new = jnp.maximum(m_old, jnp.max(s, axis=1, keepdims=True))
    alpha = jnp.exp(m_old - m_new)
    p = jnp.where(mask, jnp.exp(s - m_new), 0.0)
    l_ref[rows, :] = alpha * l_ref[rows, :] + jnp.sum(p, axis=1, keepdims=True)
    m_ref[rows, :] = m_new
    return p, alpha


def _p_attn_kernel(q_ref, gate_ref, comp_ref, ks_ref, kw_ref, o_ref,
                   qs, m_ref, l_ref, acc_ref, out_ref, *, tq, tk, npg):
    qi = pl.program_id(1)
    q0 = qi * tq
    pos_t = q0 + lax.broadcasted_iota(jnp.int32, (tq, 1), 0)
    pos_tf = pos_t.astype(F32)
    nblk = npg * BLOCKS_PER_PAGE
    W = 128

    for hh in range(N_HEADS):
        qs[hh] = q_ref[0, :, hh * HEAD_DIM:(hh + 1) * HEAD_DIM]

    def blk_of(c):
        cm = c % nblk
        return jnp.where(cm < npg, 2 * cm, 2 * (cm - npg) + 1)

    col = lax.broadcasted_iota(jnp.int32, (1, W), 1)
    blk_n = blk_of(col)
    first = col < nblk
    end_pos = (blk_n + 1) * BLK - 1
    dist_c = (pos_t - end_pos).astype(F32)
    mask_c = dist_c >= 0
    cur = pos_t // BLK
    reps = W // nblk

    sel_masks = []
    for g in range(N_KV):
        ck = comp_ref[g, 0].astype(F32)
        cv = comp_ref[g, 1].astype(F32)
        ckp = jnp.concatenate([ck[:, :HEAD_DIM], ck[:, HEAD_DIM:]] * reps, axis=0).astype(BF16)
        cvp = jnp.concatenate([cv[:, :HEAD_DIM], cv[:, HEAD_DIM:]], axis=0)
        if reps > 1:
            cvp = jnp.concatenate([cvp, jnp.zeros((W - nblk, HEAD_DIM), F32)], axis=0)
        cvp = cvp.astype(BF16)
        imp = jnp.zeros((tq, W), F32)
        for h in range(HPG):
            hh = g * HPG + h
            s = lax.dot_general(qs[hh], ckp, NT_DIMS, preferred_element_type=F32)
            s = s - _slope(hh) * dist_c
            s = jnp.where(mask_c, s, NEG)
            mx = jnp.max(s, axis=1, keepdims=True)
            e = jnp.where(mask_c, jnp.exp(s - mx), 0.0)
            den = jnp.sum(jnp.where(first, e, 0.0), axis=1, keepdims=True)
            p = e / jnp.maximum(den, 1e-30)
            imp = imp + p
            oc = jnp.dot(p.astype(BF16), cvp, preferred_element_type=F32)
            gcol = gate_ref[0, :, hh:hh + 1]
            out_ref[:, hh * HEAD_DIM:(hh + 1) * HEAD_DIM] = gcol * oc
        forced = (blk_n == 0) | (blk_n == cur) | (blk_n == cur - 1)
        score = jnp.where(forced, FORCE_SCORE, imp)
        score = jnp.where(blk_n > cur, -1.0, score)
        sel = _rank_select(score, blk_of, N_SELECT, nblk)
        sel_masks.append(jnp.where(sel, 1.0, 0.0).astype(BF16))

    col_s = lax.broadcasted_iota(jnp.int32, (W, 1), 0)
    col_blk = blk_of(col_s)
    col_ok = col_s < nblk

    def branch(kv_ref, j_lo, j_hi, tkk, mask_fn, gate_base):
        R = HPG * tq
        for g in range(N_KV):
            m_ref[g] = jnp.full((R, 1), NEG, F32)
            l_ref[g] = jnp.zeros((R, 1), F32)
            acc_ref[g] = jnp.zeros((R, HEAD_DIM), F32)

        def body(j, carry):
            k0 = pl.multiple_of(j * tkk, tkk)
            kpos = k0 + lax.broadcasted_iota(jnp.int32, (1, tkk), 1)
            dist = pos_t - kpos
            distf = dist.astype(F32)
            for g in range(N_KV):
                kt = kv_ref[0, 0, g * 2 * HEAD_DIM:g * 2 * HEAD_DIM + HEAD_DIM, pl.ds(k0, tkk)]
                vt = kv_ref[0, 0, g * 2 * HEAD_DIM + HEAD_DIM:(g + 1) * 2 * HEAD_DIM, pl.ds(k0, tkk)]
                mask = mask_fn(g, dist, kpos)
                ps, alphas = [], []
                for h in range(HPG):
                    hh = g * HPG + h
                    s = jnp.dot(qs[hh], kt, preferred_element_type=F32) - _slope(hh) * distf
                    p, alpha = _online_update(s, mask, m_ref.at[g], l_ref.at[g], pl.ds(h * tq, tq))
                    ps.append(p.astype(BF16))
                    alphas.append(alpha)
                pv = lax.dot_general(jnp.concatenate(ps, axis=0), vt, NT_DIMS, preferred_element_type=F32)
                acc_ref[g] = jnp.concatenate(alphas, axis=0) * acc_ref[g] + pv
            return carry

        lax.fori_loop(j_lo, j_hi, body, 0)
        for g in range(N_KV):
            for h in range(HPG):
                hh = g * HPG + h
                rows = pl.ds(h * tq, tq)
                o = acc_ref[g, rows, :] / jnp.maximum(l_ref[g, rows, :], 1e-30)
                gcol = gate_ref[0, :, gate_base + hh:gate_base + hh + 1]
                out_ref[:, hh * HEAD_DIM:(hh + 1) * HEAD_DIM] += gcol * o

    def slc_mask(g, dist, kpos):
        kblk = kpos // BLK
        expand = jnp.where((col_blk == kblk) & col_ok, 1.0, 0.0).astype(BF16)
        selx = jnp.dot(sel_masks[g], expand, preferred_element_type=F32)
        return (selx > 0.5) & (dist >= 0)

    def win_mask(g, dist, kpos):
        return (dist >= 0) & (dist < WINDOW)

    branch(ks_ref, 0, (q0 + tq + tk - 1) // tk, tk, slc_mask, N_HEADS)
    branch(kw_ref, jnp.maximum(q0 - WINDOW, 0) // tq, qi + 1, tq, win_mask, 2 * N_HEADS)
    o_ref[0] = out_ref[...].astype(BF16)


def _p_attn(q, gates, comp, kvtb, *, tq, tk):
    B, T, _ = q.shape
    npg = T // PAGE_SIZE
    return pl.pallas_call(
        functools.partial(_p_attn_kernel, tq=tq, tk=tk, npg=npg),
        grid=(B, T // tq),
        in_specs=[
            pl.BlockSpec((1, tq, ATTN_WIDTH), lambda b, i: (b, i, 0)),
            pl.BlockSpec((1, tq, N_GATE_PAD), lambda b, i: (b, i, 0)),
            pl.BlockSpec((N_KV, 2, npg, PAGE_SIZE), lambda b, i: (0, 0, b, 0)),
            pl.BlockSpec((1, 1, KV_WIDTH, T), lambda b, i: (0, b, 0, 0)),
            pl.BlockSpec((1, 1, KV_WIDTH, T), lambda b, i: (1, b, 0, 0)),
        ],
        out_specs=pl.BlockSpec((1, tq, ATTN_WIDTH), lambda b, i: (b, i, 0)),
        out_shape=jax.ShapeDtypeStruct((B, T, ATTN_WIDTH), BF16),
        scratch_shapes=[
            pltpu.VMEM((N_HEADS, tq, HEAD_DIM), BF16),
            pltpu.VMEM((N_KV, HPG * tq, 1), F32),
            pltpu.VMEM((N_KV, HPG * tq, 1), F32),
            pltpu.VMEM((N_KV, HPG * tq, HEAD_DIM), F32),
            pltpu.VMEM((tq, ATTN_WIDTH), F32),
        ],
        compiler_params=_cparams(("arbitrary", "arbitrary"), V7X_VMEM_LIMIT),
        name="p_attn",
    )(q, gates, comp, kvtb, kvtb)


def _stack_heads(q_ref, g, ts):
    parts = [q_ref[0, :, (g * HPG + h) * HEAD_DIM:(g * HPG + h + 1) * HEAD_DIM].astype(F32) for h in range(HPG)]
    return jnp.concatenate(parts, axis=0).astype(BF16)


def _row_consts(g, ts):
    R = HPG * ts
    row = lax.broadcasted_iota(jnp.int32, (R, 1), 0)
    t_row = row % ts
    h_row = row // ts
    slope = jnp.zeros((R, 1), F32)
    for h in range(HPG):
        slope = jnp.where(h_row == h, _slope(g * HPG + h), slope)
    return t_row, slope


def _s_select_kernel(q_ref, gate_ref, comp_ref, cnew_ref, oc_ref, bits_ref, out_s, *, past, ts):
    W = 128
    nb_past = past // BLK
    lane = lax.broadcasted_iota(jnp.int32, (1, W), 1)
    idx_fns = [lambda c: 2 * c, lambda c: 2 * c + 1, lambda c: jnp.where(c == 0, nb_past, (1 << 20) + c)]
    idx_tiles = [f(lane) for f in idx_fns]
    valid_n = lane == 0
    tok = lax.broadcasted_iota(jnp.int32, (ts, 1), 0)
    cur_t = (past + tok) // BLK
    for g in range(N_KV):
        qg = _stack_heads(q_ref, g, ts)
        t_row, slope = _row_consts(g, ts)
        pos = past + t_row
        ck = comp_ref[g, 0]
        cv = comp_ref[g, 1]
        cn = cnew_ref[g, 0, 0].astype(F32)[:, :HEAD_DIM]
        vn = cnew_ref[g, 1, 0].astype(F32)[:, :HEAD_DIM]
        s_t = [lax.dot_general(qg, ck[:, :HEAD_DIM], NT_DIMS, preferred_element_type=F32),
               lax.dot_general(qg, ck[:, HEAD_DIM:], NT_DIMS, preferred_element_type=F32),
               jnp.broadcast_to(jnp.sum(qg.astype(F32) * cn, axis=1, keepdims=True), (HPG * ts, W))]
        masks, es = [], []
        for k in range(3):
            end_pos = (idx_tiles[k] + 1) * BLK - 1
            dist = (pos - end_pos).astype(F32)
            mk = dist >= 0
            if k == 2:
                mk = mk & valid_n
            masks.append(mk)
            s_t[k] = jnp.where(mk, s_t[k] - slope * dist, NEG)
        mx = jnp.maximum(jnp.maximum(jnp.max(s_t[0], axis=1, keepdims=True),
                                     jnp.max(s_t[1], axis=1, keepdims=True)),
                         jnp.max(s_t[2], axis=1, keepdims=True))
        for k in range(3):
            es.append(jnp.where(masks[k], jnp.exp(s_t[k] - mx), 0.0))
        den = (jnp.sum(es[0], axis=1, keepdims=True) + jnp.sum(es[1], axis=1, keepdims=True)
               + jnp.sum(es[2], axis=1, keepdims=True))
        inv = 1.0 / jnp.maximum(den, 1e-30)
        ps = [e * inv for e in es]
        o_c = (jnp.dot(ps[0].astype(BF16), cv[:, :HEAD_DIM], preferred_element_type=F32)
               + jnp.dot(ps[1].astype(BF16), cv[:, HEAD_DIM:], preferred_element_type=F32)
               + ps[2][:, 0:1].astype(BF16).astype(F32) * vn)
        for h in range(HPG):
            hh = g * HPG + h
            gcol = gate_ref[0, :, hh:hh + 1]
            out_s[:, hh * HEAD_DIM:(hh + 1) * HEAD_DIM] = gcol * o_c[h * ts:(h + 1) * ts]
        scores = []
        for k in range(3):
            imp = ps[k][0:ts]
            for h in range(1, HPG):
                imp = imp + ps[k][h * ts:(h + 1) * ts]
            idx = idx_tiles[k]
            forced = (idx == 0) | (idx == cur_t) | (idx == cur_t - 1)
            sc = jnp.where(forced, FORCE_SCORE, imp)
            sc = jnp.where(idx > cur_t, -1.0, sc)
            if k == 2:
                sc = jnp.where(valid_n, sc, -2.0)
            scores.append(sc)
        ranks = [jnp.zeros((ts, W), jnp.int32) for _ in range(3)]
        for kb in range(3):
            for r in range(W):
                y = pltpu.roll(scores[kb], r, 1) if r else scores[kb]
                yi = idx_fns[kb]((lane - r) & (W - 1))
                for ka in range(3):
                    if ka == kb and r == 0:
                        continue
                    ahead = (y > scores[ka]) | ((y == scores[ka]) & (yi < idx_tiles[ka]))
                    ranks[ka] = ranks[ka] + ahead.astype(jnp.int32)
        sel = [(ranks[k] < N_SELECT) for k in range(3)]
        wt = jnp.left_shift(1, 2 * tok)
        page_bits = jnp.sum(jnp.where(sel[0], wt, 0) + jnp.where(sel[1], 2 * wt, 0), axis=0, keepdims=True)
        new_bits = jnp.sum(jnp.where(sel[2] & valid_n, jnp.left_shift(1, tok), 0), axis=0, keepdims=True)
        bits_ref[0, g, 0:1, :] = page_bits
        bits_ref[0, g, 1:2, :] = new_bits
    oc_ref[0] = out_s[...]


def _s_select(q, gates, comp, cnew, *, past):
    Bs, ts, _ = q.shape
    npg = past // PAGE_SIZE
    assert npg == 128, "one lane tile of pages per sequence"
    return pl.pallas_call(
        functools.partial(_s_select_kernel, past=past, ts=ts),
        grid=(Bs,),
        in_specs=[
            pl.BlockSpec((1, ts, ATTN_WIDTH), lambda b: (b, 0, 0)),
            pl.BlockSpec((1, ts, N_GATE_PAD), lambda b: (b, 0, 0)),
            pl.BlockSpec((N_KV, 2, npg, PAGE_SIZE), lambda b: (0, 0, b, 0)),
            pl.BlockSpec((N_KV, 2, 1, 1, PAGE_SIZE), lambda b: (0, 0, b, 0, 0)),
        ],
        out_specs=[
            pl.BlockSpec((1, ts, ATTN_WIDTH), lambda b: (b, 0, 0)),
            pl.BlockSpec((1, N_KV, 2, 128), lambda b: (b, 0, 0, 0)),
        ],
        out_shape=[
            jax.ShapeDtypeStruct((Bs, ts, ATTN_WIDTH), F32),
            jax.ShapeDtypeStruct((Bs, N_KV, 2, 128), jnp.int32),
        ],
        scratch_shapes=[pltpu.VMEM((ts, ATTN_WIDTH), F32)],
        compiler_params=_cparams(("arbitrary",), V7X_VMEM_LIMIT),
        name="s_select",
    )(q, gates, comp, cnew)


def _s_attn_kernel(pt_ref, bits_ref, q_ref, gate_ref, oc_ref, win_ref, knew_ref, slc_ref,
                   o_ref, wout_ref, buf, sem, plist, out_s, *, past, ts, npg):
    b = pl.program_id(0)
    g = pl.program_id(1)
    W = 128
    R = HPG * ts
    wk = win_ref.shape[2]
    base = (b * N_KV + g) * 2 * W

    def page_copy(page, slot):
        return pltpu.make_async_copy(slc_ref.at[page, pl.ds(g * 2 * HEAD_DIM, 2 * HEAD_DIM)], buf.at[slot], sem.at[0])

    def issue(p, cnt):
        hit = bits_ref[base + p] != 0

        @pl.when(hit)
        def _():
            page_copy(pt_ref[b * npg + p], cnt).start()
            plist[cnt] = p

        return cnt + hit.astype(jnp.int32)

    cnt = lax.fori_loop(0, npg, issue, 0)

    q_all = q_ref[0].astype(F32)
    qg = jnp.concatenate([q_all[:, h * HEAD_DIM:(h + 1) * HEAD_DIM] for h in range(HPG)], axis=0).astype(BF16)
    row = lax.broadcasted_iota(jnp.int32, (R, 1), 0)
    t_row = row % ts
    head_row = row // ts + g * HPG
    slope = jnp.zeros((R, 1), F32)
    for hh in range(N_HEADS):
        slope = jnp.where(head_row == hh, _slope(hh), slope)
    lane = lax.broadcasted_iota(jnp.int32, (1, W), 1)

    kw = win_ref[0, 0:HEAD_DIM, :].astype(BF16)
    vw = win_ref[0, HEAD_DIM:2 * HEAD_DIM, :].astype(BF16)
    kn = knew_ref[2, 0, 0:HEAD_DIM, :].astype(BF16)
    vn = knew_ref[2, 0, HEAD_DIM:2 * HEAD_DIM, :].astype(BF16)
    i_st = lax.broadcasted_iota(jnp.int32, (1, wk), 1)
    d_st = t_row + wk - i_st
    m_st = (d_st >= 0) & (d_st < WINDOW)
    d_nw = t_row - lane
    m_nw = (d_nw >= 0) & (d_nw < WINDOW) & (lane < ts)
    s_st = jnp.where(m_st, jnp.dot(qg, kw, preferred_element_type=F32) - slope * d_st.astype(F32), NEG)
    s_nw = jnp.where(m_nw, jnp.dot(qg, kn, preferred_element_type=F32) - slope * d_nw.astype(F32), NEG)
    mx = jnp.maximum(jnp.max(s_st, axis=1, keepdims=True), jnp.max(s_nw, axis=1, keepdims=True))
    e_st = jnp.where(m_st, jnp.exp(s_st - mx), 0.0)
    e_nw = jnp.where(m_nw, jnp.exp(s_nw - mx), 0.0)
    den = jnp.sum(e_st, axis=1, keepdims=True) + jnp.sum(e_nw, axis=1, keepdims=True)
    inv = 1.0 / jnp.maximum(den, 1e-30)
    o_w = (lax.dot_general((e_st * inv).astype(BF16), vw, NT_DIMS, preferred_element_type=F32)
           + lax.dot_general((e_nw * inv).astype(BF16), vn, NT_DIMS, preferred_element_type=F32))

    last = wk - W
    shifted = pltpu.roll(win_ref[0], wk - ts, 1)
    newr = pltpu.roll(knew_ref[2, 0], W - ts, 1)
    wout_ref[0, :, 0:last] = shifted[:, 0:last]
    wout_ref[0, :, last:wk] = jnp.where(lane >= W - ts, newr, shifted[:, last:wk])

    bits_new = bits_ref[base + W]
    ksn = knew_ref[1, 0, 0:HEAD_DIM, :].astype(BF16)
    vsn = knew_ref[1, 0, HEAD_DIM:2 * HEAD_DIM, :].astype(BF16)
    d_sn = t_row - lane
    m_sn = (d_sn >= 0) & (lane < ts) & ((jnp.right_shift(bits_new, t_row) & 1) == 1)
    s_sn = jnp.where(m_sn, jnp.dot(qg, ksn, preferred_element_type=F32) - slope * d_sn.astype(F32), NEG)
    m0 = jnp.max(s_sn, axis=1, keepdims=True)
    e0 = jnp.where(m_sn, jnp.exp(s_sn - m0), 0.0)
    l0 = jnp.sum(e0, axis=1, keepdims=True)
    a0 = lax.dot_general(e0.astype(BF16), vsn, NT_DIMS, preferred_element_type=F32)

    def wait_one(s, c):
        page_copy(0, s).wait()
        return c

    lax.fori_loop(0, cnt, wait_one, 0)
    sh = 2 * t_row + (lane >= BLK).astype(jnp.int32)

    def page_step(s, carry):
        m, l, acc = carry
        p = plist[s]
        bits = bits_ref[base + p]
        kt = buf[s, 0:HEAD_DIM, :].astype(BF16)
        vt = buf[s, HEAD_DIM:2 * HEAD_DIM, :].astype(BF16)
        dist = past + t_row - (p * PAGE_SIZE + lane)
        mask = ((jnp.right_shift(bits, sh) & 1) == 1) & (dist >= 0)
        sc = jnp.where(mask, jnp.dot(qg, kt, preferred_element_type=F32) - slope * dist.astype(F32), NEG)
        m_new = jnp.maximum(m, jnp.max(sc, axis=1, keepdims=True))
        alpha = jnp.exp(m - m_new)
        e = jnp.where(mask, jnp.exp(sc - m_new), 0.0)
        l = alpha * l + jnp.sum(e, axis=1, keepdims=True)
        acc = alpha * acc + lax.dot_general(e.astype(BF16), vt, NT_DIMS, preferred_element_type=F32)
        return m_new, l, acc

    m, l, acc = lax.fori_loop(0, cnt, page_step, (m0, l0, a0))
    o_s = acc / jnp.maximum(l, 1e-30)

    g_vec = jnp.zeros((ts, 1), jnp.int32) + g
    for h in range(HPG):
        rows = slice(h * ts, (h + 1) * ts)
        g_s = jnp.zeros((ts, 1), F32)
        g_w = jnp.zeros((ts, 1), F32)
        for gg in range(N_KV):
            hh = gg * HPG + h
            g_s = jnp.where(g_vec == gg, gate_ref[0, :, N_HEADS + hh:N_HEADS + hh + 1], g_s)
            g_w = jnp.where(g_vec == gg, gate_ref[0, :, 2 * N_HEADS + hh:2 * N_HEADS + hh + 1], g_w)
        out_s[:, h * HEAD_DIM:(h + 1) * HEAD_DIM] = g_s * o_s[rows] + g_w * o_w[rows]
    o_ref[0] = oc_ref[0] + out_s[...]


def _s_attn(page_table, bits, q, gates, oc, win_t, knew, slc_t, *, past):
    Bs, ts, _ = q.shape
    npg = past // PAGE_SIZE
    wk = win_t.shape[2]
    gw = HPG * HEAD_DIM
    grid_spec = pltpu.PrefetchScalarGridSpec(
        num_scalar_prefetch=2,
        grid=(Bs, N_KV),
        in_specs=[
            pl.BlockSpec((1, ts, gw), lambda b, g, *_: (b, 0, g)),
            pl.BlockSpec((1, ts, N_GATE_PAD), lambda b, g, *_: (b, 0, 0)),
            pl.BlockSpec((1, ts, gw), lambda b, g, *_: (b, 0, g)),
            pl.BlockSpec((1, 2 * HEAD_DIM, wk), lambda b, g, *_: (b, g, 0)),
            pl.BlockSpec((3, 1, 2 * HEAD_DIM, PAGE_SIZE), lambda b, g, *_: (0, b, g, 0)),
            pl.BlockSpec(memory_space=pl.ANY),
        ],
        out_specs=[
            pl.BlockSpec((1, ts, gw), lambda b, g, *_: (b, 0, g)),
            pl.BlockSpec((1, 2 * HEAD_DIM, wk), lambda b, g, *_: (b, g, 0)),
        ],
        scratch_shapes=[
            pltpu.VMEM((npg, 2 * HEAD_DIM, PAGE_SIZE), F32),
            pltpu.SemaphoreType.DMA((1,)),
            pltpu.SMEM((npg,), jnp.int32),
            pltpu.VMEM((ts, gw), F32),
        ],
    )
    return pl.pallas_call(
        functools.partial(_s_attn_kernel, past=past, ts=ts, npg=npg),
        grid_spec=grid_spec,
        out_shape=[
            jax.ShapeDtypeStruct((Bs, ts, ATTN_WIDTH), F32),
            jax.ShapeDtypeStruct(win_t.shape, F32),
        ],
        compiler_params=_cparams(("arbitrary", "arbitrary"), V7X_VMEM_LIMIT),
        name="s_attn",
    )(page_table.reshape(-1), bits.reshape(-1), q, gates, oc, win_t, knew, slc_t)


def _pool_kernel(u_ref, halo_ref, wp_ref, ps_ref, o_ref, ext, *, tp, tiles_per_seq, pos_base):
    i = pl.program_id(0)
    ext[0:HALO, :] = halo_ref[0]
    ext[HALO:HALO + tp, :] = u_ref[0]
    pos = pos_base + (i % tiles_per_seq) * tp + lax.broadcasted_iota(jnp.int32, (tp, 1), 0)
    for gi, w in enumerate(POOL_WINDOWS):
        lanes = slice(gi * POOL_GROUP, (gi + 1) * POOL_GROUP)
        own = ext[HALO:HALO + tp, lanes]
        tot = own
        for k in range(1, w):
            tot = tot + ext[HALO - k:HALO - k + tp, lanes]
        cnt = jnp.minimum(w, pos + 1).astype(F32)
        d = (tot / cnt - own).astype(BF16)
        y = jnp.dot(d, wp_ref[gi], preferred_element_type=F32)
        o_ref[0, :, lanes] = (y * ps_ref[:, lanes]).astype(o_ref.dtype)


def _pool(u, halo, w_pool, pool_scale, *, tiles_per_seq, pos_base, out_dtype):
    n_tiles, tp, _ = u.shape
    return pl.pallas_call(
        functools.partial(_pool_kernel, tp=tp, tiles_per_seq=tiles_per_seq, pos_base=pos_base),
        grid=(n_tiles,),
        in_specs=[
            pl.BlockSpec((1, tp, POOL_WIDTH), lambda i: (i, 0, 0)),
            pl.BlockSpec((1, HALO, POOL_WIDTH), lambda i: (i, 0, 0)),
            pl.BlockSpec(w_pool.shape, lambda i: (0, 0, 0)),
            pl.BlockSpec((1, POOL_WIDTH), lambda i: (0, 0)),
        ],
        out_specs=pl.BlockSpec((1, tp, POOL_WIDTH), lambda i: (i, 0, 0)),
        out_shape=jax.ShapeDtypeStruct((n_tiles, tp, POOL_WIDTH), out_dtype),
        scratch_shapes=[pltpu.VMEM((HALO + tp, POOL_WIDTH), F32)],
        compiler_params=_cparams(("arbitrary",), V7X_VMEM_LIMIT),
        name="pool",
    )(u, halo, w_pool, pool_scale)


def _rms(x, g):
    return (x * lax.rsqrt(jnp.mean(x * x, axis=-1, keepdims=True) + NORM_EPS)) * g


def _mlp_kernel(x_ref, o_ref, p_ref, wo_ref, gm_ref, wu_ref, wd_ref, gf_ref, y_ref, *, ff_chunk):
    mix = jnp.concatenate([o_ref[...], p_ref[...]], axis=1)
    h = x_ref[...] + jnp.dot(mix, wo_ref[...], preferred_element_type=F32)
    hn = _rms(h, gm_ref[...]).astype(BF16)
    acc = jnp.zeros(h.shape, F32)
    for c in range(D_FF // ff_chunk):
        a = jnp.dot(hn, wu_ref[:, c * ff_chunk:(c + 1) * ff_chunk], preferred_element_type=F32)
        a = jnp.square(jnp.maximum(a, 0.0)).astype(BF16)
        acc = acc + jnp.dot(a, wd_ref[c * ff_chunk:(c + 1) * ff_chunk, :], preferred_element_type=F32)
    y_ref[...] = _rms(h + acc, gf_ref[...])


def _mlp(x, o, p, w_out, g_mlp, w_up, w_down, g_final, *, tm):
    N = x.shape[0]
    const = lambda i: (0, 0)
    resident = lambda a: pl.BlockSpec(a.shape, const, pipeline_mode=pl.Buffered(1))
    return pl.pallas_call(
        functools.partial(_mlp_kernel, ff_chunk=1024),
        grid=(N // tm,),
        in_specs=[
            pl.BlockSpec((tm, D_MODEL), lambda i: (i, 0)),
            pl.BlockSpec((tm, ATTN_WIDTH), lambda i: (i, 0)),
            pl.BlockSpec((tm, POOL_WIDTH), lambda i: (i, 0)),
            resident(w_out), resident(g_mlp), resident(w_up), resident(w_down), resident(g_final),
        ],
        out_specs=pl.BlockSpec((tm, D_MODEL), lambda i: (i, 0)),
        out_shape=jax.ShapeDtypeStruct((N, D_MODEL), F32),
        compiler_params=_cparams(("arbitrary",), V7X_VMEM_LIMIT),
        name="mlp",
    )(x, o, p, w_out, g_mlp, w_up, w_down, g_final)


def _rows_from_kt(kt):
    B, _, S = kt.shape
    return jnp.transpose(kt.reshape(B, N_KV, 2, HEAD_DIM, S), (0, 4, 1, 2, 3))[None]


def _prep_weights(w_in, cmp_pe, cmp_w1, cmp_w2):
    w_t = jnp.transpose(w_in)
    a, b = ATTN_WIDTH, ATTN_WIDTH + 3 * KV_WIDTH
    wq, wkv = w_t[:a], w_t[a:b]
    wg, wu = w_t[b:b + 3 * N_HEADS], w_t[b + 3 * N_HEADS:]
    wg = jnp.pad(wg, ((0, N_GATE_PAD - 3 * N_HEADS), (0, 0)))
    wtok = jnp.concatenate([wq, wu, wg], axis=0).astype(BF16)
    wtok_kv = jnp.concatenate([wq, wu, wg, wkv], axis=0).astype(BF16)
    return wtok, wtok_kv, wkv.astype(BF16), _compress_weights(cmp_pe, cmp_w1, cmp_w2)


def _prompt_path(x, wts, g_attn, w_pool, pool_scale, w_out, g_mlp, w_up, w_down, g_final):
    wtok, _, wkv, (pet, w1bd, w2bd) = wts
    B, T, _ = x.shape
    tm = min(512, T)
    q, gates, u, kvt, kvtb, pages = _proj(x, g_attn, wtok, wkv, tm=tm, emit_tok_kv=False, emit_pages=True)
    n_pages = B * T // PAGE_SIZE
    comp = _compress(jnp.arange(n_pages, dtype=jnp.int32), pages, pet, w1bd, w2bd, P=min(64, n_pages))
    o = _p_attn(q, gates, comp, kvtb, tq=128, tk=256)
    nt = T // tm
    u4 = u.reshape(B, nt, tm, POOL_WIDTH)
    halo = jnp.concatenate([jnp.zeros((B, 1, HALO, POOL_WIDTH), F32), u4[:, :-1, tm - HALO:, :]], axis=1)
    pool = _pool(u.reshape(B * nt, tm, POOL_WIDTH), halo.reshape(B * nt, HALO, POOL_WIDTH),
                 w_pool, pool_scale, tiles_per_seq=nt, pos_base=0, out_dtype=BF16)
    N = B * T
    y = _mlp(x.reshape(N, D_MODEL), o.reshape(N, ATTN_WIDTH), pool.reshape(N, POOL_WIDTH),
             w_out, g_mlp, w_up, w_down, g_final, tm=tm)
    wk = min(WINDOW, T)
    return (y.reshape(B, T, D_MODEL), _rows_from_kt(kvt[0]), _rows_from_kt(kvt[1]),
            _rows_from_kt(kvt[2][:, :, T - wk:]), u[None, :, T - POOL_HIST:])


def _kt_view(rows):
    n, s = rows.shape[:2]
    return jnp.transpose(rows, (0, 2, 3, 4, 1)).reshape(n, KV_WIDTH, s)


def _sample_path(x, cache_cmp, cache_slc, state_win, state_pool, page_table, wts, g_attn,
                 w_pool, pool_scale, w_out, g_mlp, w_up, w_down, g_final):
    _, wtok_kv, wkv, (pet, w1bd, w2bd) = wts
    Bs, ts, _ = x.shape
    N = Bs * ts
    npg = page_table.shape[1]
    past = npg * PAGE_SIZE
    q, gates, u, kvt, _, kv_tok = _proj(x.reshape(1, N, D_MODEL), g_attn, wtok_kv, wkv, tm=N,
                                        emit_tok_kv=True, emit_pages=False)
    knew = jnp.pad(jnp.transpose(kvt[:, 0].reshape(3, KV_WIDTH, Bs, ts), (0, 2, 1, 3)),
                   ((0, 0), (0, 0), (0, 0), (0, PAGE_SIZE - ts)))
    comp_past = _compress(page_table.reshape(-1), _kt_view(cache_cmp), pet, w1bd, w2bd, P=min(64, Bs * npg))
    comp_new = _compress(jnp.arange(Bs, dtype=jnp.int32), knew[0], pet, w1bd, w2bd, P=Bs)
    qf = q.reshape(Bs, ts, ATTN_WIDTH).astype(F32)
    gts = gates.reshape(Bs, ts, N_GATE_PAD)
    oc, bits = _s_select(qf, gts, comp_past, comp_new.astype(F32).reshape(N_KV, 2, Bs, 1, PAGE_SIZE), past=past)
    o, win_new = _s_attn(page_table, bits, qf, gts, oc, _kt_view(state_win), knew, _kt_view(cache_slc), past=past)
    u3 = u.reshape(Bs, ts, POOL_WIDTH)
    halo = jnp.pad(state_pool, ((0, 0), (HALO - POOL_HIST, 0), (0, 0)))
    pool = _pool(u3, halo, w_pool, pool_scale, tiles_per_seq=1, pos_base=past, out_dtype=F32)
    y = _mlp(x.reshape(N, D_MODEL), o.reshape(N, ATTN_WIDTH).astype(BF16), pool.reshape(N, POOL_WIDTH).astype(BF16),
             w_out, g_mlp, w_up, w_down, g_final, tm=N)
    kv_rows = kv_tok.reshape(Bs, ts, 3, N_KV, 2, HEAD_DIM)
    s_pool = jnp.concatenate([state_pool, u3], axis=1)[None, :, ts:]
    return (y.reshape(Bs, ts, D_MODEL), kv_rows[None, :, :, 0], kv_rows[None, :, :, 1],
            _rows_from_kt(win_new), s_pool)


def kernel(x_prompt, x_sample, cache_cmp, cache_slc, state_win, state_pool, page_table, g_attn, w_in,
           cmp_pe, cmp_w1, cmp_w2, w_pool, pool_scale, w_out, g_mlp, w_up, w_down, g_final):
    assert g_attn.shape[0] == 1, "single-layer trunk"
    wts = _prep_weights(w_in[0], cmp_pe[0], cmp_w1[0], cmp_w2[0])
    mlp_w = (w_out[0].astype(BF16), g_mlp, w_up[0].astype(BF16), w_down[0].astype(BF16), g_final[None])
    y_p, p_cmp, p_slc, p_win, p_pool = _prompt_path(x_prompt, wts, g_attn, w_pool[0], pool_scale, *mlp_w)
    y_s, s_cmp, s_slc, s_win, s_pool = _sample_path(x_sample, cache_cmp[0], cache_slc[0], state_win[0],
                                                    state_pool[0], page_table, wts, g_attn, w_pool[0],
                                                    pool_scale, *mlp_w)
    return (y_p, y_s, p_cmp, p_slc, p_win, p_pool, s_cmp, s_slc, s_win, s_pool)
```

```python
import functools
import math

import jax
import jax.numpy as jnp
from jax import lax
from jax.experimental import pallas as pl
from jax.experimental.pallas import tpu as pltpu

D_MODEL = 1024
N_HEADS = 8
N_KV = 2
HPG = N_HEADS // N_KV
HEAD_DIM = 64
ATTN_WIDTH = N_HEADS * HEAD_DIM
POOL_WIDTH = D_MODEL - ATTN_WIDTH
POOL_WINDOWS = (2, 4, 8, 16)
POOL_GROUP = POOL_WIDTH // len(POOL_WINDOWS)
POOL_HIST = max(POOL_WINDOWS) - 1
BLK = 64
N_SELECT = 16
WINDOW = 512
CMP_HIDDEN = 2 * HEAD_DIM
D_FF = 4 * D_MODEL
KV_WIDTH = N_KV * 2 * HEAD_DIM
PAGE_SIZE = 128
BLOCKS_PER_PAGE = PAGE_SIZE // BLK
SCALE = HEAD_DIM ** -0.5
FORCE_SCORE = float(HPG + 1)
NORM_EPS = 1e-6
NEG = -1e30
N_GATE_PAD = 128
HALO = 16

F32 = jnp.float32
BF16 = jnp.bfloat16
NT_DIMS = (((1,), (1,)), ((), ()))

V7X_VMEM_LIMIT = 56 * 1024 * 1024


def _slope(head):
    return float(2.0 ** (-8.0 * (head + 1) / N_HEADS))


def _cparams(sem, vmem=None):
    return pltpu.CompilerParams(dimension_semantics=sem, vmem_limit_bytes=vmem)


def _proj_kernel(x_ref, g_ref, wtok_ref, wkv_ref, q_ref, gate_ref, u_ref, kc_ref, ks_ref, kw_ref, kvtb_ref, *rest,
                 tm, emit_tok_kv, emit_pages):
    x = x_ref[0]
    ms = jnp.mean(x * x, axis=-1, keepdims=True)
    xn = ((x * lax.rsqrt(ms + NORM_EPS)) * g_ref[...]).astype(BF16)
    tok = lax.dot_general(xn, wtok_ref[...], NT_DIMS, preferred_element_type=F32)
    q_ref[0] = (tok[:, :ATTN_WIDTH] * SCALE).astype(BF16)
    u_ref[0] = tok[:, ATTN_WIDTH:ATTN_WIDTH + POOL_WIDTH]
    gl = tok[:, ATTN_WIDTH + POOL_WIDTH:ATTN_WIDTH + POOL_WIDTH + N_GATE_PAD]
    gate_ref[0] = 1.0 / (1.0 + jnp.exp(-gl))
    kvt = lax.dot_general(wkv_ref[...], xn, NT_DIMS, preferred_element_type=F32)
    for br, ref in enumerate((kc_ref, ks_ref, kw_ref)):
        ref[0] = kvt[br * KV_WIDTH:(br + 1) * KV_WIDTH]
    for br in range(2):
        kvtb_ref[br, 0] = kvt[(br + 1) * KV_WIDTH:(br + 2) * KV_WIDTH].astype(BF16)
    k = 0
    if emit_tok_kv:
        base = ATTN_WIDTH + POOL_WIDTH + N_GATE_PAD
        rest[k][0] = tok[:, base:base + 3 * KV_WIDTH]
        k += 1
    if emit_pages:
        for pg in range(tm // PAGE_SIZE):
            rest[k][pg] = kvt[:KV_WIDTH, pg * PAGE_SIZE:(pg + 1) * PAGE_SIZE]


def _proj(x, g_attn, wtok, wkv, *, tm, emit_tok_kv, emit_pages):
    B, T, _ = x.shape
    nt = T // tm
    out_shape = [
        jax.ShapeDtypeStruct((B, T, ATTN_WIDTH), BF16),
        jax.ShapeDtypeStruct((B, T, N_GATE_PAD), F32),
        jax.ShapeDtypeStruct((B, T, POOL_WIDTH), F32),
    ] + [jax.ShapeDtypeStruct((B, KV_WIDTH, T), F32)] * 3 + [
        jax.ShapeDtypeStruct((2, B, KV_WIDTH, T), BF16),
    ]
    out_specs = [
        pl.BlockSpec((1, tm, ATTN_WIDTH), lambda b, i: (b, i, 0)),
        pl.BlockSpec((1, tm, N_GATE_PAD), lambda b, i: (b, i, 0)),
        pl.BlockSpec((1, tm, POOL_WIDTH), lambda b, i: (b, i, 0)),
    ] + [pl.BlockSpec((1, KV_WIDTH, tm), lambda b, i: (b, 0, i))] * 3 + [
        pl.BlockSpec((2, 1, KV_WIDTH, tm), lambda b, i: (0, b, 0, i)),
    ]
    if emit_tok_kv:
        out_shape.append(jax.ShapeDtypeStruct((B, T, 3 * KV_WIDTH), F32))
        out_specs.append(pl.BlockSpec((1, tm, 3 * KV_WIDTH), lambda b, i: (b, i, 0)))
    if emit_pages:
        ppt = tm // PAGE_SIZE
        out_shape.append(jax.ShapeDtypeStruct((B * T // PAGE_SIZE, KV_WIDTH, PAGE_SIZE), F32))
        out_specs.append(pl.BlockSpec((ppt, KV_WIDTH, PAGE_SIZE), lambda b, i: (b * nt + i, 0, 0)))
    return pl.pallas_call(
        functools.partial(_proj_kernel, tm=tm, emit_tok_kv=emit_tok_kv, emit_pages=emit_pages),
        grid=(B, nt),
        in_specs=[
            pl.BlockSpec((1, tm, D_MODEL), lambda b, i: (b, i, 0)),
            pl.BlockSpec((1, D_MODEL), lambda b, i: (0, 0)),
            pl.BlockSpec(wtok.shape, lambda b, i: (0, 0)),
            pl.BlockSpec(wkv.shape, lambda b, i: (0, 0)),
        ],
        out_specs=out_specs,
        out_shape=out_shape,
        compiler_params=_cparams(("arbitrary", "arbitrary"), V7X_VMEM_LIMIT),
        name="proj",
    )(x, g_attn, wtok, wkv)


def _gelu_tanh(x):
    c = math.sqrt(2.0 / math.pi)
    return 0.5 * x * (1.0 + jnp.tanh(c * (x + 0.044715 * (x * x * x))))


def _compress_kernel(ids_ref, src_ref, pet_ref, w1_ref, w2_ref, out_ref, buf, sem, *, P, n_steps):
    i = pl.program_id(0)

    def page_copy(step, slot, p):
        return pltpu.make_async_copy(src_ref.at[ids_ref[step * P + p]], buf.at[slot, :, p, :], sem.at[slot])

    def start(step, slot):
        for p in range(P):
            page_copy(step, slot, p).start()

    @pl.when(i == 0)
    def _():
        start(0, 0)

    @pl.when(i + 1 < n_steps)
    def _():
        start(i + 1, (i + 1) % 2)

    slot = i % 2
    for p in range(P):
        page_copy(i, slot, p).wait()

    bref = buf.at[slot]
    for c in range(2):
        acc = jnp.zeros((N_KV * P, 2 * CMP_HIDDEN), F32)
        for dp in range(HEAD_DIM // 2):
            rows = []
            for g in range(N_KV):
                halves = []
                for dd in range(2):
                    r = g * 2 * HEAD_DIM + c * HEAD_DIM + 2 * dp + dd
                    pr = c * HEAD_DIM + 2 * dp + dd
                    halves.append(bref[r] + pet_ref[pr:pr + 1, :])
                rows.append(jnp.concatenate(halves, axis=1))
            lhs = jnp.concatenate(rows, axis=0).astype(BF16)
            acc = acc + jnp.dot(lhs, w1_ref[c, dp], preferred_element_type=F32)
        hid = _gelu_tanh(acc).astype(BF16)
        oc = jnp.dot(hid, w2_ref[c], preferred_element_type=F32).astype(BF16)
        for g in range(N_KV):
            out_ref[g, c] = oc[g * P:(g + 1) * P]


def _compress(ids, src, pet, w1bd, w2bd, *, P):
    n_total = ids.shape[0]
    n_steps = n_total // P
    grid_spec = pltpu.PrefetchScalarGridSpec(
        num_scalar_prefetch=1,
        grid=(n_steps,),
        in_specs=[
            pl.BlockSpec(memory_space=pl.ANY),
            pl.BlockSpec(pet.shape, lambda i, ids: (0, 0)),
            pl.BlockSpec(w1bd.shape, lambda i, ids: (0, 0, 0, 0)),
            pl.BlockSpec(w2bd.shape, lambda i, ids: (0, 0, 0)),
        ],
        out_specs=pl.BlockSpec((N_KV, 2, P, PAGE_SIZE), lambda i, ids: (0, 0, i, 0)),
        scratch_shapes=[
            pltpu.VMEM((2, KV_WIDTH, P, PAGE_SIZE), F32),
            pltpu.SemaphoreType.DMA((2,)),
        ],
    )
    return pl.pallas_call(
        functools.partial(_compress_kernel, P=P, n_steps=n_steps),
        grid_spec=grid_spec,
        out_shape=jax.ShapeDtypeStruct((N_KV, 2, n_total, PAGE_SIZE), BF16),
        compiler_params=_cparams(("arbitrary",), V7X_VMEM_LIMIT),
        name="compress",
    )(ids, src, pet, w1bd, w2bd)


def _compress_weights(cmp_pe, cmp_w1, cmp_w2):
    eye = jnp.eye(BLOCKS_PER_PAGE, dtype=F32)
    pet = jnp.tile(jnp.transpose(cmp_pe, (1, 2, 0)).reshape(2 * HEAD_DIM, BLK), (1, BLOCKS_PER_PAGE))
    w1 = cmp_w1.reshape(2, BLK, HEAD_DIM // 2, 2, CMP_HIDDEN)
    w1bd = jnp.einsum('cluzh,jk->cuzjlkh', w1, eye).reshape(
        2, HEAD_DIM // 2, 2 * PAGE_SIZE, BLOCKS_PER_PAGE * CMP_HIDDEN).astype(BF16)
    w2bd = jnp.einsum('chd,jk->cjhkd', cmp_w2, eye).reshape(
        2, BLOCKS_PER_PAGE * CMP_HIDDEN, BLOCKS_PER_PAGE * HEAD_DIM).astype(BF16)
    return pet, w1bd, w2bd


def _col_block(c, npg):
    return 2 * c if c < npg else 2 * (c - npg) + 1


def _rank_select_t(score, blk_arr, blk_int, nblk, n_select):
    tq, W = score.shape
    x = score.T[:nblk]
    blk_full = blk_arr(lax.broadcasted_iota(jnp.int32, (nblk, tq), 0))
    rank = jnp.zeros((nblk, tq), jnp.int32)
    for i in range(nblk):
        xi = x[i:i + 1, :]
        ahead = (xi > x) | ((xi == x) & (blk_full > blk_int(i)))
        rank = rank + jnp.where(ahead, 1, 0)
    sel = jnp.where(rank < n_select, 1.0, 0.0)
    if nblk < W:
        sel = jnp.concatenate([sel, jnp.zeros((W - nblk, tq), F32)], axis=0)
    return sel.T


def _key_consts(T, npg):
    assert T <= 256 * BLK
    nblk = npg * BLOCKS_PER_PAGE
    k = jnp.arange(T, dtype=jnp.int32)[None, :]
    r = jnp.arange(HEAD_DIM, dtype=jnp.int32)[:, None]
    pos = jnp.where(r == 0, k // BLK, jnp.where(r == 1, k % BLK, 0)).astype(F32)
    blk = jnp.where(r < npg, 2 * r, 2 * (r - npg) + 1)
    en = jnp.where((k // BLK == blk) & (r < nblk), NEG, 0.0)
    return jnp.concatenate([pos, en, jnp.zeros((HEAD_DIM, T), F32)], axis=0).astype(BF16)


M_INIT = -3e38


def _p_attn_kernel(q_ref, gate_ref, comp_ref, ks_ref, kw_ref, kc_ref, o_ref,
                   qx, m_ref, acc_ref, out_ref, *, tq, tk, npg):
    qi = pl.program_id(1)
    q0 = qi * tq
    pos_t = q0 + lax.broadcasted_iota(jnp.int32, (tq, 1), 0)
    nblk = npg * BLOCKS_PER_PAGE
    W = 128
    R = HPG * tq
    lane_h = lax.broadcasted_iota(jnp.int32, (1, HEAD_DIM), 1)

    for hh in range(N_HEADS):
        g, h = divmod(hh, HPG)
        rows = pl.ds(h * tq, tq)
        qx[g, rows, 0:HEAD_DIM] = q_ref[0, :, hh * HEAD_DIM:(hh + 1) * HEAD_DIM]
        posc = jnp.where(lane_h == 0, _slope(hh) * BLK, jnp.where(lane_h == 1, _slope(hh), 0.0))
        qx[g, rows, HEAD_DIM:2 * HEAD_DIM] = jnp.broadcast_to(posc, (tq, HEAD_DIM)).astype(BF16)
        qx[g, rows, 3 * HEAD_DIM:4 * HEAD_DIM] = jnp.zeros((tq, HEAD_DIM), BF16)

    def blk_arr(c):
        return jnp.where(c < npg, 2 * c, 2 * (c - npg) + 1)

    col = lax.broadcasted_iota(jnp.int32, (1, W), 1)
    blk_n = blk_arr(col)
    first = col < nblk
    end_pos = (blk_n + 1) * BLK - 1
    dist_c = (pos_t - end_pos).astype(F32)
    mask_c = (dist_c >= 0) & first
    cur = pos_t // BLK

    for g in range(N_KV):
        ck = comp_ref[g, 0].astype(F32)
        cv = comp_ref[g, 1].astype(F32)
        pad = [jnp.zeros((W - nblk, HEAD_DIM), F32)] if nblk < W else []
        ckp = jnp.concatenate([ck[:, :HEAD_DIM], ck[:, HEAD_DIM:]] + pad, axis=0).astype(BF16)
        cvp = jnp.concatenate([cv[:, :HEAD_DIM], cv[:, HEAD_DIM:]] + pad, axis=0).astype(BF16)
        imp = jnp.zeros((tq, W), F32)
        for h in range(HPG):
            hh = g * HPG + h
            qh = qx[g, pl.ds(h * tq, tq), 0:HEAD_DIM]
            s = lax.dot_general(qh, ckp, NT_DIMS, preferred_element_type=F32)
            s = jnp.where(mask_c, s - _slope(hh) * dist_c, NEG)
            mx = jnp.max(s, axis=1, keepdims=True)
            e = jnp.where(mask_c, jnp.exp(s - mx), 0.0)
            p = e / jnp.maximum(jnp.sum(e, axis=1, keepdims=True), 1e-30)
            imp = imp + p
            oc = jnp.dot(p.astype(BF16), cvp, preferred_element_type=F32)
            gcol = gate_ref[0, :, hh:hh + 1]
            out_ref[:, hh * HEAD_DIM:(hh + 1) * HEAD_DIM] = gcol * oc
        forced = (blk_n == 0) | (blk_n == cur) | (blk_n == cur - 1)
        score = jnp.where(forced, FORCE_SCORE, imp)
        score = jnp.where(blk_n > cur, -1.0, score)
        sel = _rank_select_t(score, blk_arr, lambda i: _col_block(i, npg), nblk, N_SELECT)
        notsel = jnp.where((sel > 0.5) & (blk_n <= cur), 0.0, 1.0).astype(BF16)
        for h in range(HPG):
            qx[g, pl.ds(h * tq, tq), 2 * HEAD_DIM:3 * HEAD_DIM] = notsel[:, :HEAD_DIM]

    ones_rows = jnp.where(lax.broadcasted_iota(jnp.int32, (HEAD_DIM, 1), 0) == 0, 1.0, 0.0)
    pos_rows = q0 + lax.broadcasted_iota(jnp.int32, (R, 1), 0) % tq

    def attend_tile(kv_ref, k0, width, kc_rows, mode):
        kpos = k0 + lax.broadcasted_iota(jnp.int32, (1, width), 1)
        kc_tile = kc_ref[0:kc_rows, pl.ds(k0, width)]
        ones_tile = jnp.broadcast_to(ones_rows, (HEAD_DIM, width)).astype(BF16)
        if mode == "causal":
            keep = kpos <= pos_rows
        elif mode == "recent":
            keep = kpos > pos_rows - WINDOW
        for g in range(N_KV):
            kt = kv_ref[0, 0, g * 2 * HEAD_DIM:g * 2 * HEAD_DIM + HEAD_DIM, pl.ds(k0, width)]
            vt = kv_ref[0, 0, g * 2 * HEAD_DIM + HEAD_DIM:(g + 1) * 2 * HEAD_DIM, pl.ds(k0, width)]
            kext = jnp.concatenate([kt, kc_tile], axis=0)
            vext = jnp.concatenate([vt, ones_tile], axis=0)
            s = jnp.dot(qx[g, :, 0:HEAD_DIM + kc_rows], kext, preferred_element_type=F32)
            if mode != "full":
                s = jnp.where(keep, s, NEG)
            m_prev = m_ref[g]
            m_next = jnp.maximum(m_prev, jnp.max(s, axis=1, keepdims=True))
            alpha = jnp.exp(m_prev - m_next)
            m_ref[g] = m_next
            p = jnp.exp(s - jnp.concatenate([m_next] * (width // W), axis=1)).astype(BF16)
            pv = lax.dot_general(p, vext, NT_DIMS, preferred_element_type=F32)
            acc_ref[g] = alpha * acc_ref[g] + pv

    def reset():
        for g in range(N_KV):
            m_ref[g] = jnp.full((R, W), M_INIT, F32)
            acc_ref[g] = jnp.zeros((R, W), F32)

    def finish(gate_base):
        for g in range(N_KV):
            for h in range(HPG):
                hh = g * HPG + h
                a = acc_ref[g, pl.ds(h * tq, tq), :]
                o = a[:, :HEAD_DIM] / jnp.maximum(a[:, HEAD_DIM:HEAD_DIM + 1], 1e-30)
                gcol = gate_ref[0, :, gate_base + hh:gate_base + hh + 1]
                out_ref[:, hh * HEAD_DIM:(hh + 1) * HEAD_DIM] += gcol * o

    reset()
    n_sel = (q0 + tq + tk - 1) // tk

    def sel_body(j, carry):
        attend_tile(ks_ref, pl.multiple_of(j * tk, tk), tk, 3 * HEAD_DIM, "full")
        return carry

    lax.fori_loop(0, n_sel - 1, sel_body, 0)
    attend_tile(ks_ref, pl.multiple_of((n_sel - 1) * tk, tk), tk, 3 * HEAD_DIM, "causal")
    finish(N_HEADS)

    reset()
    wt = WINDOW // tq

    @pl.when(qi >= wt)
    def _():
        attend_tile(kw_ref, pl.multiple_of((qi - wt) * tq, tq), tq, HEAD_DIM, "recent")

    def win_body(j, carry):
        attend_tile(kw_ref, pl.multiple_of(j * tq, tq), tq, HEAD_DIM, "full")
        return carry

    lax.fori_loop(jnp.maximum(qi - wt + 1, 0), qi, win_body, 0)
    attend_tile(kw_ref, pl.multiple_of(q0, tq), tq, HEAD_DIM, "causal")
    finish(2 * N_HEADS)
    o_ref[0] = out_ref[...].astype(BF16)


def _p_attn(q, gates, comp, kvtb, kconst, *, tq, tk):
    B, T, _ = q.shape
    npg = T // PAGE_SIZE
    assert WINDOW % tq == 0 and tk % tq == 0
    return pl.pallas_call(
        functools.partial(_p_attn_kernel, tq=tq, tk=tk, npg=npg),
        grid=(B, T // tq),
        in_specs=[
            pl.BlockSpec((1, tq, ATTN_WIDTH), lambda b, i: (b, i, 0)),
            pl.BlockSpec((1, tq, N_GATE_PAD), lambda b, i: (b, i, 0)),
            pl.BlockSpec((N_KV, 2, npg, PAGE_SIZE), lambda b, i: (0, 0, b, 0)),
            pl.BlockSpec((1, 1, KV_WIDTH, T), lambda b, i: (0, b, 0, 0)),
            pl.BlockSpec((1, 1, KV_WIDTH, T), lambda b, i: (1, b, 0, 0)),
            pl.BlockSpec(kconst.shape, lambda b, i: (0, 0)),
        ],
        out_specs=pl.BlockSpec((1, tq, ATTN_WIDTH), lambda b, i: (b, i, 0)),
        out_shape=jax.ShapeDtypeStruct((B, T, ATTN_WIDTH), BF16),
        scratch_shapes=[
            pltpu.VMEM((N_KV, HPG * tq, 4 * HEAD_DIM), BF16),
            pltpu.VMEM((N_KV, HPG * tq, 128), F32),
            pltpu.VMEM((N_KV, HPG * tq, 128), F32),
            pltpu.VMEM((tq, ATTN_WIDTH), F32),
        ],
        compiler_params=_cparams(("arbitrary", "arbitrary"), V7X_VMEM_LIMIT),
        name="p_attn",
    )(q, gates, comp, kvtb, kvtb, kconst)


def _stack_heads(q_ref, g, ts):
    parts = [q_ref[0, :, (g * HPG + h) * HEAD_DIM:(g * HPG + h + 1) * HEAD_DIM].astype(F32) for h in range(HPG)]
    return jnp.concatenate(parts, axis=0).astype(BF16)


def _row_consts(g, ts):
    R = HPG * ts
    row = lax.broadcasted_iota(jnp.int32, (R, 1), 0)
    t_row = row % ts
    h_row = row // ts
    slope = jnp.zeros((R, 1), F32)
    for h in range(HPG):
        slope = jnp.where(h_row == h, _slope(g * HPG + h), slope)
    return t_row, slope


def _s_select_kernel(q_ref, gate_ref, comp_ref, cnew_ref, oc_ref, bits_ref, out_s, *, past, ts):
    W = 128
    nb_past = past // BLK
    lane = lax.broadcasted_iota(jnp.int32, (1, W), 1)
    idx_fns = [lambda c: 2 * c, lambda c: 2 * c + 1, lambda c: jnp.where(c == 0, nb_past, (1 << 20) + c)]
    idx_tiles = [f(lane) for f in idx_fns]
    valid_n = lane == 0
    tok = lax.broadcasted_iota(jnp.int32, (ts, 1), 0)
    cur_t = (past + tok) // BLK
    for g in range(N_KV):
        qg = _stack_heads(q_ref, g, ts)
        t_row, slope = _row_consts(g, ts)
        pos = past + t_row
        ck = comp_ref[g, 0]
        cv = comp_ref[g, 1]
        cn = cnew_ref[g, 0, 0].astype(F32)[:, :HEAD_DIM]
        vn = cnew_ref[g, 1, 0].astype(F32)[:, :HEAD_DIM]
        s_t = [lax.dot_general(qg, ck[:, :HEAD_DIM], NT_DIMS, preferred_element_type=F32),
               lax.dot_general(qg, ck[:, HEAD_DIM:], NT_DIMS, preferred_element_type=F32),
               jnp.broadcast_to(jnp.sum(qg.astype(F32) * cn, axis=1, keepdims=True), (HPG * ts, W))]
        masks, es = [], []
        for k in range(3):
            end_pos = (idx_tiles[k] + 1) * BLK - 1
            dist = (pos - end_pos).astype(F32)
            mk = dist >= 0
            if k == 2:
                mk = mk & valid_n
            masks.append(mk)
            s_t[k] = jnp.where(mk, s_t[k] - slope * dist, NEG)
        mx = jnp.maximum(jnp.maximum(jnp.max(s_t[0], axis=1, keepdims=True),
                                     jnp.max(s_t[1], axis=1, keepdims=True)),
                         jnp.max(s_t[2], axis=1, keepdims=True))
        for k in range(3):
            es.append(jnp.where(masks[k], jnp.exp(s_t[k] - mx), 0.0))
        den = (jnp.sum(es[0], axis=1, keepdims=True) + jnp.sum(es[1], axis=1, keepdims=True)
               + jnp.sum(es[2], axis=1, keepdims=True))
        inv = 1.0 / jnp.maximum(den, 1e-30)
        ps = [e * inv for e in es]
        o_c = (jnp.dot(ps[0].astype(BF16), cv[:, :HEAD_DIM], preferred_element_type=F32)
               + jnp.dot(ps[1].astype(BF16), cv[:, HEAD_DIM:], preferred_element_type=F32)
               + ps[2][:, 0:1].astype(BF16).astype(F32) * vn)
        for h in range(HPG):
            hh = g * HPG + h
            gcol = gate_ref[0, :, hh:hh + 1]
            out_s[:, hh * HEAD_DIM:(hh + 1) * HEAD_DIM] = gcol * o_c[h * ts:(h + 1) * ts]
        scores = []
        for k in range(3):
            imp = ps[k][0:ts]
            for h in range(1, HPG):
                imp = imp + ps[k][h * ts:(h + 1) * ts]
            idx = idx_tiles[k]
            forced = (idx == 0) | (idx == cur_t) | (idx == cur_t - 1)
            sc = jnp.where(forced, FORCE_SCORE, imp)
            sc = jnp.where(idx > cur_t, -1.0, sc)
            if k == 2:
                sc = jnp.where(valid_n, sc, -2.0)
            scores.append(sc)
        ranks = [jnp.zeros((ts, W), jnp.int32) for _ in range(3)]
        for kb in range(3):
            for r in range(W):
                y = pltpu.roll(scores[kb], r, 1) if r else scores[kb]
                yi = idx_fns[kb]((lane - r) & (W - 1))
                for ka in range(3):
                    if ka == kb and r == 0:
                        continue
                    ahead = (y > scores[ka]) | ((y == scores[ka]) & (yi < idx_tiles[ka]))
                    ranks[ka] = ranks[ka] + ahead.astype(jnp.int32)
        sel = [(ranks[k] < N_SELECT) for k in range(3)]
        wt = jnp.left_shift(1, 2 * tok)
        page_bits = jnp.sum(jnp.where(sel[0], wt, 0) + jnp.where(sel[1], 2 * wt, 0), axis=0, keepdims=True)
        new_bits = jnp.sum(jnp.where(sel[2] & valid_n, jnp.left_shift(1, tok), 0), axis=0, keepdims=True)
        bits_ref[0, g, 0:1, :] = page_bits
        bits_ref[0, g, 1:2, :] = new_bits
    oc_ref[0] = out_s[...]


def _s_select(q, gates, comp, cnew, *, past):
    Bs, ts, _ = q.shape
    npg = past // PAGE_SIZE
    assert npg == 128, "one lane tile of pages per sequence"
    return pl.pallas_call(
        functools.partial(_s_select_kernel, past=past, ts=ts),
        grid=(Bs,),
        in_specs=[
            pl.BlockSpec((1, ts, ATTN_WIDTH), lambda b: (b, 0, 0)),
            pl.BlockSpec((1, ts, N_GATE_PAD), lambda b: (b, 0, 0)),
            pl.BlockSpec((N_KV, 2, npg, PAGE_SIZE), lambda b: (0, 0, b, 0)),
            pl.BlockSpec((N_KV, 2, 1, 1, PAGE_SIZE), lambda b: (0, 0, b, 0, 0)),
        ],
        out_specs=[
            pl.BlockSpec((1, ts, ATTN_WIDTH), lambda b: (b, 0, 0)),
            pl.BlockSpec((1, N_KV, 2, 128), lambda b: (b, 0, 0, 0)),
        ],
        out_shape=[
            jax.ShapeDtypeStruct((Bs, ts, ATTN_WIDTH), F32),
            jax.ShapeDtypeStruct((Bs, N_KV, 2, 128), jnp.int32),
        ],
        scratch_shapes=[pltpu.VMEM((ts, ATTN_WIDTH), F32)],
        compiler_params=_cparams(("arbitrary",), V7X_VMEM_LIMIT),
        name="s_select",
    )(q, gates, comp, cnew)


def _s_attn_kernel(pt_ref, bits_ref, q_ref, gate_ref, oc_ref, win_ref, knew_ref, slc_ref,
                   o_ref, wout_ref, buf, sem, plist, out_s, *, past, ts, npg):
    b = pl.program_id(0)
    g = pl.program_id(1)
    W = 128
    R = HPG * ts
    wk = win_ref.shape[2]
    base = (b * N_KV + g) * 2 * W

    def page_copy(page, slot):
        return pltpu.make_async_copy(slc_ref.at[page, pl.ds(g * 2 * HEAD_DIM, 2 * HEAD_DIM)], buf.at[slot], sem.at[0])

    def issue(p, cnt):
        hit = bits_ref[base + p] != 0

        @pl.when(hit)
        def _():
            page_copy(pt_ref[b * npg + p], cnt).start()
            plist[cnt] = p

        return cnt + hit.astype(jnp.int32)

    cnt = lax.fori_loop(0, npg, issue, 0)

    q_all = q_ref[0].astype(F32)
    qg = jnp.concatenate([q_all[:, h * HEAD_DIM:(h + 1) * HEAD_DIM] for h in range(HPG)], axis=0).astype(BF16)
    row = lax.broadcasted_iota(jnp.int32, (R, 1), 0)
    t_row = row % ts
    head_row = row // ts + g * HPG
    slope = jnp.zeros((R, 1), F32)
    for hh in range(N_HEADS):
        slope = jnp.where(head_row == hh, _slope(hh), slope)
    lane = lax.broadcasted_iota(jnp.int32, (1, W), 1)

    kw = win_ref[0, 0:HEAD_DIM, :].astype(BF16)
    vw = win_ref[0, HEAD_DIM:2 * HEAD_DIM, :].astype(BF16)
    kn = knew_ref[2, 0, 0:HEAD_DIM, :].astype(BF16)
    vn = knew_ref[2, 0, HEAD_DIM:2 * HEAD_DIM, :].astype(BF16)
    i_st = lax.broadcasted_iota(jnp.int32, (1, wk), 1)
    d_st = t_row + wk - i_st
    m_st = (d_st >= 0) & (d_st < WINDOW)
    d_nw = t_row - lane
    m_nw = (d_nw >= 0) & (d_nw < WINDOW) & (lane < ts)
    s_st = jnp.where(m_st, jnp.dot(qg, kw, preferred_element_type=F32) - slope * d_st.astype(F32), NEG)
    s_nw = jnp.where(m_nw, jnp.dot(qg, kn, preferred_element_type=F32) - slope * d_nw.astype(F32), NEG)
    mx = jnp.maximum(jnp.max(s_st, axis=1, keepdims=True), jnp.max(s_nw, axis=1, keepdims=True))
    e_st = jnp.where(m_st, jnp.exp(s_st - mx), 0.0)
    e_nw = jnp.where(m_nw, jnp.exp(s_nw - mx), 0.0)
    den = jnp.sum(e_st, axis=1, keepdims=True) + jnp.sum(e_nw, axis=1, keepdims=True)
    inv = 1.0 / jnp.maximum(den, 1e-30)
    o_w = (lax.dot_general((e_st * inv).astype(BF16), vw, NT_DIMS, preferred_element_type=F32)
           + lax.dot_general((e_nw * inv).astype(BF16), vn, NT_DIMS, preferred_element_type=F32))

    last = wk - W
    shifted = pltpu.roll(win_ref[0], wk - ts, 1)
    newr = pltpu.roll(knew_ref[2, 0], W - ts, 1)
    wout_ref[0, :, 0:last] = shifted[:, 0:last]
    wout_ref[0, :, last:wk] = jnp.where(lane >= W - ts, newr, shifted[:, last:wk])

    bits_new = bits_ref[base + W]
    ksn = knew_ref[1, 0, 0:HEAD_DIM, :].astype(BF16)
    vsn = knew_ref[1, 0, HEAD_DIM:2 * HEAD_DIM, :].astype(BF16)
    d_sn = t_row - lane
    m_sn = (d_sn >= 0) & (lane < ts) & ((jnp.right_shift(bits_new, t_row) & 1) == 1)
    s_sn = jnp.where(m_sn, jnp.dot(qg, ksn, preferred_element_type=F32) - slope * d_sn.astype(F32), NEG)
    m0 = jnp.max(s_sn, axis=1, keepdims=True)
    e0 = jnp.where(m_sn, jnp.exp(s_sn - m0), 0.0)
    l0 = jnp.sum(e0, axis=1, keepdims=True)
    a0 = lax.dot_general(e0.astype(BF16), vsn, NT_DIMS, preferred_element_type=F32)

    def wait_one(s, c):
        page_copy(0, s).wait()
        return c

    lax.fori_loop(0, cnt, wait_one, 0)
    sh = 2 * t_row + (lane >= BLK).astype(jnp.int32)

    def page_step(s, carry):
        m, l, acc = carry
        p = plist[s]
        bits = bits_ref[base + p]
        kt = buf[s, 0:HEAD_DIM, :].astype(BF16)
        vt = buf[s, HEAD_DIM:2 * HEAD_DIM, :].astype(BF16)
        dist = past + t_row - (p * PAGE_SIZE + lane)
        mask = ((jnp.right_shift(bits, sh) & 1) == 1) & (dist >= 0)
        sc = jnp.where(mask, jnp.dot(qg, kt, preferred_element_type=F32) - slope * dist.astype(F32), NEG)
        m_new = jnp.maximum(m, jnp.max(sc, axis=1, keepdims=True))
        alpha = jnp.exp(m - m_new)
        e = jnp.where(mask, jnp.exp(sc - m_new), 0.0)
        l = alpha * l + jnp.sum(e, axis=1, keepdims=True)
        acc = alpha * acc + lax.dot_general(e.astype(BF16), vt, NT_DIMS, preferred_element_type=F32)
        return m_new, l, acc

    m, l, acc = lax.fori_loop(0, cnt, page_step, (m0, l0, a0))
    o_s = acc / jnp.maximum(l, 1e-30)

    g_vec = jnp.zeros((ts, 1), jnp.int32) + g
    for h in range(HPG):
        rows = slice(h * ts, (h + 1) * ts)
        g_s = jnp.zeros((ts, 1), F32)
        g_w = jnp.zeros((ts, 1), F32)
        for gg in range(N_KV):
            hh = gg * HPG + h
            g_s = jnp.where(g_vec == gg, gate_ref[0, :, N_HEADS + hh:N_HEADS + hh + 1], g_s)
            g_w = jnp.where(g_vec == gg, gate_ref[0, :, 2 * N_HEADS + hh:2 * N_HEADS + hh + 1], g_w)
        out_s[:, h * HEAD_DIM:(h + 1) * HEAD_DIM] = g_s * o_s[rows] + g_w * o_w[rows]
    o_ref[0] = oc_ref[0] + out_s[...]


def _s_attn(page_table, bits, q, gates, oc, win_t, knew, slc_t, *, past):
    Bs, ts, _ = q.shape
    npg = past // PAGE_SIZE
    wk = win_t.shape[2]
    gw = HPG * HEAD_DIM
    grid_spec = pltpu.PrefetchScalarGridSpec(
        num_scalar_prefetch=2,
        grid=(Bs, N_KV),
        in_specs=[
            pl.BlockSpec((1, ts, gw), lambda b, g, *_: (b, 0, g)),
            pl.BlockSpec((1, ts, N_GATE_PAD), lambda b, g, *_: (b, 0, 0)),
            pl.BlockSpec((1, ts, gw), lambda b, g, *_: (b, 0, g)),
            pl.BlockSpec((1, 2 * HEAD_DIM, wk), lambda b, g, *_: (b, g, 0)),
            pl.BlockSpec((3, 1, 2 * HEAD_DIM, PAGE_SIZE), lambda b, g, *_: (0, b, g, 0)),
            pl.BlockSpec(memory_space=pl.ANY),
        ],
        out_specs=[
            pl.BlockSpec((1, ts, gw), lambda b, g, *_: (b, 0, g)),
            pl.BlockSpec((1, 2 * HEAD_DIM, wk), lambda b, g, *_: (b, g, 0)),
        ],
        scratch_shapes=[
            pltpu.VMEM((npg, 2 * HEAD_DIM, PAGE_SIZE), F32),
            pltpu.SemaphoreType.DMA((1,)),
            pltpu.SMEM((npg,), jnp.int32),
            pltpu.VMEM((ts, gw), F32),
        ],
    )
    return pl.pallas_call(
        functools.partial(_s_attn_kernel, past=past, ts=ts, npg=npg),
        grid_spec=grid_spec,
        out_shape=[
            jax.ShapeDtypeStruct((Bs, ts, ATTN_WIDTH), F32),
            jax.ShapeDtypeStruct(win_t.shape, F32),
        ],
        compiler_params=_cparams(("arbitrary", "arbitrary"), V7X_VMEM_LIMIT),
        name="s_attn",
    )(page_table.reshape(-1), bits.reshape(-1), q, gates, oc, win_t, knew, slc_t)


def _pool_kernel(u_ref, halo_ref, wp_ref, ps_ref, o_ref, ext, *, tp, tiles_per_seq, pos_base):
    i = pl.program_id(0)
    ext[0:HALO, :] = halo_ref[0]
    ext[HALO:HALO + tp, :] = u_ref[0]
    pos = pos_base + (i % tiles_per_seq) * tp + lax.broadcasted_iota(jnp.int32, (tp, 1), 0)
    for gi, w in enumerate(POOL_WINDOWS):
        lanes = slice(gi * POOL_GROUP, (gi + 1) * POOL_GROUP)
        own = ext[HALO:HALO + tp, lanes]
        tot = own
        for k in range(1, w):
            tot = tot + ext[HALO - k:HALO - k + tp, lanes]
        cnt = jnp.minimum(w, pos + 1).astype(F32)
        d = (tot / cnt - own).astype(BF16)
        y = jnp.dot(d, wp_ref[gi], preferred_element_type=F32)
        o_ref[0, :, lanes] = (y * ps_ref[:, lanes]).astype(o_ref.dtype)


def _pool(u, halo, w_pool, pool_scale, *, tiles_per_seq, pos_base, out_dtype):
    n_tiles, tp, _ = u.shape
    return pl.pallas_call(
        functools.partial(_pool_kernel, tp=tp, tiles_per_seq=tiles_per_seq, pos_base=pos_base),
        grid=(n_tiles,),
        in_specs=[
            pl.BlockSpec((1, tp, POOL_WIDTH), lambda i: (i, 0, 0)),
            pl.BlockSpec((1, HALO, POOL_WIDTH), lambda i: (i, 0, 0)),
            pl.BlockSpec(w_pool.shape, lambda i: (0, 0, 0)),
            pl.BlockSpec((1, POOL_WIDTH), lambda i: (0, 0)),
        ],
        out_specs=pl.BlockSpec((1, tp, POOL_WIDTH), lambda i: (i, 0, 0)),
        out_shape=jax.ShapeDtypeStruct((n_tiles, tp, POOL_WIDTH), out_dtype),
        scratch_shapes=[pltpu.VMEM((HALO + tp, POOL_WIDTH), F32)],
        compiler_params=_cparams(("arbitrary",), V7X_VMEM_LIMIT),
        name="pool",
    )(u, halo, w_pool, pool_scale)


def _rms(x, g):
    return (x * lax.rsqrt(jnp.mean(x * x, axis=-1, keepdims=True) + NORM_EPS)) * g


def _mlp_kernel(x_ref, o_ref, p_ref, wo_ref, gm_ref, wu_ref, wd_ref, gf_ref, y_ref, *, ff_chunk):
    mix = jnp.concatenate([o_ref[...], p_ref[...]], axis=1)
    h = x_ref[...] + jnp.dot(mix, wo_ref[...], preferred_element_type=F32)
    hn = _rms(h, gm_ref[...]).astype(BF16)
    acc = jnp.zeros(h.shape, F32)
    for c in range(D_FF // ff_chunk):
        a = jnp.dot(hn, wu_ref[:, c * ff_chunk:(c + 1) * ff_chunk], preferred_element_type=F32)
        a = jnp.square(jnp.maximum(a, 0.0)).astype(BF16)
        acc = acc + jnp.dot(a, wd_ref[c * ff_chunk:(c + 1) * ff_chunk, :], preferred_element_type=F32)
    y_ref[...] = _rms(h + acc, gf_ref[...])


def _mlp(x, o, p, w_out, g_mlp, w_up, w_down, g_final, *, tm):
    N = x.shape[0]
    const = lambda i: (0, 0)
    resident = lambda a: pl.BlockSpec(a.shape, const, pipeline_mode=pl.Buffered(1))
    return pl.pallas_call(
        functools.partial(_mlp_kernel, ff_chunk=1024),
        grid=(N // tm,),
        in_specs=[
            pl.BlockSpec((tm, D_MODEL), lambda i: (i, 0)),
            pl.BlockSpec((tm, ATTN_WIDTH), lambda i: (i, 0)),
            pl.BlockSpec((tm, POOL_WIDTH), lambda i: (i, 0)),
            resident(w_out), resident(g_mlp), resident(w_up), resident(w_down), resident(g_final),
        ],
        out_specs=pl.BlockSpec((tm, D_MODEL), lambda i: (i, 0)),
        out_shape=jax.ShapeDtypeStruct((N, D_MODEL), F32),
        compiler_params=_cparams(("arbitrary",), V7X_VMEM_LIMIT),
        name="mlp",
    )(x, o, p, w_out, g_mlp, w_up, w_down, g_final)


def _rows_from_kt(kt):
    B, _, S = kt.shape
    return jnp.transpose(kt.reshape(B, N_KV, 2, HEAD_DIM, S), (0, 4, 1, 2, 3))[None]


def _prep_weights(w_in, cmp_pe, cmp_w1, cmp_w2):
    w_t = jnp.transpose(w_in)
    a, b = ATTN_WIDTH, ATTN_WIDTH + 3 * KV_WIDTH
    wq, wkv = w_t[:a], w_t[a:b]
    wg, wu = w_t[b:b + 3 * N_HEADS], w_t[b + 3 * N_HEADS:]
    wg = jnp.pad(wg, ((0, N_GATE_PAD - 3 * N_HEADS), (0, 0)))
    wtok = jnp.concatenate([wq, wu, wg], axis=0).astype(BF16)
    wtok_kv = jnp.concatenate([wq, wu, wg, wkv], axis=0).astype(BF16)
    return wtok, wtok_kv, wkv.astype(BF16), _compress_weights(cmp_pe, cmp_w1, cmp_w2)


def _prompt_path(x, wts, g_attn, w_pool, pool_scale, w_out, g_mlp, w_up, w_down, g_final):
    wtok, _, wkv, (pet, w1bd, w2bd) = wts
    B, T, _ = x.shape
    tm = min(512, T)
    q, gates, u, kc_t, ks_t, kw_t, kvtb, pages = _proj(x, g_attn, wtok, wkv, tm=tm, emit_tok_kv=False,
                                                        emit_pages=True)
    n_pages = B * T // PAGE_SIZE
    comp = _compress(jnp.arange(n_pages, dtype=jnp.int32), pages, pet, w1bd, w2bd, P=min(64, n_pages))
    o = _p_attn(q, gates, comp, kvtb, _key_consts(T, T // PAGE_SIZE), tq=min(256, T), tk=256)
    nt = T // tm
    u4 = u.reshape(B, nt, tm, POOL_WIDTH)
    halo = jnp.concatenate([jnp.zeros((B, 1, HALO, POOL_WIDTH), F32), u4[:, :-1, tm - HALO:, :]], axis=1)
    pool = _pool(u.reshape(B * nt, tm, POOL_WIDTH), halo.reshape(B * nt, HALO, POOL_WIDTH),
                 w_pool, pool_scale, tiles_per_seq=nt, pos_base=0, out_dtype=BF16)
    N = B * T
    y = _mlp(x.reshape(N, D_MODEL), o.reshape(N, ATTN_WIDTH), pool.reshape(N, POOL_WIDTH),
             w_out, g_mlp, w_up, w_down, g_final, tm=tm)
    wk = min(WINDOW, T)
    return (y.reshape(B, T, D_MODEL), _rows_from_kt(kc_t), _rows_from_kt(ks_t),
            _rows_from_kt(kw_t[:, :, T - wk:]), u[None, :, T - POOL_HIST:])


def _kt_view(rows):
    n, s = rows.shape[:2]
    return jnp.transpose(rows, (0, 2, 3, 4, 1)).reshape(n, KV_WIDTH, s)


def _sample_path(x, cache_cmp, cache_slc, state_win, state_pool, page_table, wts, g_attn,
                 w_pool, pool_scale, w_out, g_mlp, w_up, w_down, g_final):
    _, wtok_kv, wkv, (pet, w1bd, w2bd) = wts
    Bs, ts, _ = x.shape
    N = Bs * ts
    npg = page_table.shape[1]
    past = npg * PAGE_SIZE
    q, gates, u, kc_t, ks_t, kw_t, _, kv_tok = _proj(x.reshape(1, N, D_MODEL), g_attn, wtok_kv, wkv, tm=N,
                                                     emit_tok_kv=True, emit_pages=False)
    kvt = jnp.concatenate([kc_t, ks_t, kw_t], axis=0)
    knew = jnp.pad(jnp.transpose(kvt.reshape(3, KV_WIDTH, Bs, ts), (0, 2, 1, 3)),
                   ((0, 0), (0, 0), (0, 0), (0, PAGE_SIZE - ts)))
    comp_past = _compress(page_table.reshape(-1), _kt_view(cache_cmp), pet, w1bd, w2bd, P=min(64, Bs * npg))
    comp_new = _compress(jnp.arange(Bs, dtype=jnp.int32), knew[0], pet, w1bd, w2bd, P=Bs)
    qf = q.reshape(Bs, ts, ATTN_WIDTH).astype(F32)
    gts = gates.reshape(Bs, ts, N_GATE_PAD)
    oc, bits = _s_select(qf, gts, comp_past, comp_new.astype(F32).reshape(N_KV, 2, Bs, 1, PAGE_SIZE), past=past)
    o, win_new = _s_attn(page_table, bits, qf, gts, oc, _kt_view(state_win), knew, _kt_view(cache_slc), past=past)
    u3 = u.reshape(Bs, ts, POOL_WIDTH)
    halo = jnp.pad(state_pool, ((0, 0), (HALO - POOL_HIST, 0), (0, 0)))
    pool = _pool(u3, halo, w_pool, pool_scale, tiles_per_seq=1, pos_base=past, out_dtype=F32)
    y = _mlp(x.reshape(N, D_MODEL), o.reshape(N, ATTN_WIDTH).astype(BF16), pool.reshape(N, POOL_WIDTH).astype(BF16),
             w_out, g_mlp, w_up, w_down, g_final, tm=N)
    kv_rows = kv_tok.reshape(Bs, ts, 3, N_KV, 2, HEAD_DIM)
    s_pool = jnp.concatenate([state_pool, u3], axis=1)[None, :, ts:]
    return (y.reshape(Bs, ts, D_MODEL), kv_rows[None, :, :, 0], kv_rows[None, :, :, 1],
            _rows_from_kt(win_new), s_pool)


def kernel(x_prompt, x_sample, cache_cmp, cache_slc, state_win, state_pool, page_table, g_attn, w_in,
           cmp_pe, cmp_w1, cmp_w2, w_pool, pool_scale, w_out, g_mlp, w_up, w_down, g_final):
    assert g_attn.shape[0] == 1, "single-layer trunk"
    wts = _prep_weights(w_in[0], cmp_pe[0], cmp_w1[0], cmp_w2[0])
    mlp_w = (w_out[0].astype(BF16), g_mlp, w_up[0].astype(BF16), w_down[0].astype(BF16), g_final[None])
    y_p, p_cmp, p_slc, p_win, p_pool = _prompt_path(x_prompt, wts, g_attn, w_pool[0], pool_scale, *mlp_w)
    y_s, s_cmp, s_slc, s_win, s_pool = _sample_path(x_sample, cache_cmp[0], cache_slc[0], state_win[0],
                                                    state_pool[0], page_table, wts, g_attn, w_pool[0],
                                                    pool_scale, *mlp_w)
    return (y_p, y_s, p_cmp, p_slc, p_win, p_pool, s_cmp, s_slc, s_win, s_pool)
```

```python
import functools
import math

import jax
import jax.numpy as jnp
from jax import lax
from jax.experimental import pallas as pl
from jax.experimental.pallas import tpu as pltpu

D_MODEL = 1024
N_HEADS = 8
N_KV = 2
HPG = N_HEADS // N_KV
HEAD_DIM = 64
ATTN_WIDTH = N_HEADS * HEAD_DIM
POOL_WIDTH = D_MODEL - ATTN_WIDTH
POOL_WINDOWS = (2, 4, 8, 16)
POOL_GROUP = POOL_WIDTH // len(POOL_WINDOWS)
POOL_HIST = max(POOL_WINDOWS) - 1
BLK = 64
N_SELECT = 16
WINDOW = 512
CMP_HIDDEN = 2 * HEAD_DIM
D_FF = 4 * D_MODEL
KV_WIDTH = N_KV * 2 * HEAD_DIM
PAGE_SIZE = 128
BLOCKS_PER_PAGE = PAGE_SIZE // BLK
SCALE = HEAD_DIM ** -0.5
FORCE_SCORE = float(HPG + 1)
NORM_EPS = 1e-6
NEG = -1e30
N_GATE_PAD = 128
HALO = 16

F32 = jnp.float32
BF16 = jnp.bfloat16
NT_DIMS = (((1,), (1,)), ((), ()))

V7X_VMEM_LIMIT = 56 * 1024 * 1024


def _slope(head):
    return float(2.0 ** (-8.0 * (head + 1) / N_HEADS))


def _cparams(sem, vmem=None, flags=None):
    return pltpu.CompilerParams(dimension_semantics=sem, vmem_limit_bytes=vmem, flags=flags)


def _proj_kernel(x_ref, g_ref, wtok_ref, wkv_ref, q_ref, gate_ref, u_ref, kc_ref, ks_ref, kw_ref, kvtb_ref, *rest,
                 tm, emit_tok_kv, emit_pages):
    x = x_ref[0]
    ms = jnp.mean(x * x, axis=-1, keepdims=True)
    xn = ((x * lax.rsqrt(ms + NORM_EPS)) * g_ref[...]).astype(BF16)
    tok = lax.dot_general(xn, wtok_ref[...], NT_DIMS, preferred_element_type=F32)
    q_ref[0] = (tok[:, :ATTN_WIDTH] * SCALE).astype(BF16)
    u_ref[0] = tok[:, ATTN_WIDTH:ATTN_WIDTH + POOL_WIDTH]
    gl = tok[:, ATTN_WIDTH + POOL_WIDTH:ATTN_WIDTH + POOL_WIDTH + N_GATE_PAD]
    gate_ref[0] = 1.0 / (1.0 + jnp.exp(-gl))
    kvt = lax.dot_general(wkv_ref[...], xn, NT_DIMS, preferred_element_type=F32)
    for br, ref in enumerate((kc_ref, ks_ref, kw_ref)):
        ref[0] = kvt[br * KV_WIDTH:(br + 1) * KV_WIDTH]
    for br in range(2):
        kvtb_ref[br, 0] = kvt[(br + 1) * KV_WIDTH:(br + 2) * KV_WIDTH].astype(BF16)
    k = 0
    if emit_tok_kv:
        base = ATTN_WIDTH + POOL_WIDTH + N_GATE_PAD
        rest[k][0] = tok[:, base:base + 3 * KV_WIDTH]
        k += 1
    if emit_pages:
        for pg in range(tm // PAGE_SIZE):
            rest[k][pg] = kvt[:KV_WIDTH, pg * PAGE_SIZE:(pg + 1) * PAGE_SIZE]


def _proj(x, g_attn, wtok, wkv, *, tm, emit_tok_kv, emit_pages):
    B, T, _ = x.shape
    nt = T // tm
    out_shape = [
        jax.ShapeDtypeStruct((B, T, ATTN_WIDTH), BF16),
        jax.ShapeDtypeStruct((B, T, N_GATE_PAD), F32),
        jax.ShapeDtypeStruct((B, T, POOL_WIDTH), F32),
    ] + [jax.ShapeDtypeStruct((B, KV_WIDTH, T), F32)] * 3 + [
        jax.ShapeDtypeStruct((2, B, KV_WIDTH, T), BF16),
    ]
    out_specs = [
        pl.BlockSpec((1, tm, ATTN_WIDTH), lambda b, i: (b, i, 0)),
        pl.BlockSpec((1, tm, N_GATE_PAD), lambda b, i: (b, i, 0)),
        pl.BlockSpec((1, tm, POOL_WIDTH), lambda b, i: (b, i, 0)),
    ] + [pl.BlockSpec((1, KV_WIDTH, tm), lambda b, i: (b, 0, i))] * 3 + [
        pl.BlockSpec((2, 1, KV_WIDTH, tm), lambda b, i: (0, b, 0, i)),
    ]
    if emit_tok_kv:
        out_shape.append(jax.ShapeDtypeStruct((B, T, 3 * KV_WIDTH), F32))
        out_specs.append(pl.BlockSpec((1, tm, 3 * KV_WIDTH), lambda b, i: (b, i, 0)))
    if emit_pages:
        ppt = tm // PAGE_SIZE
        out_shape.append(jax.ShapeDtypeStruct((B * T // PAGE_SIZE, KV_WIDTH, PAGE_SIZE), F32))
        out_specs.append(pl.BlockSpec((ppt, KV_WIDTH, PAGE_SIZE), lambda b, i: (b * nt + i, 0, 0)))
    return pl.pallas_call(
        functools.partial(_proj_kernel, tm=tm, emit_tok_kv=emit_tok_kv, emit_pages=emit_pages),
        grid=(B, nt),
        in_specs=[
            pl.BlockSpec((1, tm, D_MODEL), lambda b, i: (b, i, 0)),
            pl.BlockSpec((1, D_MODEL), lambda b, i: (0, 0)),
            pl.BlockSpec(wtok.shape, lambda b, i: (0, 0)),
            pl.BlockSpec(wkv.shape, lambda b, i: (0, 0)),
        ],
        out_specs=out_specs,
        out_shape=out_shape,
        compiler_params=_cparams(("arbitrary", "arbitrary"), V7X_VMEM_LIMIT),
        name="proj",
    )(x, g_attn, wtok, wkv)


def _gelu_tanh(x):
    c = math.sqrt(2.0 / math.pi)
    return 0.5 * x * (1.0 + jnp.tanh(c * (x + 0.044715 * (x * x * x))))


def _compress_kernel(ids_ref, src_ref, pet_ref, w1_ref, w2_ref, out_ref, buf, sem, *, P, n_steps):
    i = pl.program_id(0)

    def page_copy(step, slot, p):
        return pltpu.make_async_copy(src_ref.at[ids_ref[step * P + p]], buf.at[slot, :, p, :], sem.at[slot])

    def start(step, slot):
        for p in range(P):
            page_copy(step, slot, p).start()

    @pl.when(i == 0)
    def _():
        start(0, 0)

    @pl.when(i + 1 < n_steps)
    def _():
        start(i + 1, (i + 1) % 2)

    slot = i % 2
    for p in range(P):
        page_copy(i, slot, p).wait()

    bref = buf.at[slot]
    for c in range(2):
        acc = jnp.zeros((N_KV * P, 2 * CMP_HIDDEN), F32)
        for dp in range(HEAD_DIM // 2):
            rows = []
            for g in range(N_KV):
                halves = []
                for dd in range(2):
                    r = g * 2 * HEAD_DIM + c * HEAD_DIM + 2 * dp + dd
                    pr = c * HEAD_DIM + 2 * dp + dd
                    halves.append(bref[r] + pet_ref[pr:pr + 1, :])
                rows.append(jnp.concatenate(halves, axis=1))
            lhs = jnp.concatenate(rows, axis=0).astype(BF16)
            acc = acc + jnp.dot(lhs, w1_ref[c, dp], preferred_element_type=F32)
        hid = _gelu_tanh(acc).astype(BF16)
        oc = jnp.dot(hid, w2_ref[c], preferred_element_type=F32).astype(BF16)
        for g in range(N_KV):
            out_ref[g, c] = oc[g * P:(g + 1) * P]


def _compress(ids, src, pet, w1bd, w2bd, *, P):
    n_total = ids.shape[0]
    n_steps = n_total // P
    grid_spec = pltpu.PrefetchScalarGridSpec(
        num_scalar_prefetch=1,
        grid=(n_steps,),
        in_specs=[
            pl.BlockSpec(memory_space=pl.ANY),
            pl.BlockSpec(pet.shape, lambda i, ids: (0, 0)),
            pl.BlockSpec(w1bd.shape, lambda i, ids: (0, 0, 0, 0)),
            pl.BlockSpec(w2bd.shape, lambda i, ids: (0, 0, 0)),
        ],
        out_specs=pl.BlockSpec((N_KV, 2, P, PAGE_SIZE), lambda i, ids: (0, 0, i, 0)),
        scratch_shapes=[
            pltpu.VMEM((2, KV_WIDTH, P, PAGE_SIZE), F32),
            pltpu.SemaphoreType.DMA((2,)),
        ],
    )
    return pl.pallas_call(
        functools.partial(_compress_kernel, P=P, n_steps=n_steps),
        grid_spec=grid_spec,
        out_shape=jax.ShapeDtypeStruct((N_KV, 2, n_total, PAGE_SIZE), BF16),
        compiler_params=_cparams(("arbitrary",), V7X_VMEM_LIMIT),
        name="compress",
    )(ids, src, pet, w1bd, w2bd)


def _compress_weights(cmp_pe, cmp_w1, cmp_w2):
    eye = jnp.eye(BLOCKS_PER_PAGE, dtype=F32)
    pet = jnp.tile(jnp.transpose(cmp_pe, (1, 2, 0)).reshape(2 * HEAD_DIM, BLK), (1, BLOCKS_PER_PAGE))
    w1 = cmp_w1.reshape(2, BLK, HEAD_DIM // 2, 2, CMP_HIDDEN)
    w1bd = jnp.einsum('cluzh,jk->cuzjlkh', w1, eye).reshape(
        2, HEAD_DIM // 2, 2 * PAGE_SIZE, BLOCKS_PER_PAGE * CMP_HIDDEN).astype(BF16)
    w2bd = jnp.einsum('chd,jk->cjhkd', cmp_w2, eye).reshape(
        2, BLOCKS_PER_PAGE * CMP_HIDDEN, BLOCKS_PER_PAGE * HEAD_DIM).astype(BF16)
    return pet, w1bd, w2bd


def _rank_select_t(score, nblk, n_select):
    tq, W = score.shape
    x = score.T
    groups = [x[8 * k:8 * k + 8] for k in range(nblk // 8)]
    ranks = [jnp.zeros((8, tq), jnp.int32) for _ in groups]
    row = lax.broadcasted_iota(jnp.int32, (8, tq), 0)
    for i in range(nblk):
        xi = groups[i // 8][i % 8:i % 8 + 1, :]
        for k, xg in enumerate(groups):
            if 8 * k + 7 < i:
                ahead = xi > xg
            elif 8 * k > i:
                ahead = xi >= xg
            else:
                ahead = (xi > xg) | ((xi == xg) & (row > i - 8 * k))
            ranks[k] = ranks[k] + jnp.where(ahead, 1, 0)
    sel = [jnp.where(r < n_select, 1.0, 0.0) for r in ranks]
    if nblk < W:
        sel.append(jnp.zeros((W - nblk, tq), F32))
    return jnp.concatenate(sel, axis=0).T


def _key_consts(T, npg):
    assert T <= 256 * BLK and npg * BLOCKS_PER_PAGE <= HEAD_DIM
    k = jnp.arange(T, dtype=jnp.int32)[None, :]
    r = jnp.arange(HEAD_DIM, dtype=jnp.int32)[:, None]
    pos = jnp.where(r == 0, k // BLK, jnp.where(r == 1, k % BLK, 0)).astype(F32)
    en = jnp.where(k // BLK == r, NEG, 0.0)
    return jnp.concatenate([pos, en, jnp.zeros((HEAD_DIM, T), F32)], axis=0).astype(BF16)


M_INIT = -3e38


def _p_attn_kernel(q_ref, gate_ref, comp_ref, ks_ref, kw_ref, kc_ref, o_ref,
                   qx, m_ref, acc_ref, out_ref, *, tq, tk, npg, row_chunk):
    qi = pl.program_id(1)
    q0 = qi * tq
    pos_t = q0 + lax.broadcasted_iota(jnp.int32, (tq, 1), 0)
    nblk = npg * BLOCKS_PER_PAGE
    W = 128
    R = HPG * tq
    lane_h = lax.broadcasted_iota(jnp.int32, (1, HEAD_DIM), 1)

    for hh in range(N_HEADS):
        g, h = divmod(hh, HPG)
        rows = pl.ds(h * tq, tq)
        qx[g, rows, 0:HEAD_DIM] = q_ref[0, :, hh * HEAD_DIM:(hh + 1) * HEAD_DIM]
        posc = jnp.where(lane_h == 0, _slope(hh) * BLK, jnp.where(lane_h == 1, _slope(hh), 0.0))
        qx[g, rows, HEAD_DIM:2 * HEAD_DIM] = jnp.broadcast_to(posc, (tq, HEAD_DIM)).astype(BF16)
        qx[g, rows, 3 * HEAD_DIM:4 * HEAD_DIM] = jnp.zeros((tq, HEAD_DIM), BF16)

    blk_n = lax.broadcasted_iota(jnp.int32, (1, W), 1)
    first = blk_n < nblk
    end_pos = (blk_n + 1) * BLK - 1
    dist_c = (pos_t - end_pos).astype(F32)
    mask_c = (dist_c >= 0) & first
    cur = pos_t // BLK
    row_w = lax.broadcasted_iota(jnp.int32, (W, W), 0)
    col_w = lax.broadcasted_iota(jnp.int32, (W, W), 1)
    perm = jnp.where((col_w == (row_w % 2) * npg + row_w // 2) & (row_w < nblk), 1.0, 0.0).astype(BF16)

    for g in range(N_KV):
        ck = comp_ref[g, 0].astype(F32)
        cv = comp_ref[g, 1].astype(F32)
        pad = [jnp.zeros((W - nblk, HEAD_DIM), F32)] if nblk < W else []
        ckp = jnp.concatenate([ck[:, :HEAD_DIM], ck[:, HEAD_DIM:]] + pad, axis=0).astype(BF16)
        cvp = jnp.concatenate([cv[:, :HEAD_DIM], cv[:, HEAD_DIM:]] + pad, axis=0).astype(BF16)
        ckp = jnp.dot(perm, ckp, preferred_element_type=F32).astype(BF16)
        cvp = jnp.dot(perm, cvp, preferred_element_type=F32).astype(BF16)
        imp = jnp.zeros((tq, W), F32)
        for h in range(HPG):
            hh = g * HPG + h
            qh = qx[g, pl.ds(h * tq, tq), 0:HEAD_DIM]
            s = lax.dot_general(qh, ckp, NT_DIMS, preferred_element_type=F32)
            s = jnp.where(mask_c, s - _slope(hh) * dist_c, NEG)
            mx = jnp.max(s, axis=1, keepdims=True)
            e = jnp.where(mask_c, jnp.exp(s - mx), 0.0)
            p = e / jnp.maximum(jnp.sum(e, axis=1, keepdims=True), 1e-30)
            imp = imp + p
            oc = jnp.dot(p.astype(BF16), cvp, preferred_element_type=F32)
            gcol = gate_ref[0, :, hh:hh + 1]
            out_ref[:, hh * HEAD_DIM:(hh + 1) * HEAD_DIM] = gcol * oc
        forced = (blk_n == 0) | (blk_n == cur) | (blk_n == cur - 1)
        score = jnp.where(forced, FORCE_SCORE, imp)
        score = jnp.where(blk_n > cur, -1.0, score)
        sel = _rank_select_t(score, nblk, N_SELECT)
        notsel = jnp.where((sel > 0.5) & (blk_n <= cur), 0.0, 1.0).astype(BF16)
        for h in range(HPG):
            qx[g, pl.ds(h * tq, tq), 2 * HEAD_DIM:3 * HEAD_DIM] = notsel[:, :HEAD_DIM]

    ones_rows = jnp.where(lax.broadcasted_iota(jnp.int32, (HEAD_DIM, 1), 0) == 0, 1.0, 0.0)
    pos_rows = q0 + lax.broadcasted_iota(jnp.int32, (R, 1), 0) % tq

    def attend_tile(kv_ref, k0, width, kc_rows, mode):
        kpos = k0 + lax.broadcasted_iota(jnp.int32, (1, width), 1)
        kc_tile = kc_ref[0:kc_rows, pl.ds(k0, width)]
        ones_tile = jnp.broadcast_to(ones_rows, (HEAD_DIM, width)).astype(BF16)
        if mode == "causal":
            keep = kpos <= pos_rows
        elif mode == "recent":
            keep = kpos > pos_rows - WINDOW
        for g in range(N_KV):
            kt = kv_ref[0, 0, g * 2 * HEAD_DIM:g * 2 * HEAD_DIM + HEAD_DIM, pl.ds(k0, width)]
            vt = kv_ref[0, 0, g * 2 * HEAD_DIM + HEAD_DIM:(g + 1) * 2 * HEAD_DIM, pl.ds(k0, width)]
            kext = jnp.concatenate([kt, kc_tile], axis=0)
            vext = jnp.concatenate([vt, ones_tile], axis=0)
            for c in range(R // row_chunk):
                rc = pl.ds(c * row_chunk, row_chunk)
                s = jnp.dot(qx[g, rc, 0:HEAD_DIM + kc_rows], kext, preferred_element_type=F32)
                if mode != "full":
                    s = jnp.where(keep[c * row_chunk:(c + 1) * row_chunk], s, NEG)
                m_prev = m_ref[g, rc, :]
                m_next = jnp.maximum(m_prev, jnp.max(s, axis=1, keepdims=True))
                alpha = jnp.exp(m_prev - m_next)
                m_ref[g, rc, :] = m_next
                p = jnp.exp(s - jnp.concatenate([m_next] * (width // W), axis=1)).astype(BF16)
                pv = lax.dot_general(p, vext, NT_DIMS, preferred_element_type=F32)
                acc_ref[g, rc, :] = alpha * acc_ref[g, rc, :] + pv

    def reset():
        for g in range(N_KV):
            m_ref[g] = jnp.full((R, W), M_INIT, F32)
            acc_ref[g] = jnp.zeros((R, W), F32)

    def finish(gate_base):
        for g in range(N_KV):
            for h in range(HPG):
                hh = g * HPG + h
                a = acc_ref[g, pl.ds(h * tq, tq), :]
                o = a[:, :HEAD_DIM] / jnp.maximum(a[:, HEAD_DIM:HEAD_DIM + 1], 1e-30)
                gcol = gate_ref[0, :, gate_base + hh:gate_base + hh + 1]
                out_ref[:, hh * HEAD_DIM:(hh + 1) * HEAD_DIM] += gcol * o

    reset()
    n_sel = (q0 + tq + tk - 1) // tk

    def sel_body(j, carry):
        attend_tile(ks_ref, pl.multiple_of(j * tk, tk), tk, 3 * HEAD_DIM, "full")
        return carry

    lax.fori_loop(0, n_sel - 1, sel_body, 0)
    attend_tile(ks_ref, pl.multiple_of((n_sel - 1) * tk, tk), tk, 3 * HEAD_DIM, "causal")
    finish(N_HEADS)

    reset()
    wt = WINDOW // tq

    @pl.when(qi >= wt)
    def _():
        attend_tile(kw_ref, pl.multiple_of((qi - wt) * tq, tq), WINDOW, HEAD_DIM, "recent")

    @pl.when(qi < wt)
    def _():
        def win_body(j, carry):
            attend_tile(kw_ref, pl.multiple_of(j * tq, tq), tq, HEAD_DIM, "full")
            return carry

        lax.fori_loop(0, qi, win_body, 0)

    attend_tile(kw_ref, pl.multiple_of(q0, tq), tq, HEAD_DIM, "causal")
    finish(2 * N_HEADS)
    o_ref[0] = out_ref[...].astype(BF16)


def _p_attn(q, gates, comp, kvtb, kconst, *, tq, tk):
    B, T, _ = q.shape
    npg = T // PAGE_SIZE
    assert WINDOW % tq == 0 and tk % tq == 0
    return pl.pallas_call(
        functools.partial(_p_attn_kernel, tq=tq, tk=tk, npg=npg, row_chunk=min(512, HPG * tq)),
        grid=(B, T // tq),
        in_specs=[
            pl.BlockSpec((1, tq, ATTN_WIDTH), lambda b, i: (b, i, 0)),
            pl.BlockSpec((1, tq, N_GATE_PAD), lambda b, i: (b, i, 0)),
            pl.BlockSpec((N_KV, 2, npg, PAGE_SIZE), lambda b, i: (0, 0, b, 0)),
            pl.BlockSpec((1, 1, KV_WIDTH, T), lambda b, i: (0, b, 0, 0)),
            pl.BlockSpec((1, 1, KV_WIDTH, T), lambda b, i: (1, b, 0, 0)),
            pl.BlockSpec(kconst.shape, lambda b, i: (0, 0)),
        ],
        out_specs=pl.BlockSpec((1, tq, ATTN_WIDTH), lambda b, i: (b, i, 0)),
        out_shape=jax.ShapeDtypeStruct((B, T, ATTN_WIDTH), BF16),
        scratch_shapes=[
            pltpu.VMEM((N_KV, HPG * tq, 4 * HEAD_DIM), BF16),
            pltpu.VMEM((N_KV, HPG * tq, 128), F32),
            pltpu.VMEM((N_KV, HPG * tq, 128), F32),
            pltpu.VMEM((tq, ATTN_WIDTH), F32),
        ],
        compiler_params=_cparams(("arbitrary", "arbitrary"), V7X_VMEM_LIMIT),
        name="p_attn",
    )(q, gates, comp, kvtb, kvtb, kconst)


def _stack_heads(q_ref, g, ts):
    parts = [q_ref[0, :, (g * HPG + h) * HEAD_DIM:(g * HPG + h + 1) * HEAD_DIM].astype(F32) for h in range(HPG)]
    return jnp.concatenate(parts, axis=0).astype(BF16)


def _row_consts(g, ts):
    R = HPG * ts
    row = lax.broadcasted_iota(jnp.int32, (R, 1), 0)
    t_row = row % ts
    h_row = row // ts
    slope = jnp.zeros((R, 1), F32)
    for h in range(HPG):
        slope = jnp.where(h_row == h, _slope(g * HPG + h), slope)
    return t_row, slope


def _s_select_kernel(q_ref, gate_ref, comp_ref, cnew_ref, oc_ref, bits_ref, out_s, *, past, ts):
    W = 128
    nb_past = past // BLK
    lane = lax.broadcasted_iota(jnp.int32, (1, W), 1)
    idx_fns = [lambda c: 2 * c, lambda c: 2 * c + 1, lambda c: jnp.where(c == 0, nb_past, (1 << 20) + c)]
    idx_tiles = [f(lane) for f in idx_fns]
    valid_n = lane == 0
    tok = lax.broadcasted_iota(jnp.int32, (ts, 1), 0)
    cur_t = (past + tok) // BLK
    for g in range(N_KV):
        qg = _stack_heads(q_ref, g, ts)
        t_row, slope = _row_consts(g, ts)
        pos = past + t_row
        ck = comp_ref[g, 0]
        cv = comp_ref[g, 1]
        cn = cnew_ref[g, 0, 0].astype(F32)[:, :HEAD_DIM]
        vn = cnew_ref[g, 1, 0].astype(F32)[:, :HEAD_DIM]
        s_t = [lax.dot_general(qg, ck[:, :HEAD_DIM], NT_DIMS, preferred_element_type=F32),
               lax.dot_general(qg, ck[:, HEAD_DIM:], NT_DIMS, preferred_element_type=F32),
               jnp.broadcast_to(jnp.sum(qg.astype(F32) * cn, axis=1, keepdims=True), (HPG * ts, W))]
        masks, es = [], []
        for k in range(3):
            end_pos = (idx_tiles[k] + 1) * BLK - 1
            dist = (pos - end_pos).astype(F32)
            mk = dist >= 0
            if k == 2:
                mk = mk & valid_n
            masks.append(mk)
            s_t[k] = jnp.where(mk, s_t[k] - slope * dist, NEG)
        mx = jnp.maximum(jnp.maximum(jnp.max(s_t[0], axis=1, keepdims=True),
                                     jnp.max(s_t[1], axis=1, keepdims=True)),
                         jnp.max(s_t[2], axis=1, keepdims=True))
        for k in range(3):
            es.append(jnp.where(masks[k], jnp.exp(s_t[k] - mx), 0.0))
        den = (jnp.sum(es[0], axis=1, keepdims=True) + jnp.sum(es[1], axis=1, keepdims=True)
               + jnp.sum(es[2], axis=1, keepdims=True))
        inv = 1.0 / jnp.maximum(den, 1e-30)
        ps = [e * inv for e in es]
        o_c = (jnp.dot(ps[0].astype(BF16), cv[:, :HEAD_DIM], preferred_element_type=F32)
               + jnp.dot(ps[1].astype(BF16), cv[:, HEAD_DIM:], preferred_element_type=F32)
               + ps[2][:, 0:1].astype(BF16).astype(F32) * vn)
        for h in range(HPG):
            hh = g * HPG + h
            gcol = gate_ref[0, :, hh:hh + 1]
            out_s[:, hh * HEAD_DIM:(hh + 1) * HEAD_DIM] = gcol * o_c[h * ts:(h + 1) * ts]
        scores = []
        for k in range(3):
            imp = ps[k][0:ts]
            for h in range(1, HPG):
                imp = imp + ps[k][h * ts:(h + 1) * ts]
            idx = idx_tiles[k]
            forced = (idx == 0) | (idx == cur_t) | (idx == cur_t - 1)
            sc = jnp.where(forced, FORCE_SCORE, imp)
            sc = jnp.where(idx > cur_t, -1.0, sc)
            if k == 2:
                sc = jnp.where(valid_n, sc, -2.0)
            scores.append(sc)
        ranks = [jnp.zeros((ts, W), jnp.int32) for _ in range(3)]
        for kb in range(3):
            for r in range(W):
                y = pltpu.roll(scores[kb], r, 1) if r else scores[kb]
                yi = idx_fns[kb]((lane - r) & (W - 1))
                for ka in range(3):
                    if ka == kb and r == 0:
                        continue
                    ahead = (y > scores[ka]) | ((y == scores[ka]) & (yi < idx_tiles[ka]))
                    ranks[ka] = ranks[ka] + ahead.astype(jnp.int32)
        sel = [(ranks[k] < N_SELECT) for k in range(3)]
        wt = jnp.left_shift(1, 2 * tok)
        page_bits = jnp.sum(jnp.where(sel[0], wt, 0) + jnp.where(sel[1], 2 * wt, 0), axis=0, keepdims=True)
        new_bits = jnp.sum(jnp.where(sel[2] & valid_n, jnp.left_shift(1, tok), 0), axis=0, keepdims=True)
        bits_ref[0, g, 0:1, :] = page_bits
        bits_ref[0, g, 1:2, :] = new_bits
    oc_ref[0] = out_s[...]


def _s_select(q, gates, comp, cnew, *, past):
    Bs, ts, _ = q.shape
    npg = past // PAGE_SIZE
    assert npg == 128, "one lane tile of pages per sequence"
    return pl.pallas_call(
        functools.partial(_s_select_kernel, past=past, ts=ts),
        grid=(Bs,),
        in_specs=[
            pl.BlockSpec((1, ts, ATTN_WIDTH), lambda b: (b, 0, 0)),
            pl.BlockSpec((1, ts, N_GATE_PAD), lambda b: (b, 0, 0)),
            pl.BlockSpec((N_KV, 2, npg, PAGE_SIZE), lambda b: (0, 0, b, 0)),
            pl.BlockSpec((N_KV, 2, 1, 1, PAGE_SIZE), lambda b: (0, 0, b, 0, 0)),
        ],
        out_specs=[
            pl.BlockSpec((1, ts, ATTN_WIDTH), lambda b: (b, 0, 0)),
            pl.BlockSpec((1, N_KV, 2, 128), lambda b: (b, 0, 0, 0)),
        ],
        out_shape=[
            jax.ShapeDtypeStruct((Bs, ts, ATTN_WIDTH), F32),
            jax.ShapeDtypeStruct((Bs, N_KV, 2, 128), jnp.int32),
        ],
        scratch_shapes=[pltpu.VMEM((ts, ATTN_WIDTH), F32)],
        compiler_params=_cparams(("arbitrary",), V7X_VMEM_LIMIT),
        name="s_select",
    )(q, gates, comp, cnew)


def _s_attn_kernel(pt_ref, bits_ref, q_ref, gate_ref, oc_ref, win_ref, knew_ref, slc_ref,
                   o_ref, wout_ref, buf, sem, plist, cnts, out_s, *, past, ts, npg, chunk):
    b = pl.program_id(0)
    g = pl.program_id(1)
    W = 128
    R = HPG * ts
    wk = win_ref.shape[2]
    lin = b * N_KV + g
    n_lin = pl.num_programs(0) * N_KV
    base = lin * 2 * W
    slot = lin % 2

    def page_copy(page, gg, sl, idx):
        return pltpu.make_async_copy(slc_ref.at[page, pl.ds(gg * 2 * HEAD_DIM, 2 * HEAD_DIM)],
                                     buf.at[sl, idx], sem.at[sl])

    def issue_all(ln, sl):
        bb = ln // N_KV
        gg = ln % N_KV

        def issue(p, cnt):
            hit = bits_ref[ln * 2 * W + p] != 0

            @pl.when(hit)
            def _():
                page_copy(pt_ref[bb * npg + p], gg, sl, cnt).start()
                plist[sl * npg + cnt] = p

            return cnt + hit.astype(jnp.int32)

        cnts[sl] = lax.fori_loop(0, npg, issue, 0, unroll=8)

    @pl.when(lin == 0)
    def _():
        buf[...] = jnp.zeros(buf.shape, F32)
        for k in range(2 * npg):
            plist[k] = 0
        issue_all(lin, slot)

    @pl.when(lin + 1 < n_lin)
    def _():
        issue_all(lin + 1, 1 - slot)

    cnt = cnts[slot]

    q_all = q_ref[0].astype(F32)
    qg = jnp.concatenate([q_all[:, h * HEAD_DIM:(h + 1) * HEAD_DIM] for h in range(HPG)], axis=0).astype(BF16)
    row = lax.broadcasted_iota(jnp.int32, (R, 1), 0)
    t_row = row % ts
    head_row = row // ts + g * HPG
    slope = jnp.zeros((R, 1), F32)
    for hh in range(N_HEADS):
        slope = jnp.where(head_row == hh, _slope(hh), slope)
    lane = lax.broadcasted_iota(jnp.int32, (1, W), 1)

    kw = win_ref[0, 0:HEAD_DIM, :].astype(BF16)
    vw = win_ref[0, HEAD_DIM:2 * HEAD_DIM, :].astype(BF16)
    kn = knew_ref[2, 0, 0:HEAD_DIM, :].astype(BF16)
    vn = knew_ref[2, 0, HEAD_DIM:2 * HEAD_DIM, :].astype(BF16)
    i_st = lax.broadcasted_iota(jnp.int32, (1, wk), 1)
    d_st = t_row + wk - i_st
    m_st = (d_st >= 0) & (d_st < WINDOW)
    d_nw = t_row - lane
    m_nw = (d_nw >= 0) & (d_nw < WINDOW) & (lane < ts)
    s_st = jnp.where(m_st, jnp.dot(qg, kw, preferred_element_type=F32) - slope * d_st.astype(F32), NEG)
    s_nw = jnp.where(m_nw, jnp.dot(qg, kn, preferred_element_type=F32) - slope * d_nw.astype(F32), NEG)
    mx = jnp.maximum(jnp.max(s_st, axis=1, keepdims=True), jnp.max(s_nw, axis=1, keepdims=True))
    e_st = jnp.where(m_st, jnp.exp(s_st - mx), 0.0)
    e_nw = jnp.where(m_nw, jnp.exp(s_nw - mx), 0.0)
    den = jnp.sum(e_st, axis=1, keepdims=True) + jnp.sum(e_nw, axis=1, keepdims=True)
    inv = 1.0 / jnp.maximum(den, 1e-30)
    o_w = (lax.dot_general((e_st * inv).astype(BF16), vw, NT_DIMS, preferred_element_type=F32)
           + lax.dot_general((e_nw * inv).astype(BF16), vn, NT_DIMS, preferred_element_type=F32))

    last = wk - W
    shifted = pltpu.roll(win_ref[0], wk - ts, 1)
    newr = pltpu.roll(knew_ref[2, 0], W - ts, 1)
    wout_ref[0, :, 0:last] = shifted[:, 0:last]
    wout_ref[0, :, last:wk] = jnp.where(lane >= W - ts, newr, shifted[:, last:wk])

    bits_new = bits_ref[base + W]
    ksn = knew_ref[1, 0, 0:HEAD_DIM, :].astype(BF16)
    vsn = knew_ref[1, 0, HEAD_DIM:2 * HEAD_DIM, :].astype(BF16)
    d_sn = t_row - lane
    m_sn = (d_sn >= 0) & (lane < ts) & ((jnp.right_shift(bits_new, t_row) & 1) == 1)
    s_sn = jnp.where(m_sn, jnp.dot(qg, ksn, preferred_element_type=F32) - slope * d_sn.astype(F32), NEG)
    m0 = jnp.max(s_sn, axis=1, keepdims=True)
    e0 = jnp.where(m_sn, jnp.exp(s_sn - m0), 0.0)
    l0 = jnp.sum(e0, axis=1, keepdims=True)
    a0 = lax.dot_general(e0.astype(BF16), vsn, NT_DIMS, preferred_element_type=F32)

    def wait_one(s, c):
        page_copy(0, 0, slot, s).wait()
        return c

    lax.fori_loop(0, cnt, wait_one, 0)
    sh = 2 * t_row + (lane >= BLK).astype(jnp.int32)
    sh_c = jnp.concatenate([sh] * chunk, axis=1)
    bslot = buf.at[slot]

    def chunk_step(c, carry):
        m, l, acc = carry
        kts, vts, kpos, bitv = [], [], [], []
        for u in range(chunk):
            s = c * chunk + u
            ok = s < cnt
            p = jnp.where(ok, plist[slot * npg + s], 0)
            bits = jnp.where(ok, bits_ref[base + p], 0)
            kts.append(bslot[s, 0:HEAD_DIM, :])
            vts.append(bslot[s, HEAD_DIM:2 * HEAD_DIM, :])
            kpos.append(p * PAGE_SIZE + lane)
            bitv.append(jnp.zeros((1, W), jnp.int32) + bits)
        kt = jnp.concatenate(kts, axis=1).astype(BF16)
        vt = jnp.concatenate(vts, axis=1).astype(BF16)
        dist = past + t_row - jnp.concatenate(kpos, axis=1)
        mask = ((jnp.right_shift(jnp.concatenate(bitv, axis=1), sh_c) & 1) == 1) & (dist >= 0)
        sc = jnp.where(mask, jnp.dot(qg, kt, preferred_element_type=F32) - slope * dist.astype(F32), NEG)
        m_new = jnp.maximum(m, jnp.max(sc, axis=1, keepdims=True))
        alpha = jnp.exp(m - m_new)
        e = jnp.where(mask, jnp.exp(sc - m_new), 0.0)
        l = alpha * l + jnp.sum(e, axis=1, keepdims=True)
        acc = alpha * acc + lax.dot_general(e.astype(BF16), vt, NT_DIMS, preferred_element_type=F32)
        return m_new, l, acc

    m, l, acc = lax.fori_loop(0, (cnt + chunk - 1) // chunk, chunk_step, (m0, l0, a0))
    o_s = acc / jnp.maximum(l, 1e-30)

    g_vec = jnp.zeros((ts, 1), jnp.int32) + g
    for h in range(HPG):
        rows = slice(h * ts, (h + 1) * ts)
        g_s = jnp.zeros((ts, 1), F32)
        g_w = jnp.zeros((ts, 1), F32)
        for gg in range(N_KV):
            hh = gg * HPG + h
            g_s = jnp.where(g_vec == gg, gate_ref[0, :, N_HEADS + hh:N_HEADS + hh + 1], g_s)
            g_w = jnp.where(g_vec == gg, gate_ref[0, :, 2 * N_HEADS + hh:2 * N_HEADS + hh + 1], g_w)
        out_s[:, h * HEAD_DIM:(h + 1) * HEAD_DIM] = g_s * o_s[rows] + g_w * o_w[rows]
    o_ref[0] = oc_ref[0] + out_s[...]


def _s_attn(page_table, bits, q, gates, oc, win_t, knew, slc_t, *, past):
    Bs, ts, _ = q.shape
    npg = past // PAGE_SIZE
    wk = win_t.shape[2]
    gw = HPG * HEAD_DIM
    grid_spec = pltpu.PrefetchScalarGridSpec(
        num_scalar_prefetch=2,
        grid=(Bs, N_KV),
        in_specs=[
            pl.BlockSpec((1, ts, gw), lambda b, g, *_: (b, 0, g)),
            pl.BlockSpec((1, ts, N_GATE_PAD), lambda b, g, *_: (b, 0, 0)),
            pl.BlockSpec((1, ts, gw), lambda b, g, *_: (b, 0, g)),
            pl.BlockSpec((1, 2 * HEAD_DIM, wk), lambda b, g, *_: (b, g, 0)),
            pl.BlockSpec((3, 1, 2 * HEAD_DIM, PAGE_SIZE), lambda b, g, *_: (0, b, g, 0)),
            pl.BlockSpec(memory_space=pl.ANY),
        ],
        out_specs=[
            pl.BlockSpec((1, ts, gw), lambda b, g, *_: (b, 0, g)),
            pl.BlockSpec((1, 2 * HEAD_DIM, wk), lambda b, g, *_: (b, g, 0)),
        ],
        scratch_shapes=[
            pltpu.VMEM((2, npg, 2 * HEAD_DIM, PAGE_SIZE), F32),
            pltpu.SemaphoreType.DMA((2,)),
            pltpu.SMEM((2 * npg,), jnp.int32),
            pltpu.SMEM((2,), jnp.int32),
            pltpu.VMEM((ts, gw), F32),
        ],
    )
    chunk = 8
    assert npg % chunk == 0
    return pl.pallas_call(
        functools.partial(_s_attn_kernel, past=past, ts=ts, npg=npg, chunk=chunk),
        grid_spec=grid_spec,
        out_shape=[
            jax.ShapeDtypeStruct((Bs, ts, ATTN_WIDTH), F32),
            jax.ShapeDtypeStruct(win_t.shape, F32),
        ],
        compiler_params=_cparams(("arbitrary", "arbitrary"), V7X_VMEM_LIMIT),
        name="s_attn",
    )(page_table.reshape(-1), bits.reshape(-1), q, gates, oc, win_t, knew, slc_t)


def _pool_kernel(u_ref, halo_ref, wp_ref, ps_ref, o_ref, ext, *, tp, tiles_per_seq, pos_base):
    i = pl.program_id(0)
    ext[0:HALO, :] = halo_ref[0]
    ext[HALO:HALO + tp, :] = u_ref[0]
    pos = pos_base + (i % tiles_per_seq) * tp + lax.broadcasted_iota(jnp.int32, (tp, 1), 0)
    for gi, w in enumerate(POOL_WINDOWS):
        lanes = slice(gi * POOL_GROUP, (gi + 1) * POOL_GROUP)
        own = ext[HALO:HALO + tp, lanes]
        tot = own
        for k in range(1, w):
            tot = tot + ext[HALO - k:HALO - k + tp, lanes]
        cnt = jnp.minimum(w, pos + 1).astype(F32)
        d = (tot / cnt - own).astype(BF16)
        y = jnp.dot(d, wp_ref[gi], preferred_element_type=F32)
        o_ref[0, :, lanes] = (y * ps_ref[:, lanes]).astype(o_ref.dtype)


def _pool(u, halo, w_pool, pool_scale, *, tiles_per_seq, pos_base, out_dtype):
    n_tiles, tp, _ = u.shape
    return pl.pallas_call(
        functools.partial(_pool_kernel, tp=tp, tiles_per_seq=tiles_per_seq, pos_base=pos_base),
        grid=(n_tiles,),
        in_specs=[
            pl.BlockSpec((1, tp, POOL_WIDTH), lambda i: (i, 0, 0)),
            pl.BlockSpec((1, HALO, POOL_WIDTH), lambda i: (i, 0, 0)),
            pl.BlockSpec(w_pool.shape, lambda i: (0, 0, 0)),
            pl.BlockSpec((1, POOL_WIDTH), lambda i: (0, 0)),
        ],
        out_specs=pl.BlockSpec((1, tp, POOL_WIDTH), lambda i: (i, 0, 0)),
        out_shape=jax.ShapeDtypeStruct((n_tiles, tp, POOL_WIDTH), out_dtype),
        scratch_shapes=[pltpu.VMEM((HALO + tp, POOL_WIDTH), F32)],
        compiler_params=_cparams(("arbitrary",), V7X_VMEM_LIMIT),
        name="pool",
    )(u, halo, w_pool, pool_scale)


def _rms(x, g):
    return (x * lax.rsqrt(jnp.mean(x * x, axis=-1, keepdims=True) + NORM_EPS)) * g


def _mlp_kernel(x_ref, o_ref, p_ref, wo_ref, gm_ref, wu_ref, wd_ref, gf_ref, y_ref, *, ff_chunk):
    mix = jnp.concatenate([o_ref[...], p_ref[...]], axis=1)
    h = x_ref[...] + jnp.dot(mix, wo_ref[...], preferred_element_type=F32)
    hn = _rms(h, gm_ref[...]).astype(BF16)
    acc = jnp.zeros(h.shape, F32)
    for c in range(D_FF // ff_chunk):
        a = jnp.dot(hn, wu_ref[:, c * ff_chunk:(c + 1) * ff_chunk], preferred_element_type=F32)
        a = jnp.square(jnp.maximum(a, 0.0)).astype(BF16)
        acc = acc + jnp.dot(a, wd_ref[c * ff_chunk:(c + 1) * ff_chunk, :], preferred_element_type=F32)
    y_ref[...] = _rms(h + acc, gf_ref[...])


def _mlp(x, o, p, w_out, g_mlp, w_up, w_down, g_final, *, tm):
    N = x.shape[0]
    const = lambda i: (0, 0)
    resident = lambda a: pl.BlockSpec(a.shape, const, pipeline_mode=pl.Buffered(1))
    return pl.pallas_call(
        functools.partial(_mlp_kernel, ff_chunk=1024),
        grid=(N // tm,),
        in_specs=[
            pl.BlockSpec((tm, D_MODEL), lambda i: (i, 0)),
            pl.BlockSpec((tm, ATTN_WIDTH), lambda i: (i, 0)),
            pl.BlockSpec((tm, POOL_WIDTH), lambda i: (i, 0)),
            resident(w_out), resident(g_mlp), resident(w_up), resident(w_down), resident(g_final),
        ],
        out_specs=pl.BlockSpec((tm, D_MODEL), lambda i: (i, 0)),
        out_shape=jax.ShapeDtypeStruct((N, D_MODEL), F32),
        compiler_params=_cparams(("arbitrary",), V7X_VMEM_LIMIT),
        name="mlp",
    )(x, o, p, w_out, g_mlp, w_up, w_down, g_final)


def _rows_from_kt(kt):
    B, _, S = kt.shape
    return jnp.transpose(kt.reshape(B, N_KV, 2, HEAD_DIM, S), (0, 4, 1, 2, 3))[None]


def _prep_weights(w_in, cmp_pe, cmp_w1, cmp_w2):
    w_t = jnp.transpose(w_in)
    a, b = ATTN_WIDTH, ATTN_WIDTH + 3 * KV_WIDTH
    wq, wkv = w_t[:a], w_t[a:b]
    wg, wu = w_t[b:b + 3 * N_HEADS], w_t[b + 3 * N_HEADS:]
    wg = jnp.pad(wg, ((0, N_GATE_PAD - 3 * N_HEADS), (0, 0)))
    wtok = jnp.concatenate([wq, wu, wg], axis=0).astype(BF16)
    wtok_kv = jnp.concatenate([wq, wu, wg, wkv], axis=0).astype(BF16)
    return wtok, wtok_kv, wkv.astype(BF16), _compress_weights(cmp_pe, cmp_w1, cmp_w2)


def _prompt_path(x, wts, g_attn, w_pool, pool_scale, w_out, g_mlp, w_up, w_down, g_final):
    wtok, _, wkv, (pet, w1bd, w2bd) = wts
    B, T, _ = x.shape
    tm = min(512, T)
    q, gates, u, kc_t, ks_t, kw_t, kvtb, pages = _proj(x, g_attn, wtok, wkv, tm=tm, emit_tok_kv=False,
                                                        emit_pages=True)
    n_pages = B * T // PAGE_SIZE
    comp = _compress(jnp.arange(n_pages, dtype=jnp.int32), pages, pet, w1bd, w2bd, P=min(64, n_pages))
    o = _p_attn(q, gates, comp, kvtb, _key_consts(T, T // PAGE_SIZE), tq=min(256, T), tk=min(512, T))
    nt = T // tm
    u4 = u.reshape(B, nt, tm, POOL_WIDTH)
    halo = jnp.concatenate([jnp.zeros((B, 1, HALO, POOL_WIDTH), F32), u4[:, :-1, tm - HALO:, :]], axis=1)
    pool = _pool(u.reshape(B * nt, tm, POOL_WIDTH), halo.reshape(B * nt, HALO, POOL_WIDTH),
                 w_pool, pool_scale, tiles_per_seq=nt, pos_base=0, out_dtype=BF16)
    N = B * T
    y = _mlp(x.reshape(N, D_MODEL), o.reshape(N, ATTN_WIDTH), pool.reshape(N, POOL_WIDTH),
             w_out, g_mlp, w_up, w_down, g_final, tm=tm)
    wk = min(WINDOW, T)
    return (y.reshape(B, T, D_MODEL), _rows_from_kt(kc_t), _rows_from_kt(ks_t),
            _rows_from_kt(kw_t[:, :, T - wk:]), u[None, :, T - POOL_HIST:])


def _kt_view(rows):
    n, s = rows.shape[:2]
    return jnp.transpose(rows, (0, 2, 3, 4, 1)).reshape(n, KV_WIDTH, s)


def _sample_path(x, cache_cmp, cache_slc, state_win, state_pool, page_table, wts, g_attn,
                 w_pool, pool_scale, w_out, g_mlp, w_up, w_down, g_final):
    _, wtok_kv, wkv, (pet, w1bd, w2bd) = wts
    Bs, ts, _ = x.shape
    N = Bs * ts
    npg = page_table.shape[1]
    past = npg * PAGE_SIZE
    q, gates, u, kc_t, ks_t, kw_t, _, kv_tok = _proj(x.reshape(1, N, D_MODEL), g_attn, wtok_kv, wkv, tm=N,
                                                     emit_tok_kv=True, emit_pages=False)
    kvt = jnp.concatenate([kc_t, ks_t, kw_t], axis=0)
    knew = jnp.pad(jnp.transpose(kvt.reshape(3, KV_WIDTH, Bs, ts), (0, 2, 1, 3)),
                   ((0, 0), (0, 0), (0, 0), (0, PAGE_SIZE - ts)))
    comp_past = _compress(page_table.reshape(-1), _kt_view(cache_cmp), pet, w1bd, w2bd, P=min(64, Bs * npg))
    comp_new = _compress(jnp.arange(Bs, dtype=jnp.int32), knew[0], pet, w1bd, w2bd, P=Bs)
    qf = q.reshape(Bs, ts, ATTN_WIDTH).astype(F32)
    gts = gates.reshape(Bs, ts, N_GATE_PAD)
    oc, bits = _s_select(qf, gts, comp_past, comp_new.astype(F32).reshape(N_KV, 2, Bs, 1, PAGE_SIZE), past=past)
    o, win_new = _s_attn(page_table, bits, qf, gts, oc, _kt_view(state_win), knew, _kt_view(cache_slc), past=past)
    u3 = u.reshape(Bs, ts, POOL_WIDTH)
    halo = jnp.pad(state_pool, ((0, 0), (HALO - POOL_HIST, 0), (0, 0)))
    pool = _pool(u3, halo, w_pool, pool_scale, tiles_per_seq=1, pos_base=past, out_dtype=F32)
    y = _mlp(x.reshape(N, D_MODEL), o.reshape(N, ATTN_WIDTH).astype(BF16), pool.reshape(N, POOL_WIDTH).astype(BF16),
             w_out, g_mlp, w_up, w_down, g_final, tm=N)
    kv_rows = kv_tok.reshape(Bs, ts, 3, N_KV, 2, HEAD_DIM)
    s_pool = jnp.concatenate([state_pool, u3], axis=1)[None, :, ts:]
    return (y.reshape(Bs, ts, D_MODEL), kv_rows[None, :, :, 0], kv_rows[None, :, :, 1],
            _rows_from_kt(win_new), s_pool)


def kernel(x_prompt, x_sample, cache_cmp, cache_slc, state_win, state_pool, page_table, g_attn, w_in,
           cmp_pe, cmp_w1, cmp_w2, w_pool, pool_scale, w_out, g_mlp, w_up, w_down, g_final):
    assert g_attn.shape[0] == 1, "single-layer trunk"
    wts = _prep_weights(w_in[0], cmp_pe[0], cmp_w1[0], cmp_w2[0])
    mlp_w = (w_out[0].astype(BF16), g_mlp, w_up[0].astype(BF16), w_down[0].astype(BF16), g_final[None])
    y_p, p_cmp, p_slc, p_win, p_pool = _prompt_path(x_prompt, wts, g_attn, w_pool[0], pool_scale, *mlp_w)
    y_s, s_cmp, s_slc, s_win, s_pool = _sample_path(x_sample, cache_cmp[0], cache_slc[0], state_win[0],
                                                    state_pool[0], page_table, wts, g_attn, w_pool[0],
                                                    pool_scale, *mlp_w)
    return (y_p, y_s, p_cmp, p_slc, p_win, p_pool, s_cmp, s_slc, s_win, s_pool)
```

```python
import functools
import math

import jax
import jax.numpy as jnp
from jax import lax
from jax.experimental import pallas as pl
from jax.experimental.pallas import tpu as pltpu

D_MODEL = 1024
N_HEADS = 8
N_KV = 2
HPG = N_HEADS // N_KV
HEAD_DIM = 64
ATTN_WIDTH = N_HEADS * HEAD_DIM
POOL_WIDTH = D_MODEL - ATTN_WIDTH
POOL_WINDOWS = (2, 4, 8, 16)
POOL_GROUP = POOL_WIDTH // len(POOL_WINDOWS)
POOL_HIST = max(POOL_WINDOWS) - 1
BLK = 64
N_SELECT = 16
WINDOW = 512
CMP_HIDDEN = 2 * HEAD_DIM
D_FF = 4 * D_MODEL
KV_WIDTH = N_KV * 2 * HEAD_DIM
PAGE_SIZE = 128
BLOCKS_PER_PAGE = PAGE_SIZE // BLK
SCALE = HEAD_DIM ** -0.5
FORCE_SCORE = float(HPG + 1)
NORM_EPS = 1e-6
NEG = -1e30
N_GATE_PAD = 128
HALO = 16

F32 = jnp.float32
BF16 = jnp.bfloat16
NT_DIMS = (((1,), (1,)), ((), ()))

V7X_VMEM_LIMIT = 56 * 1024 * 1024


def _slope(head):
    return float(2.0 ** (-8.0 * (head + 1) / N_HEADS))


def _cparams(sem, vmem=None, flags=None):
    return pltpu.CompilerParams(dimension_semantics=sem, vmem_limit_bytes=vmem, flags=flags)


def _proj_kernel(x_ref, g_ref, wtok_ref, wkv_ref, q_ref, gate_ref, u_ref, kc_ref, ks_ref, kw_ref, kvtb_ref, *rest,
                 tm, emit_tok_kv, emit_pages):
    x = x_ref[0]
    ms = jnp.mean(x * x, axis=-1, keepdims=True)
    xn = ((x * lax.rsqrt(ms + NORM_EPS)) * g_ref[...]).astype(BF16)
    tok = lax.dot_general(xn, wtok_ref[...], NT_DIMS, preferred_element_type=F32)
    q_ref[0] = (tok[:, :ATTN_WIDTH] * SCALE).astype(BF16)
    u_ref[0] = tok[:, ATTN_WIDTH:ATTN_WIDTH + POOL_WIDTH]
    gl = tok[:, ATTN_WIDTH + POOL_WIDTH:ATTN_WIDTH + POOL_WIDTH + N_GATE_PAD]
    gate_ref[0] = 1.0 / (1.0 + jnp.exp(-gl))
    kvt = lax.dot_general(wkv_ref[...], xn, NT_DIMS, preferred_element_type=F32)
    for br, ref in enumerate((kc_ref, ks_ref, kw_ref)):
        ref[0] = kvt[br * KV_WIDTH:(br + 1) * KV_WIDTH]
    for br in range(2):
        kvtb_ref[br, 0] = kvt[(br + 1) * KV_WIDTH:(br + 2) * KV_WIDTH].astype(BF16)
    k = 0
    if emit_tok_kv:
        base = ATTN_WIDTH + POOL_WIDTH + N_GATE_PAD
        rest[k][0] = tok[:, base:base + 3 * KV_WIDTH]
        k += 1
    if emit_pages:
        for pg in range(tm // PAGE_SIZE):
            rest[k][pg] = kvt[:KV_WIDTH, pg * PAGE_SIZE:(pg + 1) * PAGE_SIZE]


def _proj(x, g_attn, wtok, wkv, *, tm, emit_tok_kv, emit_pages):
    B, T, _ = x.shape
    nt = T // tm
    out_shape = [
        jax.ShapeDtypeStruct((B, T, ATTN_WIDTH), BF16),
        jax.ShapeDtypeStruct((B, T, N_GATE_PAD), F32),
        jax.ShapeDtypeStruct((B, T, POOL_WIDTH), F32),
    ] + [jax.ShapeDtypeStruct((B, KV_WIDTH, T), F32)] * 3 + [
        jax.ShapeDtypeStruct((2, B, KV_WIDTH, T), BF16),
    ]
    out_specs = [
        pl.BlockSpec((1, tm, ATTN_WIDTH), lambda b, i: (b, i, 0)),
        pl.BlockSpec((1, tm, N_GATE_PAD), lambda b, i: (b, i, 0)),
        pl.BlockSpec((1, tm, POOL_WIDTH), lambda b, i: (b, i, 0)),
    ] + [pl.BlockSpec((1, KV_WIDTH, tm), lambda b, i: (b, 0, i))] * 3 + [
        pl.BlockSpec((2, 1, KV_WIDTH, tm), lambda b, i: (0, b, 0, i)),
    ]
    if emit_tok_kv:
        out_shape.append(jax.ShapeDtypeStruct((B, T, 3 * KV_WIDTH), F32))
        out_specs.append(pl.BlockSpec((1, tm, 3 * KV_WIDTH), lambda b, i: (b, i, 0)))
    if emit_pages:
        ppt = tm // PAGE_SIZE
        out_shape.append(jax.ShapeDtypeStruct((B * T // PAGE_SIZE, KV_WIDTH, PAGE_SIZE), F32))
        out_specs.append(pl.BlockSpec((ppt, KV_WIDTH, PAGE_SIZE), lambda b, i: (b * nt + i, 0, 0)))
    return pl.pallas_call(
        functools.partial(_proj_kernel, tm=tm, emit_tok_kv=emit_tok_kv, emit_pages=emit_pages),
        grid=(B, nt),
        in_specs=[
            pl.BlockSpec((1, tm, D_MODEL), lambda b, i: (b, i, 0)),
            pl.BlockSpec((1, D_MODEL), lambda b, i: (0, 0)),
            pl.BlockSpec(wtok.shape, lambda b, i: (0, 0)),
            pl.BlockSpec(wkv.shape, lambda b, i: (0, 0)),
        ],
        out_specs=out_specs,
        out_shape=out_shape,
        compiler_params=_cparams(("arbitrary", "arbitrary"), V7X_VMEM_LIMIT),
        name="proj",
    )(x, g_attn, wtok, wkv)


def _gelu_tanh(x):
    c = math.sqrt(2.0 / math.pi)
    return 0.5 * x * (1.0 + jnp.tanh(c * (x + 0.044715 * (x * x * x))))


def _compress_kernel(ids_ref, src_ref, pet_ref, w1_ref, w2_ref, out_ref, buf, sem, *, P, n_steps):
    i = pl.program_id(0)

    def page_copy(step, slot, p):
        return pltpu.make_async_copy(src_ref.at[ids_ref[step * P + p]], buf.at[slot, :, p, :], sem.at[slot])

    def start(step, slot):
        for p in range(P):
            page_copy(step, slot, p).start()

    @pl.when(i == 0)
    def _():
        start(0, 0)

    @pl.when(i + 1 < n_steps)
    def _():
        start(i + 1, (i + 1) % 2)

    slot = i % 2
    for p in range(P):
        page_copy(i, slot, p).wait()

    bref = buf.at[slot]
    for c in range(2):
        acc = jnp.zeros((N_KV * P, 2 * CMP_HIDDEN), F32)
        for dp in range(HEAD_DIM // 2):
            rows = []
            for g in range(N_KV):
                halves = []
                for dd in range(2):
                    r = g * 2 * HEAD_DIM + c * HEAD_DIM + 2 * dp + dd
                    pr = c * HEAD_DIM + 2 * dp + dd
                    halves.append(bref[r] + pet_ref[pr:pr + 1, :])
                rows.append(jnp.concatenate(halves, axis=1))
            lhs = jnp.concatenate(rows, axis=0).astype(BF16)
            acc = acc + jnp.dot(lhs, w1_ref[c, dp], preferred_element_type=F32)
        hid = _gelu_tanh(acc).astype(BF16)
        oc = jnp.dot(hid, w2_ref[c], preferred_element_type=F32).astype(BF16)
        for g in range(N_KV):
            out_ref[g, c] = oc[g * P:(g + 1) * P]


def _compress(ids, src, pet, w1bd, w2bd, *, P):
    n_total = ids.shape[0]
    n_steps = n_total // P
    grid_spec = pltpu.PrefetchScalarGridSpec(
        num_scalar_prefetch=1,
        grid=(n_steps,),
        in_specs=[
            pl.BlockSpec(memory_space=pl.ANY),
            pl.BlockSpec(pet.shape, lambda i, ids: (0, 0)),
            pl.BlockSpec(w1bd.shape, lambda i, ids: (0, 0, 0, 0), pipeline_mode=pl.Buffered(1)),
            pl.BlockSpec(w2bd.shape, lambda i, ids: (0, 0, 0)),
        ],
        out_specs=pl.BlockSpec((N_KV, 2, P, PAGE_SIZE), lambda i, ids: (0, 0, i, 0)),
        scratch_shapes=[
            pltpu.VMEM((2, KV_WIDTH, P, PAGE_SIZE), F32),
            pltpu.SemaphoreType.DMA((2,)),
        ],
    )
    return pl.pallas_call(
        functools.partial(_compress_kernel, P=P, n_steps=n_steps),
        grid_spec=grid_spec,
        out_shape=jax.ShapeDtypeStruct((N_KV, 2, n_total, PAGE_SIZE), BF16),
        compiler_params=_cparams(("arbitrary",), V7X_VMEM_LIMIT),
        name="compress",
    )(ids, src, pet, w1bd, w2bd)


def _compress_weights(cmp_pe, cmp_w1, cmp_w2):
    eye = jnp.eye(BLOCKS_PER_PAGE, dtype=F32)
    pet = jnp.tile(jnp.transpose(cmp_pe, (1, 2, 0)).reshape(2 * HEAD_DIM, BLK), (1, BLOCKS_PER_PAGE))
    w1 = cmp_w1.reshape(2, BLK, HEAD_DIM // 2, 2, CMP_HIDDEN)
    w1bd = jnp.einsum('cluzh,jk->cuzjlkh', w1, eye).reshape(
        2, HEAD_DIM // 2, 2 * PAGE_SIZE, BLOCKS_PER_PAGE * CMP_HIDDEN).astype(BF16)
    w2bd = jnp.einsum('chd,jk->cjhkd', cmp_w2, eye).reshape(
        2, BLOCKS_PER_PAGE * CMP_HIDDEN, BLOCKS_PER_PAGE * HEAD_DIM).astype(BF16)
    return pet, w1bd, w2bd


def _rank_select_t(score, nblk, n_select):
    tq, W = score.shape
    x = score.T
    groups = [x[8 * k:8 * k + 8] for k in range(nblk // 8)]
    ranks = [jnp.zeros((8, tq), jnp.int32) for _ in groups]
    row = lax.broadcasted_iota(jnp.int32, (8, tq), 0)
    for i in range(nblk):
        xi = groups[i // 8][i % 8:i % 8 + 1, :]
        for k, xg in enumerate(groups):
            if 8 * k + 7 < i:
                ahead = xi > xg
            elif 8 * k > i:
                ahead = xi >= xg
            else:
                ahead = (xi > xg) | ((xi == xg) & (row > i - 8 * k))
            ranks[k] = ranks[k] + jnp.where(ahead, 1, 0)
    sel = [jnp.where(r < n_select, 1.0, 0.0) for r in ranks]
    if nblk < W:
        sel.append(jnp.zeros((W - nblk, tq), F32))
    return jnp.concatenate(sel, axis=0).T


def _key_consts(T, npg):
    assert T <= 256 * BLK and npg * BLOCKS_PER_PAGE <= HEAD_DIM
    k = jnp.arange(T, dtype=jnp.int32)[None, :]
    r = jnp.arange(HEAD_DIM, dtype=jnp.int32)[:, None]
    pos = jnp.where(r == 0, k // BLK, jnp.where(r == 1, k % BLK, 0)).astype(F32)
    en = jnp.where(k // BLK == r, NEG, 0.0)
    return jnp.concatenate([pos, en, jnp.zeros((HEAD_DIM, T), F32)], axis=0).astype(BF16)


M_INIT = -3e38


def _p_attn_kernel(q_ref, gate_ref, comp_ref, ks_ref, kw_ref, kc_ref, o_ref,
                   qx, m_ref, acc_ref, out_ref, flag_v, flag_s, flag_sem, *, tq, tk, npg, row_chunk):
    qi = pl.program_id(1)
    q0 = qi * tq
    pos_t = q0 + lax.broadcasted_iota(jnp.int32, (tq, 1), 0)
    nblk = npg * BLOCKS_PER_PAGE
    W = 128
    R = HPG * tq
    lane_h = lax.broadcasted_iota(jnp.int32, (1, HEAD_DIM), 1)

    for hh in range(N_HEADS):
        g, h = divmod(hh, HPG)
        rows = pl.ds(h * tq, tq)
        qx[g, rows, 0:HEAD_DIM] = q_ref[0, :, hh * HEAD_DIM:(hh + 1) * HEAD_DIM]
        posc = jnp.where(lane_h == 0, _slope(hh) * BLK, jnp.where(lane_h == 1, _slope(hh), 0.0))
        qx[g, rows, HEAD_DIM:2 * HEAD_DIM] = jnp.broadcast_to(posc, (tq, HEAD_DIM)).astype(BF16)
        qx[g, rows, 3 * HEAD_DIM:4 * HEAD_DIM] = jnp.zeros((tq, HEAD_DIM), BF16)

    blk_n = lax.broadcasted_iota(jnp.int32, (1, W), 1)
    first = blk_n < nblk
    end_pos = (blk_n + 1) * BLK - 1
    dist_c = (pos_t - end_pos).astype(F32)
    mask_c = (dist_c >= 0) & first
    cur = pos_t // BLK
    row_w = lax.broadcasted_iota(jnp.int32, (W, W), 0)
    col_w = lax.broadcasted_iota(jnp.int32, (W, W), 1)
    perm = jnp.where((col_w == (row_w % 2) * npg + row_w // 2) & (row_w < nblk), 1.0, 0.0).astype(BF16)

    used = []
    for g in range(N_KV):
        ck = comp_ref[g, 0].astype(F32)
        cv = comp_ref[g, 1].astype(F32)
        pad = [jnp.zeros((W - nblk, HEAD_DIM), F32)] if nblk < W else []
        ckp = jnp.concatenate([ck[:, :HEAD_DIM], ck[:, HEAD_DIM:]] + pad, axis=0).astype(BF16)
        cvp = jnp.concatenate([cv[:, :HEAD_DIM], cv[:, HEAD_DIM:]] + pad, axis=0).astype(BF16)
        ckp = jnp.dot(perm, ckp, preferred_element_type=F32).astype(BF16)
        cvp = jnp.dot(perm, cvp, preferred_element_type=F32).astype(BF16)
        imp = jnp.zeros((tq, W), F32)
        for h in range(HPG):
            hh = g * HPG + h
            qh = qx[g, pl.ds(h * tq, tq), 0:HEAD_DIM]
            s = lax.dot_general(qh, ckp, NT_DIMS, preferred_element_type=F32)
            s = jnp.where(mask_c, s - _slope(hh) * dist_c, NEG)
            mx = jnp.max(s, axis=1, keepdims=True)
            e = jnp.where(mask_c, jnp.exp(s - mx), 0.0)
            p = e / jnp.maximum(jnp.sum(e, axis=1, keepdims=True), 1e-30)
            imp = imp + p
            oc = jnp.dot(p.astype(BF16), cvp, preferred_element_type=F32)
            gcol = gate_ref[0, :, hh:hh + 1]
            out_ref[:, hh * HEAD_DIM:(hh + 1) * HEAD_DIM] = gcol * oc
        forced = (blk_n == 0) | (blk_n == cur) | (blk_n == cur - 1)
        score = jnp.where(forced, FORCE_SCORE, imp)
        score = jnp.where(blk_n > cur, -1.0, score)
        sel = _rank_select_t(score, nblk, N_SELECT)
        picked = (sel > 0.5) & (blk_n <= cur)
        notsel = jnp.where(picked, 0.0, 1.0).astype(BF16)
        for h in range(HPG):
            qx[g, pl.ds(h * tq, tq), 2 * HEAD_DIM:3 * HEAD_DIM] = notsel[:, :HEAD_DIM]
        used.append(jnp.max(jnp.where(picked, 1, 0), axis=0, keepdims=True))

    flag_v[...] = jnp.concatenate(used + [jnp.zeros((8 - N_KV, W), jnp.int32)], axis=0)
    flag_copy = pltpu.make_async_copy(flag_v, flag_s, flag_sem.at[0])
    flag_copy.start()

    ones_rows = jnp.where(lax.broadcasted_iota(jnp.int32, (HEAD_DIM, 1), 0) == 0, 1.0, 0.0)
    pos_rows = q0 + lax.broadcasted_iota(jnp.int32, (R, 1), 0) % tq

    def attend_tile(kv_ref, k0, width, kc_rows, mode, groups=tuple(range(N_KV))):
        kpos = k0 + lax.broadcasted_iota(jnp.int32, (1, width), 1)
        kc_tile = kc_ref[0:kc_rows, pl.ds(k0, width)]
        ones_tile = jnp.broadcast_to(ones_rows, (HEAD_DIM, width)).astype(BF16)
        if mode == "causal":
            keep = kpos <= pos_rows
        elif mode == "recent":
            keep = kpos > pos_rows - WINDOW
        for g in groups:
            kt = kv_ref[0, 0, g * 2 * HEAD_DIM:g * 2 * HEAD_DIM + HEAD_DIM, pl.ds(k0, width)]
            vt = kv_ref[0, 0, g * 2 * HEAD_DIM + HEAD_DIM:(g + 1) * 2 * HEAD_DIM, pl.ds(k0, width)]
            kext = jnp.concatenate([kt, kc_tile], axis=0)
            vext = jnp.concatenate([vt, ones_tile], axis=0)
            for c in range(R // row_chunk):
                rc = pl.ds(c * row_chunk, row_chunk)
                s = jnp.dot(qx[g, rc, 0:HEAD_DIM + kc_rows], kext, preferred_element_type=F32)
                if mode != "full":
                    s = jnp.where(keep[c * row_chunk:(c + 1) * row_chunk], s, NEG)
                m_prev = m_ref[g, rc, :]
                m_next = jnp.maximum(m_prev, jnp.max(s, axis=1, keepdims=True))
                alpha = jnp.exp(m_prev - m_next)
                m_ref[g, rc, :] = m_next
                p = jnp.exp(s - jnp.concatenate([m_next] * (width // W), axis=1)).astype(BF16)
                pv = lax.dot_general(p, vext, NT_DIMS, preferred_element_type=F32)
                acc_ref[g, rc, :] = alpha * acc_ref[g, rc, :] + pv

    def reset():
        for g in range(N_KV):
            m_ref[g] = jnp.full((R, W), M_INIT, F32)
            acc_ref[g] = jnp.zeros((R, W), F32)

    def finish(gate_base):
        for g in range(N_KV):
            for h in range(HPG):
                hh = g * HPG + h
                a = acc_ref[g, pl.ds(h * tq, tq), :]
                o = a[:, :HEAD_DIM] / jnp.maximum(a[:, HEAD_DIM:HEAD_DIM + 1], 1e-30)
                gcol = gate_ref[0, :, gate_base + hh:gate_base + hh + 1]
                out_ref[:, hh * HEAD_DIM:(hh + 1) * HEAD_DIM] += gcol * o

    reset()
    wt = WINDOW // tq

    @pl.when(qi >= wt)
    def _():
        attend_tile(kw_ref, pl.multiple_of((qi - wt) * tq, tq), WINDOW, HEAD_DIM, "recent")

    @pl.when(qi < wt)
    def _():
        def win_body(j, carry):
            attend_tile(kw_ref, pl.multiple_of(j * tq, tq), tq, HEAD_DIM, "full")
            return carry

        lax.fori_loop(0, qi, win_body, 0)

    attend_tile(kw_ref, pl.multiple_of(q0, tq), tq, HEAD_DIM, "causal")
    finish(2 * N_HEADS)

    reset()
    flag_copy.wait()
    n_sel = (q0 + tq + tk - 1) // tk
    bpt = tk // BLK

    def sel_body(j, carry):
        for g in range(N_KV):
            hit = flag_s[g, j * bpt]
            for u in range(1, bpt):
                hit = hit | flag_s[g, j * bpt + u]

            @pl.when(hit != 0)
            def _():
                attend_tile(ks_ref, pl.multiple_of(j * tk, tk), tk, 3 * HEAD_DIM, "full", groups=(g,))

        return carry

    lax.fori_loop(0, n_sel - 1, sel_body, 0)
    attend_tile(ks_ref, pl.multiple_of((n_sel - 1) * tk, tk), tk, 3 * HEAD_DIM, "causal")
    finish(N_HEADS)
    o_ref[0] = out_ref[...].astype(BF16)


def _p_attn(q, gates, comp, kvtb, kconst, *, tq, tk):
    B, T, _ = q.shape
    npg = T // PAGE_SIZE
    assert WINDOW % tq == 0 and tk % tq == 0
    return pl.pallas_call(
        functools.partial(_p_attn_kernel, tq=tq, tk=tk, npg=npg, row_chunk=min(512, HPG * tq)),
        grid=(B, T // tq),
        in_specs=[
            pl.BlockSpec((1, tq, ATTN_WIDTH), lambda b, i: (b, i, 0)),
            pl.BlockSpec((1, tq, N_GATE_PAD), lambda b, i: (b, i, 0)),
            pl.BlockSpec((N_KV, 2, npg, PAGE_SIZE), lambda b, i: (0, 0, b, 0)),
            pl.BlockSpec((1, 1, KV_WIDTH, T), lambda b, i: (0, b, 0, 0)),
            pl.BlockSpec((1, 1, KV_WIDTH, T), lambda b, i: (1, b, 0, 0)),
            pl.BlockSpec(kconst.shape, lambda b, i: (0, 0)),
        ],
        out_specs=pl.BlockSpec((1, tq, ATTN_WIDTH), lambda b, i: (b, i, 0)),
        out_shape=jax.ShapeDtypeStruct((B, T, ATTN_WIDTH), BF16),
        scratch_shapes=[
            pltpu.VMEM((N_KV, HPG * tq, 4 * HEAD_DIM), BF16),
            pltpu.VMEM((N_KV, HPG * tq, 128), F32),
            pltpu.VMEM((N_KV, HPG * tq, 128), F32),
            pltpu.VMEM((tq, ATTN_WIDTH), F32),
            pltpu.VMEM((8, 128), jnp.int32),
            pltpu.SMEM((8, 128), jnp.int32),
            pltpu.SemaphoreType.DMA((1,)),
        ],
        compiler_params=_cparams(("arbitrary", "arbitrary"), V7X_VMEM_LIMIT),
        name="p_attn",
    )(q, gates, comp, kvtb, kvtb, kconst)


def _stack_heads(q_ref, g, ts):
    parts = [q_ref[0, :, (g * HPG + h) * HEAD_DIM:(g * HPG + h + 1) * HEAD_DIM].astype(F32) for h in range(HPG)]
    return jnp.concatenate(parts, axis=0).astype(BF16)


def _row_consts(g, ts):
    R = HPG * ts
    row = lax.broadcasted_iota(jnp.int32, (R, 1), 0)
    t_row = row % ts
    h_row = row // ts
    slope = jnp.zeros((R, 1), F32)
    for h in range(HPG):
        slope = jnp.where(h_row == h, _slope(g * HPG + h), slope)
    return t_row, slope


def _s_select_kernel(q_ref, gate_ref, comp_ref, cnew_ref, oc_ref, bits_ref, out_s, *, past, ts):
    W = 128
    nb_past = past // BLK
    lane = lax.broadcasted_iota(jnp.int32, (1, W), 1)
    idx_fns = [lambda c: 2 * c, lambda c: 2 * c + 1, lambda c: jnp.where(c == 0, nb_past, (1 << 20) + c)]
    idx_tiles = [f(lane) for f in idx_fns]
    valid_n = lane == 0
    tok = lax.broadcasted_iota(jnp.int32, (ts, 1), 0)
    cur_t = (past + tok) // BLK
    for g in range(N_KV):
        qg = _stack_heads(q_ref, g, ts)
        t_row, slope = _row_consts(g, ts)
        pos = past + t_row
        ck = comp_ref[g, 0]
        cv = comp_ref[g, 1]
        cn = cnew_ref[g, 0, 0].astype(F32)[:, :HEAD_DIM]
        vn = cnew_ref[g, 1, 0].astype(F32)[:, :HEAD_DIM]
        s_t = [lax.dot_general(qg, ck[:, :HEAD_DIM], NT_DIMS, preferred_element_type=F32),
               lax.dot_general(qg, ck[:, HEAD_DIM:], NT_DIMS, preferred_element_type=F32),
               jnp.broadcast_to(jnp.sum(qg.astype(F32) * cn, axis=1, keepdims=True), (HPG * ts, W))]
        masks, es = [], []
        for k in range(3):
            end_pos = (idx_tiles[k] + 1) * BLK - 1
            dist = (pos - end_pos).astype(F32)
            mk = dist >= 0
            if k == 2:
                mk = mk & valid_n
            masks.append(mk)
            s_t[k] = jnp.where(mk, s_t[k] - slope * dist, NEG)
        mx = jnp.maximum(jnp.maximum(jnp.max(s_t[0], axis=1, keepdims=True),
                                     jnp.max(s_t[1], axis=1, keepdims=True)),
                         jnp.max(s_t[2], axis=1, keepdims=True))
        for k in range(3):
            es.append(jnp.where(masks[k], jnp.exp(s_t[k] - mx), 0.0))
        den = (jnp.sum(es[0], axis=1, keepdims=True) + jnp.sum(es[1], axis=1, keepdims=True)
               + jnp.sum(es[2], axis=1, keepdims=True))
        inv = 1.0 / jnp.maximum(den, 1e-30)
        ps = [e * inv for e in es]
        o_c = (jnp.dot(ps[0].astype(BF16), cv[:, :HEAD_DIM], preferred_element_type=F32)
               + jnp.dot(ps[1].astype(BF16), cv[:, HEAD_DIM:], preferred_element_type=F32)
               + ps[2][:, 0:1].astype(BF16).astype(F32) * vn)
        for h in range(HPG):
            hh = g * HPG + h
            gcol = gate_ref[0, :, hh:hh + 1]
            out_s[:, hh * HEAD_DIM:(hh + 1) * HEAD_DIM] = gcol * o_c[h * ts:(h + 1) * ts]
        scores = []
        for k in range(3):
            imp = ps[k][0:ts]
            for h in range(1, HPG):
                imp = imp + ps[k][h * ts:(h + 1) * ts]
            idx = idx_tiles[k]
            forced = (idx == 0) | (idx == cur_t) | (idx == cur_t - 1)
            sc = jnp.where(forced, FORCE_SCORE, imp)
            sc = jnp.where(idx > cur_t, -1.0, sc)
            if k == 2:
                sc = jnp.where(valid_n, sc, -2.0)
            scores.append(sc)
        s_new = scores[2][:, 0:1]
        parts = [[jnp.where(s_new > scores[ka], 1, 0), jnp.zeros((ts, W), jnp.int32)] for ka in range(2)]
        for kb in range(2):
            for r in range(W):
                y = pltpu.roll(scores[kb], r, 1) if r else scores[kb]
                yi = idx_fns[kb]((lane - r) & (W - 1))
                for ka in range(2):
                    if ka == kb and r == 0:
                        continue
                    ahead = (y > scores[ka]) | ((y == scores[ka]) & (yi < idx_tiles[ka]))
                    parts[ka][r % 2] = parts[ka][r % 2] + jnp.where(ahead, 1, 0)
        sel = [(parts[k][0] + parts[k][1]) < N_SELECT for k in range(2)]
        rank_new = (jnp.sum(jnp.where(scores[0] >= s_new, 1, 0), axis=1, keepdims=True)
                    + jnp.sum(jnp.where(scores[1] >= s_new, 1, 0), axis=1, keepdims=True))
        wt = jnp.left_shift(1, 2 * tok)
        page_bits = jnp.sum(jnp.where(sel[0], wt, 0) + jnp.where(sel[1], 2 * wt, 0), axis=0, keepdims=True)
        new_bits = jnp.sum(jnp.where((rank_new < N_SELECT) & valid_n, jnp.left_shift(1, tok), 0),
                           axis=0, keepdims=True)
        bits_ref[0, g, 0:1, :] = page_bits
        bits_ref[0, g, 1:2, :] = new_bits
    oc_ref[0] = out_s[...]


def _s_select(q, gates, comp, cnew, *, past):
    Bs, ts, _ = q.shape
    npg = past // PAGE_SIZE
    assert npg == 128, "one lane tile of pages per sequence"
    return pl.pallas_call(
        functools.partial(_s_select_kernel, past=past, ts=ts),
        grid=(Bs,),
        in_specs=[
            pl.BlockSpec((1, ts, ATTN_WIDTH), lambda b: (b, 0, 0)),
            pl.BlockSpec((1, ts, N_GATE_PAD), lambda b: (b, 0, 0)),
            pl.BlockSpec((N_KV, 2, npg, PAGE_SIZE), lambda b: (0, 0, b, 0)),
            pl.BlockSpec((N_KV, 2, 1, 1, PAGE_SIZE), lambda b: (0, 0, b, 0, 0)),
        ],
        out_specs=[
            pl.BlockSpec((1, ts, ATTN_WIDTH), lambda b: (b, 0, 0)),
            pl.BlockSpec((1, N_KV, 2, 128), lambda b: (b, 0, 0, 0)),
        ],
        out_shape=[
            jax.ShapeDtypeStruct((Bs, ts, ATTN_WIDTH), F32),
            jax.ShapeDtypeStruct((Bs, N_KV, 2, 128), jnp.int32),
        ],
        scratch_shapes=[pltpu.VMEM((ts, ATTN_WIDTH), F32)],
        compiler_params=_cparams(("arbitrary",), V7X_VMEM_LIMIT),
        name="s_select",
    )(q, gates, comp, cnew)


def _s_attn_kernel(pt_ref, bits_ref, q_ref, gate_ref, oc_ref, win_ref, knew_ref, slc_ref,
                   o_ref, wout_ref, buf, sem, plist, cnts, out_s, *, past, ts, npg, chunk):
    b = pl.program_id(0)
    g = pl.program_id(1)
    W = 128
    R = HPG * ts
    wk = win_ref.shape[2]
    lin = b * N_KV + g
    n_lin = pl.num_programs(0) * N_KV
    base = lin * 2 * W
    slot = lin % 2

    def page_copy(page, gg, sl, idx):
        return pltpu.make_async_copy(slc_ref.at[page, pl.ds(gg * 2 * HEAD_DIM, 2 * HEAD_DIM)],
                                     buf.at[sl, idx], sem.at[sl])

    def issue_all(ln, sl):
        bb = ln // N_KV
        gg = ln % N_KV

        def issue(p, cnt):
            hit = bits_ref[ln * 2 * W + p] != 0

            @pl.when(hit)
            def _():
                page_copy(pt_ref[bb * npg + p], gg, sl, cnt).start()
                plist[sl * npg + cnt] = p

            return cnt + hit.astype(jnp.int32)

        cnts[sl] = lax.fori_loop(0, npg, issue, 0, unroll=8)

    @pl.when(lin == 0)
    def _():
        buf[...] = jnp.zeros(buf.shape, F32)
        for k in range(2 * npg):
            plist[k] = 0
        issue_all(lin, slot)

    @pl.when(lin + 1 < n_lin)
    def _():
        issue_all(lin + 1, 1 - slot)

    cnt = cnts[slot]

    q_all = q_ref[0].astype(F32)
    qg = jnp.concatenate([q_all[:, h * HEAD_DIM:(h + 1) * HEAD_DIM] for h in range(HPG)], axis=0).astype(BF16)
    row = lax.broadcasted_iota(jnp.int32, (R, 1), 0)
    t_row = row % ts
    head_row = row // ts + g * HPG
    slope = jnp.zeros((R, 1), F32)
    for hh in range(N_HEADS):
        slope = jnp.where(head_row == hh, _slope(hh), slope)
    lane = lax.broadcasted_iota(jnp.int32, (1, W), 1)

    kw = win_ref[0, 0:HEAD_DIM, :].astype(BF16)
    vw = win_ref[0, HEAD_DIM:2 * HEAD_DIM, :].astype(BF16)
    kn = knew_ref[2, 0, 0:HEAD_DIM, :].astype(BF16)
    vn = knew_ref[2, 0, HEAD_DIM:2 * HEAD_DIM, :].astype(BF16)
    i_st = lax.broadcasted_iota(jnp.int32, (1, wk), 1)
    d_st = t_row + wk - i_st
    m_st = (d_st >= 0) & (d_st < WINDOW)
    d_nw = t_row - lane
    m_nw = (d_nw >= 0) & (d_nw < WINDOW) & (lane < ts)
    s_st = jnp.where(m_st, jnp.dot(qg, kw, preferred_element_type=F32) - slope * d_st.astype(F32), NEG)
    s_nw = jnp.where(m_nw, jnp.dot(qg, kn, preferred_element_type=F32) - slope * d_nw.astype(F32), NEG)
    mx = jnp.maximum(jnp.max(s_st, axis=1, keepdims=True), jnp.max(s_nw, axis=1, keepdims=True))
    e_st = jnp.where(m_st, jnp.exp(s_st - mx), 0.0)
    e_nw = jnp.where(m_nw, jnp.exp(s_nw - mx), 0.0)
    den = jnp.sum(e_st, axis=1, keepdims=True) + jnp.sum(e_nw, axis=1, keepdims=True)
    inv = 1.0 / jnp.maximum(den, 1e-30)
    o_w = (lax.dot_general((e_st * inv).astype(BF16), vw, NT_DIMS, preferred_element_type=F32)
           + lax.dot_general((e_nw * inv).astype(BF16), vn, NT_DIMS, preferred_element_type=F32))

    last = wk - W
    shifted = pltpu.roll(win_ref[0], wk - ts, 1)
    newr = pltpu.roll(knew_ref[2, 0], W - ts, 1)
    wout_ref[0, :, 0:last] = shifted[:, 0:last]
    wout_ref[0, :, last:wk] = jnp.where(lane >= W - ts, newr, shifted[:, last:wk])

    bits_new = bits_ref[base + W]
    ksn = knew_ref[1, 0, 0:HEAD_DIM, :].astype(BF16)
    vsn = knew_ref[1, 0, HEAD_DIM:2 * HEAD_DIM, :].astype(BF16)
    d_sn = t_row - lane
    m_sn = (d_sn >= 0) & (lane < ts) & ((jnp.right_shift(bits_new, t_row) & 1) == 1)
    s_sn = jnp.where(m_sn, jnp.dot(qg, ksn, preferred_element_type=F32) - slope * d_sn.astype(F32), NEG)
    m0 = jnp.max(s_sn, axis=1, keepdims=True)
    e0 = jnp.where(m_sn, jnp.exp(s_sn - m0), 0.0)
    l0 = jnp.sum(e0, axis=1, keepdims=True)
    a0 = lax.dot_general(e0.astype(BF16), vsn, NT_DIMS, preferred_element_type=F32)

    def wait_one(s, c):
        page_copy(0, 0, slot, s).wait()
        return c

    lax.fori_loop(0, cnt, wait_one, 0)
    sh = 2 * t_row + (lane >= BLK).astype(jnp.int32)
    sh_c = jnp.concatenate([sh] * chunk, axis=1)
    bslot = buf.at[slot]

    def chunk_step(c, carry):
        m, l, acc = carry
        kts, vts, kpos, bitv = [], [], [], []
        for u in range(chunk):
            s = c * chunk + u
            ok = s < cnt
            p = jnp.where(ok, plist[slot * npg + s], 0)
            bits = jnp.where(ok, bits_ref[base + p], 0)
            kts.append(bslot[s, 0:HEAD_DIM, :])
            vts.append(bslot[s, HEAD_DIM:2 * HEAD_DIM, :])
            kpos.append(p * PAGE_SIZE + lane)
            bitv.append(jnp.zeros((1, W), jnp.int32) + bits)
        kt = jnp.concatenate(kts, axis=1).astype(BF16)
        vt = jnp.concatenate(vts, axis=1).astype(BF16)
        dist = past + t_row - jnp.concatenate(kpos, axis=1)
        mask = ((jnp.right_shift(jnp.concatenate(bitv, axis=1), sh_c) & 1) == 1) & (dist >= 0)
        sc = jnp.where(mask, jnp.dot(qg, kt, preferred_element_type=F32) - slope * dist.astype(F32), NEG)
        m_new = jnp.maximum(m, jnp.max(sc, axis=1, keepdims=True))
        alpha = jnp.exp(m - m_new)
        e = jnp.where(mask, jnp.exp(sc - m_new), 0.0)
        l = alpha * l + jnp.sum(e, axis=1, keepdims=True)
        acc = alpha * acc + lax.dot_general(e.astype(BF16), vt, NT_DIMS, preferred_element_type=F32)
        return m_new, l, acc

    m, l, acc = lax.fori_loop(0, (cnt + chunk - 1) // chunk, chunk_step, (m0, l0, a0))
    o_s = acc / jnp.maximum(l, 1e-30)

    g_vec = jnp.zeros((ts, 1), jnp.int32) + g
    for h in range(HPG):
        rows = slice(h * ts, (h + 1) * ts)
        g_s = jnp.zeros((ts, 1), F32)
        g_w = jnp.zeros((ts, 1), F32)
        for gg in range(N_KV):
            hh = gg * HPG + h
            g_s = jnp.where(g_vec == gg, gate_ref[0, :, N_HEADS + hh:N_HEADS + hh + 1], g_s)
            g_w = jnp.where(g_vec == gg, gate_ref[0, :, 2 * N_HEADS + hh:2 * N_HEADS + hh + 1], g_w)
        out_s[:, h * HEAD_DIM:(h + 1) * HEAD_DIM] = g_s * o_s[rows] + g_w * o_w[rows]
    o_ref[0] = oc_ref[0] + out_s[...]


def _s_attn(page_table, bits, q, gates, oc, win_t, knew, slc_t, *, past):
    Bs, ts, _ = q.shape
    npg = past // PAGE_SIZE
    wk = win_t.shape[2]
    gw = HPG * HEAD_DIM
    grid_spec = pltpu.PrefetchScalarGridSpec(
        num_scalar_prefetch=2,
        grid=(Bs, N_KV),
        in_specs=[
            pl.BlockSpec((1, ts, gw), lambda b, g, *_: (b, 0, g)),
            pl.BlockSpec((1, ts, N_GATE_PAD), lambda b, g, *_: (b, 0, 0)),
            pl.BlockSpec((1, ts, gw), lambda b, g, *_: (b, 0, g)),
            pl.BlockSpec((1, 2 * HEAD_DIM, wk), lambda b, g, *_: (b, g, 0)),
            pl.BlockSpec((3, 1, 2 * HEAD_DIM, PAGE_SIZE), lambda b, g, *_: (0, b, g, 0)),
            pl.BlockSpec(memory_space=pl.ANY),
        ],
        out_specs=[
            pl.BlockSpec((1, ts, gw), lambda b, g, *_: (b, 0, g)),
            pl.BlockSpec((1, 2 * HEAD_DIM, wk), lambda b, g, *_: (b, g, 0)),
        ],
        scratch_shapes=[
            pltpu.VMEM((2, npg, 2 * HEAD_DIM, PAGE_SIZE), F32),
            pltpu.SemaphoreType.DMA((2,)),
            pltpu.SMEM((2 * npg,), jnp.int32),
            pltpu.SMEM((2,), jnp.int32),
            pltpu.VMEM((ts, gw), F32),
        ],
    )
    chunk = 8
    assert npg % chunk == 0
    return pl.pallas_call(
        functools.partial(_s_attn_kernel, past=past, ts=ts, npg=npg, chunk=chunk),
        grid_spec=grid_spec,
        out_shape=[
            jax.ShapeDtypeStruct((Bs, ts, ATTN_WIDTH), F32),
            jax.ShapeDtypeStruct(win_t.shape, F32),
        ],
        compiler_params=_cparams(("arbitrary", "arbitrary"), V7X_VMEM_LIMIT),
        name="s_attn",
    )(page_table.reshape(-1), bits.reshape(-1), q, gates, oc, win_t, knew, slc_t)


def _pool_kernel(u_ref, halo_ref, wp_ref, ps_ref, o_ref, ext, *, nb, tp, tiles_per_seq, pos_base):
    i = pl.program_id(0)
    ext[:, 0:HALO, :] = halo_ref[...]
    ext[:, HALO:HALO + tp, :] = u_ref[...]
    tile = i * nb + lax.broadcasted_iota(jnp.int32, (nb, tp, 1), 0)
    pos = pos_base + (tile % tiles_per_seq) * tp + lax.broadcasted_iota(jnp.int32, (nb, tp, 1), 1)
    for gi, w in enumerate(POOL_WINDOWS):
        lanes = slice(gi * POOL_GROUP, (gi + 1) * POOL_GROUP)
        own = ext[:, HALO:HALO + tp, lanes]
        tot = own
        for k in range(1, w):
            tot = tot + ext[:, HALO - k:HALO - k + tp, lanes]
        cnt = jnp.minimum(w, pos + 1).astype(F32)
        d = (tot / cnt - own).astype(BF16).reshape(nb * tp, POOL_GROUP)
        y = jnp.dot(d, wp_ref[gi], preferred_element_type=F32) * ps_ref[:, lanes]
        o_ref[:, :, lanes] = y.reshape(nb, tp, POOL_GROUP).astype(o_ref.dtype)


def _pool(u, halo, w_pool, pool_scale, *, nb, tiles_per_seq, pos_base, out_dtype):
    n_tiles, tp, _ = u.shape
    return pl.pallas_call(
        functools.partial(_pool_kernel, nb=nb, tp=tp, tiles_per_seq=tiles_per_seq, pos_base=pos_base),
        grid=(n_tiles // nb,),
        in_specs=[
            pl.BlockSpec((nb, tp, POOL_WIDTH), lambda i: (i, 0, 0)),
            pl.BlockSpec((nb, HALO, POOL_WIDTH), lambda i: (i, 0, 0)),
            pl.BlockSpec(w_pool.shape, lambda i: (0, 0, 0)),
            pl.BlockSpec((1, POOL_WIDTH), lambda i: (0, 0)),
        ],
        out_specs=pl.BlockSpec((nb, tp, POOL_WIDTH), lambda i: (i, 0, 0)),
        out_shape=jax.ShapeDtypeStruct((n_tiles, tp, POOL_WIDTH), out_dtype),
        scratch_shapes=[pltpu.VMEM((nb, HALO + tp, POOL_WIDTH), F32)],
        compiler_params=_cparams(("arbitrary",), V7X_VMEM_LIMIT),
        name="pool",
    )(u, halo, w_pool, pool_scale)


def _rms(x, g):
    return (x * lax.rsqrt(jnp.mean(x * x, axis=-1, keepdims=True) + NORM_EPS)) * g


def _mlp_kernel(x_ref, o_ref, p_ref, wo_ref, gm_ref, wu_ref, wd_ref, gf_ref, y_ref, *, ff_chunk):
    mix = jnp.concatenate([o_ref[...], p_ref[...]], axis=1)
    h = x_ref[...] + jnp.dot(mix, wo_ref[...], preferred_element_type=F32)
    hn = _rms(h, gm_ref[...]).astype(BF16)
    acc = jnp.zeros(h.shape, F32)
    for c in range(D_FF // ff_chunk):
        a = jnp.dot(hn, wu_ref[:, c * ff_chunk:(c + 1) * ff_chunk], preferred_element_type=F32)
        a = jnp.square(jnp.maximum(a, 0.0)).astype(BF16)
        acc = acc + jnp.dot(a, wd_ref[c * ff_chunk:(c + 1) * ff_chunk, :], preferred_element_type=F32)
    y_ref[...] = _rms(h + acc, gf_ref[...])


def _mlp(x, o, p, w_out, g_mlp, w_up, w_down, g_final, *, tm):
    N = x.shape[0]
    const = lambda i: (0, 0)
    resident = lambda a: pl.BlockSpec(a.shape, const, pipeline_mode=pl.Buffered(1))
    return pl.pallas_call(
        functools.partial(_mlp_kernel, ff_chunk=1024),
        grid=(N // tm,),
        in_specs=[
            pl.BlockSpec((tm, D_MODEL), lambda i: (i, 0)),
            pl.BlockSpec((tm, ATTN_WIDTH), lambda i: (i, 0)),
            pl.BlockSpec((tm, POOL_WIDTH), lambda i: (i, 0)),
            resident(w_out), resident(g_mlp), resident(w_up), resident(w_down), resident(g_final),
        ],
        out_specs=pl.BlockSpec((tm, D_MODEL), lambda i: (i, 0)),
        out_shape=jax.ShapeDtypeStruct((N, D_MODEL), F32),
        compiler_params=_cparams(("arbitrary",), V7X_VMEM_LIMIT),
        name="mlp",
    )(x, o, p, w_out, g_mlp, w_up, w_down, g_final)


def _rows_from_kt(kt):
    B, _, S = kt.shape
    return jnp.transpose(kt.reshape(B, N_KV, 2, HEAD_DIM, S), (0, 4, 1, 2, 3))[None]


def _prep_weights(w_in, cmp_pe, cmp_w1, cmp_w2):
    w_t = jnp.transpose(w_in)
    a, b = ATTN_WIDTH, ATTN_WIDTH + 3 * KV_WIDTH
    wq, wkv = w_t[:a], w_t[a:b]
    wg, wu = w_t[b:b + 3 * N_HEADS], w_t[b + 3 * N_HEADS:]
    wg = jnp.pad(wg, ((0, N_GATE_PAD - 3 * N_HEADS), (0, 0)))
    wtok = jnp.concatenate([wq, wu, wg], axis=0).astype(BF16)
    wtok_kv = jnp.concatenate([wq, wu, wg, wkv], axis=0).astype(BF16)
    return wtok, wtok_kv, wkv.astype(BF16), _compress_weights(cmp_pe, cmp_w1, cmp_w2)


def _prompt_path(x, wts, g_attn, w_pool, pool_scale, w_out, g_mlp, w_up, w_down, g_final):
    wtok, _, wkv, (pet, w1bd, w2bd) = wts
    B, T, _ = x.shape
    tm = min(512, T)
    q, gates, u, kc_t, ks_t, kw_t, kvtb, pages = _proj(x, g_attn, wtok, wkv, tm=tm, emit_tok_kv=False,
                                                        emit_pages=True)
    n_pages = B * T // PAGE_SIZE
    comp = _compress(jnp.arange(n_pages, dtype=jnp.int32), pages, pet, w1bd, w2bd, P=min(64, n_pages))
    o = _p_attn(q, gates, comp, kvtb, _key_consts(T, T // PAGE_SIZE), tq=min(256, T), tk=min(512, T))
    nt = T // tm
    u4 = u.reshape(B, nt, tm, POOL_WIDTH)
    halo = jnp.concatenate([jnp.zeros((B, 1, HALO, POOL_WIDTH), F32), u4[:, :-1, tm - HALO:, :]], axis=1)
    pool = _pool(u.reshape(B * nt, tm, POOL_WIDTH), halo.reshape(B * nt, HALO, POOL_WIDTH),
                 w_pool, pool_scale, nb=1, tiles_per_seq=nt, pos_base=0, out_dtype=BF16)
    N = B * T
    y = _mlp(x.reshape(N, D_MODEL), o.reshape(N, ATTN_WIDTH), pool.reshape(N, POOL_WIDTH),
             w_out, g_mlp, w_up, w_down, g_final, tm=tm)
    wk = min(WINDOW, T)
    return (y.reshape(B, T, D_MODEL), _rows_from_kt(kc_t), _rows_from_kt(ks_t),
            _rows_from_kt(kw_t[:, :, T - wk:]), u[None, :, T - POOL_HIST:])


def _kt_view(rows):
    n, s = rows.shape[:2]
    return jnp.transpose(rows, (0, 2, 3, 4, 1)).reshape(n, KV_WIDTH, s)


def _sample_path(x, cache_cmp, cache_slc, state_win, state_pool, page_table, wts, g_attn,
                 w_pool, pool_scale, w_out, g_mlp, w_up, w_down, g_final):
    _, wtok_kv, wkv, (pet, w1bd, w2bd) = wts
    Bs, ts, _ = x.shape
    N = Bs * ts
    npg = page_table.shape[1]
    past = npg * PAGE_SIZE
    q, gates, u, kc_t, ks_t, kw_t, _, kv_tok = _proj(x.reshape(1, N, D_MODEL), g_attn, wtok_kv, wkv, tm=N,
                                                     emit_tok_kv=True, emit_pages=False)
    kvt = jnp.concatenate([kc_t, ks_t, kw_t], axis=0)
    knew = jnp.pad(jnp.transpose(kvt.reshape(3, KV_WIDTH, Bs, ts), (0, 2, 1, 3)),
                   ((0, 0), (0, 0), (0, 0), (0, PAGE_SIZE - ts)))
    comp_past = _compress(page_table.reshape(-1), _kt_view(cache_cmp), pet, w1bd, w2bd, P=min(128, Bs * npg))
    comp_new = _compress(jnp.arange(Bs, dtype=jnp.int32), knew[0], pet, w1bd, w2bd, P=Bs)
    qf = q.reshape(Bs, ts, ATTN_WIDTH).astype(F32)
    gts = gates.reshape(Bs, ts, N_GATE_PAD)
    oc, bits = _s_select(qf, gts, comp_past, comp_new.astype(F32).reshape(N_KV, 2, Bs, 1, PAGE_SIZE), past=past)
    o, win_new = _s_attn(page_table, bits, qf, gts, oc, _kt_view(state_win), knew, _kt_view(cache_slc), past=past)
    u3 = u.reshape(Bs, ts, POOL_WIDTH)
    halo = jnp.pad(state_pool, ((0, 0), (HALO - POOL_HIST, 0), (0, 0)))
    pool = _pool(u3, halo, w_pool, pool_scale, nb=Bs, tiles_per_seq=1, pos_base=past, out_dtype=F32)
    y = _mlp(x.reshape(N, D_MODEL), o.reshape(N, ATTN_WIDTH).astype(BF16), pool.reshape(N, POOL_WIDTH).astype(BF16),
             w_out, g_mlp, w_up, w_down, g_final, tm=N)
    kv_rows = kv_tok.reshape(Bs, ts, 3, N_KV, 2, HEAD_DIM)
    s_pool = jnp.concatenate([state_pool, u3], axis=1)[None, :, ts:]
    return (y.reshape(Bs, ts, D_MODEL), kv_rows[None, :, :, 0], kv_rows[None, :, :, 1],
            _rows_from_kt(win_new), s_pool)


def kernel(x_prompt, x_sample, cache_cmp, cache_slc, state_win, state_pool, page_table, g_attn, w_in,
           cmp_pe, cmp_w1, cmp_w2, w_pool, pool_scale, w_out, g_mlp, w_up, w_down, g_final):
    assert g_attn.shape[0] == 1, "single-layer trunk"
    wts = _prep_weights(w_in[0], cmp_pe[0], cmp_w1[0], cmp_w2[0])
    mlp_w = (w_out[0].astype(BF16), g_mlp, w_up[0].astype(BF16), w_down[0].astype(BF16), g_final[None])
    y_p, p_cmp, p_slc, p_win, p_pool = _prompt_path(x_prompt, wts, g_attn, w_pool[0], pool_scale, *mlp_w)
    y_s, s_cmp, s_slc, s_win, s_pool = _sample_path(x_sample, cache_cmp[0], cache_slc[0], state_win[0],
                                                    state_pool[0], page_table, wts, g_attn, w_pool[0],
                                                    pool_scale, *mlp_w)
    return (y_p, y_s, p_cmp, p_slc, p_win, p_pool, s_cmp, s_slc, s_win, s_pool)
```

```python
import functools
import math

import jax
import jax.numpy as jnp
from jax import lax
from jax.experimental import pallas as pl
from jax.experimental.pallas import tpu as pltpu

D_MODEL = 1024
N_HEADS = 8
N_KV = 2
HPG = N_HEADS // N_KV
HEAD_DIM = 64
ATTN_WIDTH = N_HEADS * HEAD_DIM
POOL_WIDTH = D_MODEL - ATTN_WIDTH
POOL_WINDOWS = (2, 4, 8, 16)
POOL_GROUP = POOL_WIDTH // len(POOL_WINDOWS)
POOL_HIST = max(POOL_WINDOWS) - 1
BLK = 64
N_SELECT = 16
WINDOW = 512
CMP_HIDDEN = 2 * HEAD_DIM
D_FF = 4 * D_MODEL
KV_WIDTH = N_KV * 2 * HEAD_DIM
PAGE_SIZE = 128
BLOCKS_PER_PAGE = PAGE_SIZE // BLK
SCALE = HEAD_DIM ** -0.5
FORCE_SCORE = float(HPG + 1)
NORM_EPS = 1e-6
NEG = -1e30
N_GATE_PAD = 128
HALO = 16

F32 = jnp.float32
BF16 = jnp.bfloat16
NT_DIMS = (((1,), (1,)), ((), ()))

V7X_VMEM_LIMIT = 56 * 1024 * 1024


def _slope(head):
    return float(2.0 ** (-8.0 * (head + 1) / N_HEADS))


def _cparams(sem, vmem=None, flags=None):
    return pltpu.CompilerParams(dimension_semantics=sem, vmem_limit_bytes=vmem, flags=flags)


def _proj_kernel(x_ref, g_ref, wtok_ref, wkv_ref, q_ref, gate_ref, u_ref, kc_ref, ks_ref, kw_ref, kvtb_ref, *rest,
                 tm, emit_tok_kv, emit_pages):
    x = x_ref[0]
    ms = jnp.mean(x * x, axis=-1, keepdims=True)
    xn = ((x * lax.rsqrt(ms + NORM_EPS)) * g_ref[...]).astype(BF16)
    tok = lax.dot_general(xn, wtok_ref[...], NT_DIMS, preferred_element_type=F32)
    q_ref[0] = (tok[:, :ATTN_WIDTH] * SCALE).astype(BF16)
    u_ref[0] = tok[:, ATTN_WIDTH:ATTN_WIDTH + POOL_WIDTH]
    gl = tok[:, ATTN_WIDTH + POOL_WIDTH:ATTN_WIDTH + POOL_WIDTH + N_GATE_PAD]
    gate_ref[0] = 1.0 / (1.0 + jnp.exp(-gl))
    kvt = lax.dot_general(wkv_ref[...], xn, NT_DIMS, preferred_element_type=F32)
    for br, ref in enumerate((kc_ref, ks_ref, kw_ref)):
        ref[0] = kvt[br * KV_WIDTH:(br + 1) * KV_WIDTH]
    for br in range(2):
        kvtb_ref[br, 0] = kvt[(br + 1) * KV_WIDTH:(br + 2) * KV_WIDTH].astype(BF16)
    k = 0
    if emit_tok_kv:
        base = ATTN_WIDTH + POOL_WIDTH + N_GATE_PAD
        rest[k][0] = tok[:, base:base + 3 * KV_WIDTH]
        k += 1
    if emit_pages:
        for pg in range(tm // PAGE_SIZE):
            rest[k][pg] = kvt[:KV_WIDTH, pg * PAGE_SIZE:(pg + 1) * PAGE_SIZE]


def _proj(x, g_attn, wtok, wkv, *, tm, emit_tok_kv, emit_pages):
    B, T, _ = x.shape
    nt = T // tm
    out_shape = [
        jax.ShapeDtypeStruct((B, T, ATTN_WIDTH), BF16),
        jax.ShapeDtypeStruct((B, T, N_GATE_PAD), F32),
        jax.ShapeDtypeStruct((B, T, POOL_WIDTH), F32),
    ] + [jax.ShapeDtypeStruct((B, KV_WIDTH, T), F32)] * 3 + [
        jax.ShapeDtypeStruct((2, B, KV_WIDTH, T), BF16),
    ]
    out_specs = [
        pl.BlockSpec((1, tm, ATTN_WIDTH), lambda b, i: (b, i, 0)),
        pl.BlockSpec((1, tm, N_GATE_PAD), lambda b, i: (b, i, 0)),
        pl.BlockSpec((1, tm, POOL_WIDTH), lambda b, i: (b, i, 0)),
    ] + [pl.BlockSpec((1, KV_WIDTH, tm), lambda b, i: (b, 0, i))] * 3 + [
        pl.BlockSpec((2, 1, KV_WIDTH, tm), lambda b, i: (0, b, 0, i)),
    ]
    if emit_tok_kv:
        out_shape.append(jax.ShapeDtypeStruct((B, T, 3 * KV_WIDTH), F32))
        out_specs.append(pl.BlockSpec((1, tm, 3 * KV_WIDTH), lambda b, i: (b, i, 0)))
    if emit_pages:
        ppt = tm // PAGE_SIZE
        out_shape.append(jax.ShapeDtypeStruct((B * T // PAGE_SIZE, KV_WIDTH, PAGE_SIZE), F32))
        out_specs.append(pl.BlockSpec((ppt, KV_WIDTH, PAGE_SIZE), lambda b, i: (b * nt + i, 0, 0)))
    return pl.pallas_call(
        functools.partial(_proj_kernel, tm=tm, emit_tok_kv=emit_tok_kv, emit_pages=emit_pages),
        grid=(B, nt),
        in_specs=[
            pl.BlockSpec((1, tm, D_MODEL), lambda b, i: (b, i, 0)),
            pl.BlockSpec((1, D_MODEL), lambda b, i: (0, 0)),
            pl.BlockSpec(wtok.shape, lambda b, i: (0, 0)),
            pl.BlockSpec(wkv.shape, lambda b, i: (0, 0)),
        ],
        out_specs=out_specs,
        out_shape=out_shape,
        compiler_params=_cparams(("arbitrary", "arbitrary"), V7X_VMEM_LIMIT),
        name="proj",
    )(x, g_attn, wtok, wkv)


def _gelu_tanh(x):
    c = math.sqrt(2.0 / math.pi)
    return 0.5 * x * (1.0 + jnp.tanh(c * (x + 0.044715 * (x * x * x))))


def _compress_kernel(ids_ref, src_ref, pet_ref, w1_ref, w2_ref, out_ref, buf, sem, *, P, n_steps):
    i = pl.program_id(0)

    def page_copy(step, slot, p):
        return pltpu.make_async_copy(src_ref.at[ids_ref[step * P + p]], buf.at[slot, :, p, :], sem.at[slot])

    def start(step, slot):
        for p in range(P):
            page_copy(step, slot, p).start()

    @pl.when(i == 0)
    def _():
        start(0, 0)

    @pl.when(i + 1 < n_steps)
    def _():
        start(i + 1, (i + 1) % 2)

    slot = i % 2
    for p in range(P):
        page_copy(i, slot, p).wait()

    bref = buf.at[slot]
    for c in range(2):
        acc = jnp.zeros((N_KV * P, 2 * CMP_HIDDEN), F32)
        for dp in range(HEAD_DIM // 2):
            rows = []
            for g in range(N_KV):
                halves = []
                for dd in range(2):
                    r = g * 2 * HEAD_DIM + c * HEAD_DIM + 2 * dp + dd
                    pr = c * HEAD_DIM + 2 * dp + dd
                    halves.append(bref[r] + pet_ref[pr:pr + 1, :])
                rows.append(jnp.concatenate(halves, axis=1))
            lhs = jnp.concatenate(rows, axis=0).astype(BF16)
            zero = jnp.zeros((BLK, CMP_HIDDEN), BF16)
            w_rows = []
            for dd in range(2):
                w = w1_ref[c, dp, dd]
                w_rows += [jnp.concatenate([w, zero], axis=1), jnp.concatenate([zero, w], axis=1)]
            acc = acc + jnp.dot(lhs, jnp.concatenate(w_rows, axis=0), preferred_element_type=F32)
        hid = _gelu_tanh(acc).astype(BF16)
        oc = jnp.dot(hid, w2_ref[c], preferred_element_type=F32).astype(BF16)
        for g in range(N_KV):
            out_ref[g, c] = oc[g * P:(g + 1) * P]


def _compress(ids, src, pet, w1bd, w2bd, *, P):
    n_total = ids.shape[0]
    n_steps = n_total // P
    grid_spec = pltpu.PrefetchScalarGridSpec(
        num_scalar_prefetch=1,
        grid=(n_steps,),
        in_specs=[
            pl.BlockSpec(memory_space=pl.ANY),
            pl.BlockSpec(pet.shape, lambda i, ids: (0, 0)),
            pl.BlockSpec(w1bd.shape, lambda i, ids: (0, 0, 0, 0, 0), pipeline_mode=pl.Buffered(1)),
            pl.BlockSpec(w2bd.shape, lambda i, ids: (0, 0, 0)),
        ],
        out_specs=pl.BlockSpec((N_KV, 2, P, PAGE_SIZE), lambda i, ids: (0, 0, i, 0)),
        scratch_shapes=[
            pltpu.VMEM((2, KV_WIDTH, P, PAGE_SIZE), F32),
            pltpu.SemaphoreType.DMA((2,)),
        ],
    )
    return pl.pallas_call(
        functools.partial(_compress_kernel, P=P, n_steps=n_steps),
        grid_spec=grid_spec,
        out_shape=jax.ShapeDtypeStruct((N_KV, 2, n_total, PAGE_SIZE), BF16),
        compiler_params=_cparams(("arbitrary",), V7X_VMEM_LIMIT),
        name="compress",
    )(ids, src, pet, w1bd, w2bd)


def _compress_weights(cmp_pe, cmp_w1, cmp_w2):
    eye = jnp.eye(BLOCKS_PER_PAGE, dtype=F32)
    pet = jnp.tile(jnp.transpose(cmp_pe, (1, 2, 0)).reshape(2 * HEAD_DIM, BLK), (1, BLOCKS_PER_PAGE))
    w1 = cmp_w1.reshape(2, BLK, HEAD_DIM // 2, 2, CMP_HIDDEN)
    w1bd = jnp.transpose(w1, (0, 2, 3, 1, 4)).astype(BF16)
    w2bd = jnp.einsum('chd,jk->cjhkd', cmp_w2, eye).reshape(
        2, BLOCKS_PER_PAGE * CMP_HIDDEN, BLOCKS_PER_PAGE * HEAD_DIM).astype(BF16)
    return pet, w1bd, w2bd


def _rank_select_t(score, nblk, n_select):
    tq, W = score.shape
    x = score.T
    groups = [x[8 * k:8 * k + 8] for k in range(nblk // 8)]
    ranks = [jnp.zeros((8, tq), jnp.int32) for _ in groups]
    row = lax.broadcasted_iota(jnp.int32, (8, tq), 0)
    for i in range(nblk):
        xi = groups[i // 8][i % 8:i % 8 + 1, :]
        for k, xg in enumerate(groups):
            if 8 * k + 7 < i:
                ahead = xi > xg
            elif 8 * k > i:
                ahead = xi >= xg
            else:
                ahead = (xi > xg) | ((xi == xg) & (row > i - 8 * k))
            ranks[k] = ranks[k] + jnp.where(ahead, 1, 0)
    sel = [jnp.where(r < n_select, 1.0, 0.0) for r in ranks]
    if nblk < W:
        sel.append(jnp.zeros((W - nblk, tq), F32))
    return jnp.concatenate(sel, axis=0).T


def _key_consts(T, npg):
    assert T <= 256 * BLK and npg * BLOCKS_PER_PAGE <= HEAD_DIM
    k = jnp.arange(T, dtype=jnp.int32)[None, :]
    r = jnp.arange(HEAD_DIM, dtype=jnp.int32)[:, None]
    pos = jnp.where(r == 0, k // BLK, jnp.where(r == 1, k % BLK, 0)).astype(F32)
    en = jnp.where(k // BLK == r, NEG, 0.0)
    return jnp.concatenate([pos, en, jnp.zeros((HEAD_DIM, T), F32)], axis=0).astype(BF16)


M_INIT = -3e38


def _p_attn_kernel(q_ref, gate_ref, comp_ref, ks_ref, kw_ref, kc_ref, o_ref,
                   qx, m_ref, acc_ref, out_ref, flag_v, flag_s, flag_sem, *, tq, tk, npg, row_chunk):
    qi = pl.program_id(1)
    q0 = qi * tq
    pos_t = q0 + lax.broadcasted_iota(jnp.int32, (tq, 1), 0)
    nblk = npg * BLOCKS_PER_PAGE
    W = 128
    R = HPG * tq
    lane_h = lax.broadcasted_iota(jnp.int32, (1, HEAD_DIM), 1)

    for hh in range(N_HEADS):
        g, h = divmod(hh, HPG)
        rows = pl.ds(h * tq, tq)
        qx[g, rows, 0:HEAD_DIM] = q_ref[0, :, hh * HEAD_DIM:(hh + 1) * HEAD_DIM]
        posc = jnp.where(lane_h == 0, _slope(hh) * BLK, jnp.where(lane_h == 1, _slope(hh), 0.0))
        qx[g, rows, HEAD_DIM:2 * HEAD_DIM] = jnp.broadcast_to(posc, (tq, HEAD_DIM)).astype(BF16)
        qx[g, rows, 3 * HEAD_DIM:4 * HEAD_DIM] = jnp.zeros((tq, HEAD_DIM), BF16)

    blk_n = lax.broadcasted_iota(jnp.int32, (1, W), 1)
    first = blk_n < nblk
    end_pos = (blk_n + 1) * BLK - 1
    dist_c = (pos_t - end_pos).astype(F32)
    mask_c = (dist_c >= 0) & first
    cur = pos_t // BLK
    row_w = lax.broadcasted_iota(jnp.int32, (W, W), 0)
    col_w = lax.broadcasted_iota(jnp.int32, (W, W), 1)
    perm = jnp.where((col_w == (row_w % 2) * npg + row_w // 2) & (row_w < nblk), 1.0, 0.0).astype(BF16)

    used = []
    for g in range(N_KV):
        ck = comp_ref[g, 0].astype(F32)
        cv = comp_ref[g, 1].astype(F32)
        pad = [jnp.zeros((W - nblk, HEAD_DIM), F32)] if nblk < W else []
        ckp = jnp.concatenate([ck[:, :HEAD_DIM], ck[:, HEAD_DIM:]] + pad, axis=0).astype(BF16)
        cvp = jnp.concatenate([cv[:, :HEAD_DIM], cv[:, HEAD_DIM:]] + pad, axis=0).astype(BF16)
        ckp = jnp.dot(perm, ckp, preferred_element_type=F32).astype(BF16)
        cvp = jnp.dot(perm, cvp, preferred_element_type=F32).astype(BF16)
        imp = jnp.zeros((tq, W), F32)
        for h in range(HPG):
            hh = g * HPG + h
            qh = qx[g, pl.ds(h * tq, tq), 0:HEAD_DIM]
            s = lax.dot_general(qh, ckp, NT_DIMS, preferred_element_type=F32)
            s = jnp.where(mask_c, s - _slope(hh) * dist_c, NEG)
            mx = jnp.max(s, axis=1, keepdims=True)
            e = jnp.where(mask_c, jnp.exp(s - mx), 0.0)
            p = e / jnp.maximum(jnp.sum(e, axis=1, keepdims=True), 1e-30)
            imp = imp + p
            oc = jnp.dot(p.astype(BF16), cvp, preferred_element_type=F32)
            gcol = gate_ref[0, :, hh:hh + 1]
            out_ref[:, hh * HEAD_DIM:(hh + 1) * HEAD_DIM] = gcol * oc
        forced = (blk_n == 0) | (blk_n == cur) | (blk_n == cur - 1)
        score = jnp.where(forced, FORCE_SCORE, imp)
        score = jnp.where(blk_n > cur, -1.0, score)
        sel = _rank_select_t(score, nblk, N_SELECT)
        picked = (sel > 0.5) & (blk_n <= cur)
        notsel = jnp.where(picked, 0.0, 1.0).astype(BF16)
        for h in range(HPG):
            qx[g, pl.ds(h * tq, tq), 2 * HEAD_DIM:3 * HEAD_DIM] = notsel[:, :HEAD_DIM]
        used.append(jnp.max(jnp.where(picked, 1, 0), axis=0, keepdims=True))

    flag_v[...] = jnp.concatenate(used + [jnp.zeros((8 - N_KV, W), jnp.int32)], axis=0)
    flag_copy = pltpu.make_async_copy(flag_v, flag_s, flag_sem.at[0])
    flag_copy.start()

    pos_rows = q0 + lax.broadcasted_iota(jnp.int32, (R, 1), 0) % tq

    ones_rows = jnp.where(lax.broadcasted_iota(jnp.int32, (HEAD_DIM, 1), 0) == 0, 1.0, 0.0)

    def attend_tile(kv_ref, k0, width, kc_rows, mode, groups=tuple(range(N_KV))):
        kpos = k0 + lax.broadcasted_iota(jnp.int32, (1, width), 1)
        kc_tile = kc_ref[0:kc_rows, pl.ds(k0, width)]
        ones_tile = jnp.broadcast_to(ones_rows, (HEAD_DIM, width)).astype(BF16)
        if mode == "causal":
            keep = kpos <= pos_rows
        elif mode == "recent":
            keep = kpos > pos_rows - WINDOW
        for g in groups:
            kt = kv_ref[0, 0, g * 2 * HEAD_DIM:g * 2 * HEAD_DIM + HEAD_DIM, pl.ds(k0, width)]
            kext = jnp.concatenate([kt, kc_tile], axis=0)
            vt = kv_ref[0, 0, g * 2 * HEAD_DIM + HEAD_DIM:(g + 1) * 2 * HEAD_DIM, pl.ds(k0, width)]
            vext = jnp.concatenate([vt, ones_tile], axis=0)
            for c in range(R // row_chunk):
                rc = pl.ds(c * row_chunk, row_chunk)
                s = jnp.dot(qx[g, rc, 0:HEAD_DIM + kc_rows], kext, preferred_element_type=F32)
                if mode != "full":
                    s = jnp.where(keep[c * row_chunk:(c + 1) * row_chunk], s, NEG)
                m_prev = m_ref[g, rc, :]
                m_next = jnp.maximum(m_prev, jnp.max(s, axis=1, keepdims=True))
                alpha = jnp.exp(m_prev - m_next)
                m_ref[g, rc, :] = m_next
                p = jnp.exp(s - jnp.concatenate([m_next] * (width // W), axis=1)).astype(BF16)
                pv = lax.dot_general(p, vext, NT_DIMS, preferred_element_type=F32)
                acc_ref[g, rc, :] = alpha * acc_ref[g, rc, :] + pv

    def reset():
        for g in range(N_KV):
            m_ref[g] = jnp.full((R, W), M_INIT, F32)
            acc_ref[g] = jnp.zeros((R, W), F32)

    def finish(gate_base):
        for g in range(N_KV):
            for h in range(HPG):
                hh = g * HPG + h
                a = acc_ref[g, pl.ds(h * tq, tq), :]
                o = a[:, :HEAD_DIM] / jnp.maximum(a[:, HEAD_DIM:HEAD_DIM + 1], 1e-30)
                gcol = gate_ref[0, :, gate_base + hh:gate_base + hh + 1]
                out_ref[:, hh * HEAD_DIM:(hh + 1) * HEAD_DIM] += gcol * o

    reset()
    wt = WINDOW // tq

    @pl.when(qi >= wt)
    def _():
        attend_tile(kw_ref, pl.multiple_of((qi - wt) * tq, tq), WINDOW, HEAD_DIM, "recent")

    @pl.when(qi < wt)
    def _():
        def win_body(j, carry):
            attend_tile(kw_ref, pl.multiple_of(j * tq, tq), tq, HEAD_DIM, "full")
            return carry

        lax.fori_loop(0, qi, win_body, 0)

    attend_tile(kw_ref, pl.multiple_of(q0, tq), tq, HEAD_DIM, "causal")
    finish(2 * N_HEADS)

    reset()
    flag_copy.wait()
    n_sel = (q0 + tq + tk - 1) // tk
    bpt = tk // BLK

    def sel_body(j, carry):
        for g in range(N_KV):
            hit = flag_s[g, j * bpt]
            for u in range(1, bpt):
                hit = hit | flag_s[g, j * bpt + u]

            @pl.when(hit != 0)
            def _():
                attend_tile(ks_ref, pl.multiple_of(j * tk, tk), tk, 3 * HEAD_DIM, "full", groups=(g,))

        return carry

    lax.fori_loop(0, n_sel - 1, sel_body, 0)
    attend_tile(ks_ref, pl.multiple_of((n_sel - 1) * tk, tk), tk, 3 * HEAD_DIM, "causal")
    finish(N_HEADS)
    o_ref[0] = out_ref[...].astype(BF16)


def _p_attn(q, gates, comp, kvtb, kconst, *, tq, tk):
    B, T, _ = q.shape
    npg = T // PAGE_SIZE
    assert WINDOW % tq == 0 and tk % tq == 0
    return pl.pallas_call(
        functools.partial(_p_attn_kernel, tq=tq, tk=tk, npg=npg, row_chunk=min(1024, HPG * tq)),
        grid=(B, T // tq),
        in_specs=[
            pl.BlockSpec((1, tq, ATTN_WIDTH), lambda b, i: (b, i, 0)),
            pl.BlockSpec((1, tq, N_GATE_PAD), lambda b, i: (b, i, 0)),
            pl.BlockSpec((N_KV, 2, npg, PAGE_SIZE), lambda b, i: (0, 0, b, 0)),
            pl.BlockSpec((1, 1, KV_WIDTH, T), lambda b, i: (0, b, 0, 0)),
            pl.BlockSpec((1, 1, KV_WIDTH, T), lambda b, i: (1, b, 0, 0)),
            pl.BlockSpec(kconst.shape, lambda b, i: (0, 0)),
        ],
        out_specs=pl.BlockSpec((1, tq, ATTN_WIDTH), lambda b, i: (b, i, 0)),
        out_shape=jax.ShapeDtypeStruct((B, T, ATTN_WIDTH), BF16),
        scratch_shapes=[
            pltpu.VMEM((N_KV, HPG * tq, 4 * HEAD_DIM), BF16),
            pltpu.VMEM((N_KV, HPG * tq, 128), F32),
            pltpu.VMEM((N_KV, HPG * tq, 128), F32),
            pltpu.VMEM((tq, ATTN_WIDTH), F32),
            pltpu.VMEM((8, 128), jnp.int32),
            pltpu.SMEM((8, 128), jnp.int32),
            pltpu.SemaphoreType.DMA((1,)),
        ],
        compiler_params=_cparams(("arbitrary", "arbitrary"), V7X_VMEM_LIMIT),
        name="p_attn",
    )(q, gates, comp, kvtb, kvtb, kconst)


def _stack_heads(q_ref, g, ts):
    parts = [q_ref[0, :, (g * HPG + h) * HEAD_DIM:(g * HPG + h + 1) * HEAD_DIM].astype(F32) for h in range(HPG)]
    return jnp.concatenate(parts, axis=0).astype(BF16)


def _row_consts(g, ts):
    R = HPG * ts
    row = lax.broadcasted_iota(jnp.int32, (R, 1), 0)
    t_row = row % ts
    h_row = row // ts
    slope = jnp.zeros((R, 1), F32)
    for h in range(HPG):
        slope = jnp.where(h_row == h, _slope(g * HPG + h), slope)
    return t_row, slope


def _s_select_kernel(q_ref, gate_ref, comp_ref, cnew_ref, oc_ref, bits_ref, out_s, *, past, ts):
    W = 128
    nb_past = past // BLK
    lane = lax.broadcasted_iota(jnp.int32, (1, W), 1)
    idx_fns = [lambda c: 2 * c, lambda c: 2 * c + 1, lambda c: jnp.where(c == 0, nb_past, (1 << 20) + c)]
    idx_tiles = [f(lane) for f in idx_fns]
    valid_n = lane == 0
    tok = lax.broadcasted_iota(jnp.int32, (ts, 1), 0)
    cur_t = (past + tok) // BLK
    for g in range(N_KV):
        qg = _stack_heads(q_ref, g, ts)
        t_row, slope = _row_consts(g, ts)
        pos = past + t_row
        ck = comp_ref[g, 0]
        cv = comp_ref[g, 1]
        cn = cnew_ref[g, 0, 0].astype(F32)[:, :HEAD_DIM]
        vn = cnew_ref[g, 1, 0].astype(F32)[:, :HEAD_DIM]
        s_t = [lax.dot_general(qg, ck[:, :HEAD_DIM], NT_DIMS, preferred_element_type=F32),
               lax.dot_general(qg, ck[:, HEAD_DIM:], NT_DIMS, preferred_element_type=F32),
               jnp.broadcast_to(jnp.sum(qg.astype(F32) * cn, axis=1, keepdims=True), (HPG * ts, W))]
        masks, es = [], []
        for k in range(3):
            end_pos = (idx_tiles[k] + 1) * BLK - 1
            dist = (pos - end_pos).astype(F32)
            mk = dist >= 0
            if k == 2:
                mk = mk & valid_n
            masks.append(mk)
            s_t[k] = jnp.where(mk, s_t[k] - slope * dist, NEG)
        mx = jnp.maximum(jnp.maximum(jnp.max(s_t[0], axis=1, keepdims=True),
                                     jnp.max(s_t[1], axis=1, keepdims=True)),
                         jnp.max(s_t[2], axis=1, keepdims=True))
        for k in range(3):
            es.append(jnp.where(masks[k], jnp.exp(s_t[k] - mx), 0.0))
        den = (jnp.sum(es[0], axis=1, keepdims=True) + jnp.sum(es[1], axis=1, keepdims=True)
               + jnp.sum(es[2], axis=1, keepdims=True))
        inv = 1.0 / jnp.maximum(den, 1e-30)
        ps = [e * inv for e in es]
        o_c = (jnp.dot(ps[0].astype(BF16), cv[:, :HEAD_DIM], preferred_element_type=F32)
               + jnp.dot(ps[1].astype(BF16), cv[:, HEAD_DIM:], preferred_element_type=F32)
               + ps[2][:, 0:1].astype(BF16).astype(F32) * vn)
        for h in range(HPG):
            hh = g * HPG + h
            gcol = gate_ref[0, :, hh:hh + 1]
            out_s[:, hh * HEAD_DIM:(hh + 1) * HEAD_DIM] = gcol * o_c[h * ts:(h + 1) * ts]
        scores = []
        for k in range(3):
            imp = ps[k][0:ts]
            for h in range(1, HPG):
                imp = imp + ps[k][h * ts:(h + 1) * ts]
            idx = idx_tiles[k]
            forced = (idx == 0) | (idx == cur_t) | (idx == cur_t - 1)
            sc = jnp.where(forced, FORCE_SCORE, imp)
            sc = jnp.where(idx > cur_t, -1.0, sc)
            if k == 2:
                sc = jnp.where(valid_n, sc, -2.0)
            scores.append(sc)
        s_new = scores[2][:, 0:1]
        parts = [[jnp.where(s_new > scores[ka], 1, 0), jnp.zeros((ts, W), jnp.int32)] for ka in range(2)]
        for kb in range(2):
            for r in range(W):
                y = pltpu.roll(scores[kb], r, 1) if r else scores[kb]
                yi = idx_fns[kb]((lane - r) & (W - 1))
                for ka in range(2):
                    if ka == kb and r == 0:
                        continue
                    ahead = (y > scores[ka]) | ((y == scores[ka]) & (yi < idx_tiles[ka]))
                    parts[ka][r % 2] = parts[ka][r % 2] + jnp.where(ahead, 1, 0)
        sel = [(parts[k][0] + parts[k][1]) < N_SELECT for k in range(2)]
        rank_new = (jnp.sum(jnp.where(scores[0] >= s_new, 1, 0), axis=1, keepdims=True)
                    + jnp.sum(jnp.where(scores[1] >= s_new, 1, 0), axis=1, keepdims=True))
        wt = jnp.left_shift(1, 2 * tok)
        page_bits = jnp.sum(jnp.where(sel[0], wt, 0) + jnp.where(sel[1], 2 * wt, 0), axis=0, keepdims=True)
        new_bits = jnp.sum(jnp.where((rank_new < N_SELECT) & valid_n, jnp.left_shift(1, tok), 0),
                           axis=0, keepdims=True)
        bits_ref[0, g, 0:1, :] = page_bits
        bits_ref[0, g, 1:2, :] = new_bits
    oc_ref[0] = out_s[...]


def _s_select(q, gates, comp, cnew, *, past):
    Bs, ts, _ = q.shape
    npg = past // PAGE_SIZE
    assert npg == 128, "one lane tile of pages per sequence"
    return pl.pallas_call(
        functools.partial(_s_select_kernel, past=past, ts=ts),
        grid=(Bs,),
        in_specs=[
            pl.BlockSpec((1, ts, ATTN_WIDTH), lambda b: (b, 0, 0)),
            pl.BlockSpec((1, ts, N_GATE_PAD), lambda b: (b, 0, 0)),
            pl.BlockSpec((N_KV, 2, npg, PAGE_SIZE), lambda b: (0, 0, b, 0)),
            pl.BlockSpec((N_KV, 2, 1, 1, PAGE_SIZE), lambda b: (0, 0, b, 0, 0)),
        ],
        out_specs=[
            pl.BlockSpec((1, ts, ATTN_WIDTH), lambda b: (b, 0, 0)),
            pl.BlockSpec((1, N_KV, 2, 128), lambda b: (b, 0, 0, 0)),
        ],
        out_shape=[
            jax.ShapeDtypeStruct((Bs, ts, ATTN_WIDTH), F32),
            jax.ShapeDtypeStruct((Bs, N_KV, 2, 128), jnp.int32),
        ],
        scratch_shapes=[pltpu.VMEM((ts, ATTN_WIDTH), F32)],
        compiler_params=_cparams(("arbitrary",), V7X_VMEM_LIMIT),
        name="s_select",
    )(q, gates, comp, cnew)


def _s_attn_kernel(pt_ref, bits_ref, q_ref, gate_ref, oc_ref, win_ref, knew_ref, slc_ref,
                   o_ref, wout_ref, buf, sem, plist, cnts, out_s, *, past, ts, npg, chunk):
    b = pl.program_id(0)
    g = pl.program_id(1)
    W = 128
    R = HPG * ts
    wk = win_ref.shape[2]
    lin = b * N_KV + g
    n_lin = pl.num_programs(0) * N_KV
    base = lin * 2 * W
    slot = lin % 2

    def page_copy(page, gg, sl, idx):
        return pltpu.make_async_copy(slc_ref.at[page, pl.ds(gg * 2 * HEAD_DIM, 2 * HEAD_DIM)],
                                     buf.at[sl, idx], sem.at[sl])

    def issue_all(ln, sl):
        bb = ln // N_KV
        gg = ln % N_KV

        def scan(p, cnt):
            plist[sl * npg + cnt] = p
            return cnt + (bits_ref[ln * 2 * W + p] != 0).astype(jnp.int32)

        cnt = lax.fori_loop(0, npg, scan, 0, unroll=8)
        cnts[sl] = cnt

        def start(s, c):
            page_copy(pt_ref[bb * npg + plist[sl * npg + s]], gg, sl, s).start()
            return c

        lax.fori_loop(0, cnt, start, 0)

    @pl.when(lin == 0)
    def _():
        buf[...] = jnp.zeros(buf.shape, F32)
        for k in range(2 * npg):
            plist[k] = 0
        issue_all(lin, slot)

    @pl.when(lin + 1 < n_lin)
    def _():
        issue_all(lin + 1, 1 - slot)

    cnt = cnts[slot]

    q_all = q_ref[0].astype(F32)
    qg = jnp.concatenate([q_all[:, h * HEAD_DIM:(h + 1) * HEAD_DIM] for h in range(HPG)], axis=0).astype(BF16)
    row = lax.broadcasted_iota(jnp.int32, (R, 1), 0)
    t_row = row % ts
    head_row = row // ts + g * HPG
    slope = jnp.zeros((R, 1), F32)
    for hh in range(N_HEADS):
        slope = jnp.where(head_row == hh, _slope(hh), slope)
    lane = lax.broadcasted_iota(jnp.int32, (1, W), 1)

    kw = win_ref[0, 0:HEAD_DIM, :].astype(BF16)
    vw = win_ref[0, HEAD_DIM:2 * HEAD_DIM, :].astype(BF16)
    kn = knew_ref[2, 0, 0:HEAD_DIM, :].astype(BF16)
    vn = knew_ref[2, 0, HEAD_DIM:2 * HEAD_DIM, :].astype(BF16)
    i_st = lax.broadcasted_iota(jnp.int32, (1, wk), 1)
    d_st = t_row + wk - i_st
    m_st = (d_st >= 0) & (d_st < WINDOW)
    d_nw = t_row - lane
    m_nw = (d_nw >= 0) & (d_nw < WINDOW) & (lane < ts)
    s_st = jnp.where(m_st, jnp.dot(qg, kw, preferred_element_type=F32) - slope * d_st.astype(F32), NEG)
    s_nw = jnp.where(m_nw, jnp.dot(qg, kn, preferred_element_type=F32) - slope * d_nw.astype(F32), NEG)
    mx = jnp.maximum(jnp.max(s_st, axis=1, keepdims=True), jnp.max(s_nw, axis=1, keepdims=True))
    e_st = jnp.where(m_st, jnp.exp(s_st - mx), 0.0)
    e_nw = jnp.where(m_nw, jnp.exp(s_nw - mx), 0.0)
    den = jnp.sum(e_st, axis=1, keepdims=True) + jnp.sum(e_nw, axis=1, keepdims=True)
    inv = 1.0 / jnp.maximum(den, 1e-30)
    o_w = (lax.dot_general((e_st * inv).astype(BF16), vw, NT_DIMS, preferred_element_type=F32)
           + lax.dot_general((e_nw * inv).astype(BF16), vn, NT_DIMS, preferred_element_type=F32))

    last = wk - W
    shifted = pltpu.roll(win_ref[0], wk - ts, 1)
    newr = pltpu.roll(knew_ref[2, 0], W - ts, 1)
    wout_ref[0, :, 0:last] = shifted[:, 0:last]
    wout_ref[0, :, last:wk] = jnp.where(lane >= W - ts, newr, shifted[:, last:wk])

    bits_new = bits_ref[base + W]
    ksn = knew_ref[1, 0, 0:HEAD_DIM, :].astype(BF16)
    vsn = knew_ref[1, 0, HEAD_DIM:2 * HEAD_DIM, :].astype(BF16)
    d_sn = t_row - lane
    m_sn = (d_sn >= 0) & (lane < ts) & ((jnp.right_shift(bits_new, t_row) & 1) == 1)
    s_sn = jnp.where(m_sn, jnp.dot(qg, ksn, preferred_element_type=F32) - slope * d_sn.astype(F32), NEG)
    m0 = jnp.max(s_sn, axis=1, keepdims=True)
    e0 = jnp.where(m_sn, jnp.exp(s_sn - m0), 0.0)
    l0 = jnp.sum(e0, axis=1, keepdims=True)
    a0 = lax.dot_general(e0.astype(BF16), vsn, NT_DIMS, preferred_element_type=F32)

    def wait_one(s, c):
        page_copy(0, 0, slot, s).wait()
        return c

    lax.fori_loop(0, cnt, wait_one, 0)
    sh = 2 * t_row + (lane >= BLK).astype(jnp.int32)
    sh_c = jnp.concatenate([sh] * chunk, axis=1)
    bslot = buf.at[slot]

    def chunk_step(c, carry):
        m, l, acc = carry
        kts, vts, kpos, bitv = [], [], [], []
        for u in range(chunk):
            s = c * chunk + u
            ok = s < cnt
            p = jnp.where(ok, plist[slot * npg + s], 0)
            bits = jnp.where(ok, bits_ref[base + p], 0)
            kts.append(bslot[s, 0:HEAD_DIM, :])
            vts.append(bslot[s, HEAD_DIM:2 * HEAD_DIM, :])
            kpos.append(p * PAGE_SIZE + lane)
            bitv.append(jnp.zeros((1, W), jnp.int32) + bits)
        kt = jnp.concatenate(kts, axis=1).astype(BF16)
        vt = jnp.concatenate(vts, axis=1).astype(BF16)
        dist = past + t_row - jnp.concatenate(kpos, axis=1)
        mask = ((jnp.right_shift(jnp.concatenate(bitv, axis=1), sh_c) & 1) == 1) & (dist >= 0)
        sc = jnp.where(mask, jnp.dot(qg, kt, preferred_element_type=F32) - slope * dist.astype(F32), NEG)
        m_new = jnp.maximum(m, jnp.max(sc, axis=1, keepdims=True))
        alpha = jnp.exp(m - m_new)
        e = jnp.where(mask, jnp.exp(sc - m_new), 0.0)
        l = alpha * l + jnp.sum(e, axis=1, keepdims=True)
        acc = alpha * acc + lax.dot_general(e.astype(BF16), vt, NT_DIMS, preferred_element_type=F32)
        return m_new, l, acc

    m, l, acc = lax.fori_loop(0, (cnt + chunk - 1) // chunk, chunk_step, (m0, l0, a0))
    o_s = acc / jnp.maximum(l, 1e-30)

    g_vec = jnp.zeros((ts, 1), jnp.int32) + g
    for h in range(HPG):
        rows = slice(h * ts, (h + 1) * ts)
        g_s = jnp.zeros((ts, 1), F32)
        g_w = jnp.zeros((ts, 1), F32)
        for gg in range(N_KV):
            hh = gg * HPG + h
            g_s = jnp.where(g_vec == gg, gate_ref[0, :, N_HEADS + hh:N_HEADS + hh + 1], g_s)
            g_w = jnp.where(g_vec == gg, gate_ref[0, :, 2 * N_HEADS + hh:2 * N_HEADS + hh + 1], g_w)
        out_s[:, h * HEAD_DIM:(h + 1) * HEAD_DIM] = g_s * o_s[rows] + g_w * o_w[rows]
    o_ref[0] = oc_ref[0] + out_s[...]


def _s_attn(page_table, bits, q, gates, oc, win_t, knew, slc_t, *, past):
    Bs, ts, _ = q.shape
    npg = past // PAGE_SIZE
    wk = win_t.shape[2]
    gw = HPG * HEAD_DIM
    grid_spec = pltpu.PrefetchScalarGridSpec(
        num_scalar_prefetch=2,
        grid=(Bs, N_KV),
        in_specs=[
            pl.BlockSpec((1, ts, gw), lambda b, g, *_: (b, 0, g)),
            pl.BlockSpec((1, ts, N_GATE_PAD), lambda b, g, *_: (b, 0, 0)),
            pl.BlockSpec((1, ts, gw), lambda b, g, *_: (b, 0, g)),
            pl.BlockSpec((1, 2 * HEAD_DIM, wk), lambda b, g, *_: (b, g, 0)),
            pl.BlockSpec((3, 1, 2 * HEAD_DIM, PAGE_SIZE), lambda b, g, *_: (0, b, g, 0)),
            pl.BlockSpec(memory_space=pl.ANY),
        ],
        out_specs=[
            pl.BlockSpec((1, ts, gw), lambda b, g, *_: (b, 0, g)),
            pl.BlockSpec((1, 2 * HEAD_DIM, wk), lambda b, g, *_: (b, g, 0)),
        ],
        scratch_shapes=[
            pltpu.VMEM((2, npg, 2 * HEAD_DIM, PAGE_SIZE), F32),
            pltpu.SemaphoreType.DMA((2,)),
            pltpu.SMEM((2 * npg,), jnp.int32),
            pltpu.SMEM((2,), jnp.int32),
            pltpu.VMEM((ts, gw), F32),
        ],
    )
    chunk = 8
    assert npg % chunk == 0
    return pl.pallas_call(
        functools.partial(_s_attn_kernel, past=past, ts=ts, npg=npg, chunk=chunk),
        grid_spec=grid_spec,
        out_shape=[
            jax.ShapeDtypeStruct((Bs, ts, ATTN_WIDTH), F32),
            jax.ShapeDtypeStruct(win_t.shape, F32),
        ],
        compiler_params=_cparams(("arbitrary", "arbitrary"), V7X_VMEM_LIMIT),
        name="s_attn",
    )(page_table.reshape(-1), bits.reshape(-1), q, gates, oc, win_t, knew, slc_t)


def _pool_kernel(u_ref, halo_ref, wp_ref, ps_ref, o_ref, ext, *, nb, tp, tiles_per_seq, pos_base):
    i = pl.program_id(0)
    ext[:, 0:HALO, :] = halo_ref[...]
    ext[:, HALO:HALO + tp, :] = u_ref[...]
    tile = i * nb + lax.broadcasted_iota(jnp.int32, (nb, tp, 1), 0)
    pos = pos_base + (tile % tiles_per_seq) * tp + lax.broadcasted_iota(jnp.int32, (nb, tp, 1), 1)
    for gi, w in enumerate(POOL_WINDOWS):
        lanes = slice(gi * POOL_GROUP, (gi + 1) * POOL_GROUP)
        own = ext[:, HALO:HALO + tp, lanes]
        tot = own
        for k in range(1, w):
            tot = tot + ext[:, HALO - k:HALO - k + tp, lanes]
        cnt = jnp.minimum(w, pos + 1).astype(F32)
        d = (tot / cnt - own).astype(BF16).reshape(nb * tp, POOL_GROUP)
        y = jnp.dot(d, wp_ref[gi], preferred_element_type=F32) * ps_ref[:, lanes]
        o_ref[:, :, lanes] = y.reshape(nb, tp, POOL_GROUP).astype(o_ref.dtype)


def _pool(u, halo, w_pool, pool_scale, *, nb, tiles_per_seq, pos_base, out_dtype):
    n_tiles, tp, _ = u.shape
    return pl.pallas_call(
        functools.partial(_pool_kernel, nb=nb, tp=tp, tiles_per_seq=tiles_per_seq, pos_base=pos_base),
        grid=(n_tiles // nb,),
        in_specs=[
            pl.BlockSpec((nb, tp, POOL_WIDTH), lambda i: (i, 0, 0)),
            pl.BlockSpec((nb, HALO, POOL_WIDTH), lambda i: (i, 0, 0)),
            pl.BlockSpec(w_pool.shape, lambda i: (0, 0, 0)),
            pl.BlockSpec((1, POOL_WIDTH), lambda i: (0, 0)),
        ],
        out_specs=pl.BlockSpec((nb, tp, POOL_WIDTH), lambda i: (i, 0, 0)),
        out_shape=jax.ShapeDtypeStruct((n_tiles, tp, POOL_WIDTH), out_dtype),
        scratch_shapes=[pltpu.VMEM((nb, HALO + tp, POOL_WIDTH), F32)],
        compiler_params=_cparams(("arbitrary",), V7X_VMEM_LIMIT),
        name="pool",
    )(u, halo, w_pool, pool_scale)


def _rms(x, g):
    return (x * lax.rsqrt(jnp.mean(x * x, axis=-1, keepdims=True) + NORM_EPS)) * g


def _mlp_kernel(x_ref, o_ref, p_ref, wo_ref, gm_ref, wu_ref, wd_ref, gf_ref, y_ref, *, ff_chunk):
    mix = jnp.concatenate([o_ref[...], p_ref[...]], axis=1)
    h = x_ref[...] + jnp.dot(mix, wo_ref[...], preferred_element_type=F32)
    hn = _rms(h, gm_ref[...]).astype(BF16)
    acc = jnp.zeros(h.shape, F32)
    for c in range(D_FF // ff_chunk):
        a = jnp.dot(hn, wu_ref[:, c * ff_chunk:(c + 1) * ff_chunk], preferred_element_type=F32)
        a = jnp.square(jnp.maximum(a, 0.0)).astype(BF16)
        acc = acc + jnp.dot(a, wd_ref[c * ff_chunk:(c + 1) * ff_chunk, :], preferred_element_type=F32)
    y_ref[...] = _rms(h + acc, gf_ref[...])


def _mlp(x, o, p, w_out, g_mlp, w_up, w_down, g_final, *, tm):
    N = x.shape[0]
    const = lambda i: (0, 0)
    resident = lambda a: pl.BlockSpec(a.shape, const, pipeline_mode=pl.Buffered(1))
    return pl.pallas_call(
        functools.partial(_mlp_kernel, ff_chunk=1024),
        grid=(N // tm,),
        in_specs=[
            pl.BlockSpec((tm, D_MODEL), lambda i: (i, 0)),
            pl.BlockSpec((tm, ATTN_WIDTH), lambda i: (i, 0)),
            pl.BlockSpec((tm, POOL_WIDTH), lambda i: (i, 0)),
            resident(w_out), resident(g_mlp), resident(w_up), resident(w_down), resident(g_final),
        ],
        out_specs=pl.BlockSpec((tm, D_MODEL), lambda i: (i, 0)),
        out_shape=jax.ShapeDtypeStruct((N, D_MODEL), F32),
        compiler_params=_cparams(("arbitrary",), V7X_VMEM_LIMIT),
        name="mlp",
    )(x, o, p, w_out, g_mlp, w_up, w_down, g_final)


def _rows_from_kt(kt):
    B, _, S = kt.shape
    return jnp.transpose(kt.reshape(B, N_KV, 2, HEAD_DIM, S), (0, 4, 1, 2, 3))[None]


def _prep_weights(w_in, cmp_pe, cmp_w1, cmp_w2):
    w_t = jnp.transpose(w_in)
    a, b = ATTN_WIDTH, ATTN_WIDTH + 3 * KV_WIDTH
    wq, wkv = w_t[:a], w_t[a:b]
    wg, wu = w_t[b:b + 3 * N_HEADS], w_t[b + 3 * N_HEADS:]
    wg = jnp.pad(wg, ((0, N_GATE_PAD - 3 * N_HEADS), (0, 0)))
    wtok = jnp.concatenate([wq, wu, wg], axis=0).astype(BF16)
    wtok_kv = jnp.concatenate([wq, wu, wg, wkv], axis=0).astype(BF16)
    return wtok, wtok_kv, wkv.astype(BF16), _compress_weights(cmp_pe, cmp_w1, cmp_w2)


def _prompt_path(x, wts, g_attn, w_pool, pool_scale, w_out, g_mlp, w_up, w_down, g_final):
    wtok, _, wkv, (pet, w1bd, w2bd) = wts
    B, T, _ = x.shape
    tm = min(512, T)
    q, gates, u, kc_t, ks_t, kw_t, kvtb, pages = _proj(x, g_attn, wtok, wkv, tm=tm, emit_tok_kv=False,
                                                        emit_pages=True)
    n_pages = B * T // PAGE_SIZE
    comp = _compress(jnp.arange(n_pages, dtype=jnp.int32), pages, pet, w1bd, w2bd, P=min(64, n_pages))
    o = _p_attn(q, gates, comp, kvtb, _key_consts(T, T // PAGE_SIZE), tq=min(256, T), tk=min(512, T))
    nt = T // tm
    u4 = u.reshape(B, nt, tm, POOL_WIDTH)
    halo = jnp.concatenate([jnp.zeros((B, 1, HALO, POOL_WIDTH), F32), u4[:, :-1, tm - HALO:, :]], axis=1)
    pool = _pool(u.reshape(B * nt, tm, POOL_WIDTH), halo.reshape(B * nt, HALO, POOL_WIDTH),
                 w_pool, pool_scale, nb=1, tiles_per_seq=nt, pos_base=0, out_dtype=BF16)
    N = B * T
    y = _mlp(x.reshape(N, D_MODEL), o.reshape(N, ATTN_WIDTH), pool.reshape(N, POOL_WIDTH),
             w_out, g_mlp, w_up, w_down, g_final, tm=tm)
    wk = min(WINDOW, T)
    return (y.reshape(B, T, D_MODEL), _rows_from_kt(kc_t), _rows_from_kt(ks_t),
            _rows_from_kt(kw_t[:, :, T - wk:]), u[None, :, T - POOL_HIST:])


def _kt_view(rows):
    n, s = rows.shape[:2]
    return jnp.transpose(rows, (0, 2, 3, 4, 1)).reshape(n, KV_WIDTH, s)


def _sample_path(x, cache_cmp, cache_slc, state_win, state_pool, page_table, wts, g_attn,
                 w_pool, pool_scale, w_out, g_mlp, w_up, w_down, g_final):
    _, wtok_kv, wkv, (pet, w1bd, w2bd) = wts
    Bs, ts, _ = x.shape
    N = Bs * ts
    npg = page_table.shape[1]
    past = npg * PAGE_SIZE
    q, gates, u, kc_t, ks_t, kw_t, _, kv_tok = _proj(x.reshape(1, N, D_MODEL), g_attn, wtok_kv, wkv, tm=N,
                                                     emit_tok_kv=True, emit_pages=False)
    kvt = jnp.concatenate([kc_t, ks_t, kw_t], axis=0)
    knew = jnp.pad(jnp.transpose(kvt.reshape(3, KV_WIDTH, Bs, ts), (0, 2, 1, 3)),
                   ((0, 0), (0, 0), (0, 0), (0, PAGE_SIZE - ts)))
    comp_past = _compress(page_table.reshape(-1), _kt_view(cache_cmp), pet, w1bd, w2bd, P=min(128, Bs * npg))
    comp_new = _compress(jnp.arange(Bs, dtype=jnp.int32), knew[0], pet, w1bd, w2bd, P=Bs)
    qf = q.reshape(Bs, ts, ATTN_WIDTH).astype(F32)
    gts = gates.reshape(Bs, ts, N_GATE_PAD)
    oc, bits = _s_select(qf, gts, comp_past, comp_new.astype(F32).reshape(N_KV, 2, Bs, 1, PAGE_SIZE), past=past)
    o, win_new = _s_attn(page_table, bits, qf, gts, oc, _kt_view(state_win), knew, _kt_view(cache_slc), past=past)
    u3 = u.reshape(Bs, ts, POOL_WIDTH)
    halo = jnp.pad(state_pool, ((0, 0), (HALO - POOL_HIST, 0), (0, 0)))
    pool = _pool(u3, halo, w_pool, pool_scale, nb=Bs, tiles_per_seq=1, pos_base=past, out_dtype=F32)
    y = _mlp(x.reshape(N, D_MODEL), o.reshape(N, ATTN_WIDTH).astype(BF16), pool.reshape(N, POOL_WIDTH).astype(BF16),
             w_out, g_mlp, w_up, w_down, g_final, tm=N)
    kv_rows = kv_tok.reshape(Bs, ts, 3, N_KV, 2, HEAD_DIM)
    s_pool = jnp.concatenate([state_pool, u3], axis=1)[None, :, ts:]
    return (y.reshape(Bs, ts, D_MODEL), kv_rows[None, :, :, 0], kv_rows[None, :, :, 1],
            _rows_from_kt(win_new), s_pool)


def kernel(x_prompt, x_sample, cache_cmp, cache_slc, state_win, state_pool, page_table, g_attn, w_in,
           cmp_pe, cmp_w1, cmp_w2, w_pool, pool_scale, w_out, g_mlp, w_up, w_down, g_final):
    assert g_attn.shape[0] == 1, "single-layer trunk"
    wts = _prep_weights(w_in[0], cmp_pe[0], cmp_w1[0], cmp_w2[0])
    mlp_w = (w_out[0].astype(BF16), g_mlp, w_up[0].astype(BF16), w_down[0].astype(BF16), g_final[None])
    y_p, p_cmp, p_slc, p_win, p_pool = _prompt_path(x_prompt, wts, g_attn, w_pool[0], pool_scale, *mlp_w)
    y_s, s_cmp, s_slc, s_win, s_pool = _sample_path(x_sample, cache_cmp[0], cache_slc[0], state_win[0],
                                                    state_pool[0], page_table, wts, g_attn, w_pool[0],
                                                    pool_scale, *mlp_w)
    return (y_p, y_s, p_cmp, p_slc, p_win, p_pool, s_cmp, s_slc, s_win, s_pool)
```

```python
import functools
import math

import jax
import jax.numpy as jnp
from jax import lax
from jax.experimental import pallas as pl
from jax.experimental.pallas import tpu as pltpu

D_MODEL = 1024
N_HEADS = 8
N_KV = 2
HPG = N_HEADS // N_KV
HEAD_DIM = 64
ATTN_WIDTH = N_HEADS * HEAD_DIM
POOL_WIDTH = D_MODEL - ATTN_WIDTH
POOL_WINDOWS = (2, 4, 8, 16)
POOL_GROUP = POOL_WIDTH // len(POOL_WINDOWS)
POOL_HIST = max(POOL_WINDOWS) - 1
BLK = 64
N_SELECT = 16
WINDOW = 512
CMP_HIDDEN = 2 * HEAD_DIM
D_FF = 4 * D_MODEL
KV_WIDTH = N_KV * 2 * HEAD_DIM
PAGE_SIZE = 128
BLOCKS_PER_PAGE = PAGE_SIZE // BLK
SCALE = HEAD_DIM ** -0.5
FORCE_SCORE = float(HPG + 1)
NORM_EPS = 1e-6
NEG = -1e30
N_GATE_PAD = 128
HALO = 16

F32 = jnp.float32
BF16 = jnp.bfloat16
NT_DIMS = (((1,), (1,)), ((), ()))

V7X_VMEM_LIMIT = 56 * 1024 * 1024


def _slope(head):
    return float(2.0 ** (-8.0 * (head + 1) / N_HEADS))


def _cparams(sem, vmem=None, flags=None):
    return pltpu.CompilerParams(dimension_semantics=sem, vmem_limit_bytes=vmem, flags=flags)


def _proj_kernel(x_ref, g_ref, wtok_ref, wkv_ref, q_ref, gate_ref, u_ref, kc_ref, ks_ref, kw_ref, kvtb_ref, *rest,
                 tm, emit_tok_kv, emit_pages, seq_len):
    x = x_ref[0]
    ms = jnp.mean(x * x, axis=-1, keepdims=True)
    xn = ((x * lax.rsqrt(ms + NORM_EPS)) * g_ref[...]).astype(BF16)
    tok = lax.dot_general(xn, wtok_ref[...], NT_DIMS, preferred_element_type=F32)
    q_ref[0] = (tok[:, :ATTN_WIDTH] * SCALE).astype(BF16)
    u_ref[0] = tok[:, ATTN_WIDTH:ATTN_WIDTH + POOL_WIDTH]
    gl = tok[:, ATTN_WIDTH + POOL_WIDTH:ATTN_WIDTH + POOL_WIDTH + N_GATE_PAD]
    gate_ref[0] = 1.0 / (1.0 + jnp.exp(-gl))
    kvt = lax.dot_general(wkv_ref[...], xn, NT_DIMS, preferred_element_type=F32)
    for br, ref in enumerate((kc_ref, ks_ref, kw_ref)):
        ref[0] = kvt[br * KV_WIDTH:(br + 1) * KV_WIDTH]
    for br in range(2):
        kvtb_ref[br, 0] = kvt[(br + 1) * KV_WIDTH:(br + 2) * KV_WIDTH].astype(BF16)
    k = 0
    if emit_tok_kv:
        base = ATTN_WIDTH + POOL_WIDTH + N_GATE_PAD
        rest[k][0] = tok[:, base:base + 3 * KV_WIDTH]
        lane = lax.broadcasted_iota(jnp.int32, (1, PAGE_SIZE), 1)
        kvp = kvt if tm >= PAGE_SIZE else jnp.concatenate(
            [kvt, jnp.zeros((3 * KV_WIDTH, PAGE_SIZE - tm), F32)], axis=1)
        for s in range(tm // seq_len):
            grp, off = divmod(s * seq_len, PAGE_SIZE)
            blk = kvp[:, grp * PAGE_SIZE:(grp + 1) * PAGE_SIZE]
            if off:
                blk = pltpu.roll(blk, PAGE_SIZE - off, 1)
            blk = jnp.where(lane < seq_len, blk, 0.0)
            for br in range(3):
                rest[k + 1][br, s] = blk[br * KV_WIDTH:(br + 1) * KV_WIDTH]
        k += 2
    if emit_pages:
        for pg in range(tm // PAGE_SIZE):
            rest[k][pg] = kvt[:KV_WIDTH, pg * PAGE_SIZE:(pg + 1) * PAGE_SIZE]


def _proj(x, g_attn, wtok, wkv, *, tm, emit_tok_kv, emit_pages, seq_len=None):
    B, T, _ = x.shape
    nt = T // tm
    out_shape = [
        jax.ShapeDtypeStruct((B, T, ATTN_WIDTH), BF16),
        jax.ShapeDtypeStruct((B, T, N_GATE_PAD), F32),
        jax.ShapeDtypeStruct((B, T, POOL_WIDTH), F32),
    ] + [jax.ShapeDtypeStruct((B, KV_WIDTH, T), F32)] * 3 + [
        jax.ShapeDtypeStruct((2, B, KV_WIDTH, T), BF16),
    ]
    out_specs = [
        pl.BlockSpec((1, tm, ATTN_WIDTH), lambda b, i: (b, i, 0)),
        pl.BlockSpec((1, tm, N_GATE_PAD), lambda b, i: (b, i, 0)),
        pl.BlockSpec((1, tm, POOL_WIDTH), lambda b, i: (b, i, 0)),
    ] + [pl.BlockSpec((1, KV_WIDTH, tm), lambda b, i: (b, 0, i))] * 3 + [
        pl.BlockSpec((2, 1, KV_WIDTH, tm), lambda b, i: (0, b, 0, i)),
    ]
    if emit_tok_kv:
        out_shape.append(jax.ShapeDtypeStruct((B, T, 3 * KV_WIDTH), F32))
        out_specs.append(pl.BlockSpec((1, tm, 3 * KV_WIDTH), lambda b, i: (b, i, 0)))
        assert B == 1 and nt == 1 and tm % seq_len == 0
        out_shape.append(jax.ShapeDtypeStruct((3, tm // seq_len, KV_WIDTH, PAGE_SIZE), F32))
        out_specs.append(pl.BlockSpec((3, tm // seq_len, KV_WIDTH, PAGE_SIZE), lambda b, i: (0, 0, 0, 0)))
    if emit_pages:
        ppt = tm // PAGE_SIZE
        out_shape.append(jax.ShapeDtypeStruct((B * T // PAGE_SIZE, KV_WIDTH, PAGE_SIZE), F32))
        out_specs.append(pl.BlockSpec((ppt, KV_WIDTH, PAGE_SIZE), lambda b, i: (b * nt + i, 0, 0)))
    return pl.pallas_call(
        functools.partial(_proj_kernel, tm=tm, emit_tok_kv=emit_tok_kv, emit_pages=emit_pages, seq_len=seq_len),
        grid=(B, nt),
        in_specs=[
            pl.BlockSpec((1, tm, D_MODEL), lambda b, i: (b, i, 0)),
            pl.BlockSpec((1, D_MODEL), lambda b, i: (0, 0)),
            pl.BlockSpec(wtok.shape, lambda b, i: (0, 0)),
            pl.BlockSpec(wkv.shape, lambda b, i: (0, 0)),
        ],
        out_specs=out_specs,
        out_shape=out_shape,
        compiler_params=_cparams(("arbitrary", "arbitrary"), V7X_VMEM_LIMIT),
        name="proj",
    )(x, g_attn, wtok, wkv)


def _gelu_tanh(x):
    c = math.sqrt(2.0 / math.pi)
    return 0.5 * x * (1.0 + jnp.tanh(c * (x + 0.044715 * (x * x * x))))


def _compress_kernel(ids_ref, src_ref, pet_ref, w1_ref, w2_ref, out_ref, buf, sem, *, P, n_steps):
    i = pl.program_id(0)

    def page_copy(step, slot, p):
        return pltpu.make_async_copy(src_ref.at[ids_ref[step * P + p]], buf.at[slot, :, p, :], sem.at[slot])

    def start(step, slot):
        for p in range(P):
            page_copy(step, slot, p).start()

    @pl.when(i == 0)
    def _():
        start(0, 0)

    @pl.when(i + 1 < n_steps)
    def _():
        start(i + 1, (i + 1) % 2)

    slot = i % 2
    for p in range(P):
        page_copy(i, slot, p).wait()

    bref = buf.at[slot]
    for c in range(2):
        acc = jnp.zeros((N_KV * P, 2 * CMP_HIDDEN), F32)
        for dp in range(HEAD_DIM // 2):
            rows = []
            for g in range(N_KV):
                halves = []
                for dd in range(2):
                    r = g * 2 * HEAD_DIM + c * HEAD_DIM + 2 * dp + dd
                    pr = c * HEAD_DIM + 2 * dp + dd
                    halves.append(bref[r] + pet_ref[pr:pr + 1, :])
                rows.append(jnp.concatenate(halves, axis=1))
            lhs = jnp.concatenate(rows, axis=0).astype(BF16)
            zero = jnp.zeros((BLK, CMP_HIDDEN), BF16)
            w_rows = []
            for dd in range(2):
                w = w1_ref[c, dp, dd]
                w_rows += [jnp.concatenate([w, zero], axis=1), jnp.concatenate([zero, w], axis=1)]
            acc = acc + jnp.dot(lhs, jnp.concatenate(w_rows, axis=0), preferred_element_type=F32)
        hid = _gelu_tanh(acc).astype(BF16)
        oc = jnp.dot(hid, w2_ref[c], preferred_element_type=F32).astype(BF16)
        for g in range(N_KV):
            out_ref[g, c] = oc[g * P:(g + 1) * P]


def _compress(ids, src, pet, w1bd, w2bd, *, P):
    n_total = ids.shape[0]
    n_steps = n_total // P
    grid_spec = pltpu.PrefetchScalarGridSpec(
        num_scalar_prefetch=1,
        grid=(n_steps,),
        in_specs=[
            pl.BlockSpec(memory_space=pl.ANY),
            pl.BlockSpec(pet.shape, lambda i, ids: (0, 0)),
            pl.BlockSpec(w1bd.shape, lambda i, ids: (0, 0, 0, 0, 0), pipeline_mode=pl.Buffered(1)),
            pl.BlockSpec(w2bd.shape, lambda i, ids: (0, 0, 0)),
        ],
        out_specs=pl.BlockSpec((N_KV, 2, P, PAGE_SIZE), lambda i, ids: (0, 0, i, 0)),
        scratch_shapes=[
            pltpu.VMEM((2, KV_WIDTH, P, PAGE_SIZE), F32),
            pltpu.SemaphoreType.DMA((2,)),
        ],
    )
    return pl.pallas_call(
        functools.partial(_compress_kernel, P=P, n_steps=n_steps),
        grid_spec=grid_spec,
        out_shape=jax.ShapeDtypeStruct((N_KV, 2, n_total, PAGE_SIZE), BF16),
        compiler_params=_cparams(("arbitrary",), V7X_VMEM_LIMIT),
        name="compress",
    )(ids, src, pet, w1bd, w2bd)


def _compress_weights(cmp_pe, cmp_w1, cmp_w2):
    eye = jnp.eye(BLOCKS_PER_PAGE, dtype=F32)
    pet = jnp.tile(jnp.transpose(cmp_pe, (1, 2, 0)).reshape(2 * HEAD_DIM, BLK), (1, BLOCKS_PER_PAGE))
    w1 = cmp_w1.reshape(2, BLK, HEAD_DIM // 2, 2, CMP_HIDDEN)
    w1bd = jnp.transpose(w1, (0, 2, 3, 1, 4)).astype(BF16)
    w2bd = jnp.einsum('chd,jk->cjhkd', cmp_w2, eye).reshape(
        2, BLOCKS_PER_PAGE * CMP_HIDDEN, BLOCKS_PER_PAGE * HEAD_DIM).astype(BF16)
    return pet, w1bd, w2bd


def _rank_select_t(score, nblk, n_select):
    tq, W = score.shape
    x = score.T
    groups = [x[8 * k:8 * k + 8] for k in range(nblk // 8)]
    ranks = [jnp.zeros((8, tq), jnp.int32) for _ in groups]
    row = lax.broadcasted_iota(jnp.int32, (8, tq), 0)
    for i in range(nblk):
        xi = groups[i // 8][i % 8:i % 8 + 1, :]
        for k, xg in enumerate(groups):
            if 8 * k + 7 < i:
                ahead = xi > xg
            elif 8 * k > i:
                ahead = xi >= xg
            else:
                ahead = (xi > xg) | ((xi == xg) & (row > i - 8 * k))
            ranks[k] = ranks[k] + jnp.where(ahead, 1, 0)
    sel = [jnp.where(r < n_select, 1.0, 0.0) for r in ranks]
    if nblk < W:
        sel.append(jnp.zeros((W - nblk, tq), F32))
    return jnp.concatenate(sel, axis=0).T


def _key_consts(T, npg):
    assert T <= 256 * BLK and npg * BLOCKS_PER_PAGE <= HEAD_DIM
    k = jnp.arange(T, dtype=jnp.int32)[None, :]
    r = jnp.arange(HEAD_DIM, dtype=jnp.int32)[:, None]
    pos = jnp.where(r == 0, k // BLK, jnp.where(r == 1, k % BLK, 0)).astype(F32)
    en = jnp.where(k // BLK == r, NEG, 0.0)
    return jnp.concatenate([pos, en, jnp.zeros((HEAD_DIM, T), F32)], axis=0).astype(BF16)


M_INIT = -3e38


def _p_attn_kernel(q_ref, gate_ref, comp_ref, ks_ref, kw_ref, kc_ref, o_ref,
                   qx, m_ref, acc_ref, out_ref, flag_v, flag_s, flag_sem, *, tq, tk, npg, row_chunk):
    qi = pl.program_id(1)
    q0 = qi * tq
    pos_t = q0 + lax.broadcasted_iota(jnp.int32, (tq, 1), 0)
    nblk = npg * BLOCKS_PER_PAGE
    W = 128
    R = HPG * tq
    lane_h = lax.broadcasted_iota(jnp.int32, (1, HEAD_DIM), 1)

    for hh in range(N_HEADS):
        g, h = divmod(hh, HPG)
        rows = pl.ds(h * tq, tq)
        qx[g, rows, 0:HEAD_DIM] = q_ref[0, :, hh * HEAD_DIM:(hh + 1) * HEAD_DIM]
        posc = jnp.where(lane_h == 0, _slope(hh) * BLK, jnp.where(lane_h == 1, _slope(hh), 0.0))
        qx[g, rows, HEAD_DIM:2 * HEAD_DIM] = jnp.broadcast_to(posc, (tq, HEAD_DIM)).astype(BF16)
        qx[g, rows, 3 * HEAD_DIM:4 * HEAD_DIM] = jnp.zeros((tq, HEAD_DIM), BF16)

    blk_n = lax.broadcasted_iota(jnp.int32, (1, W), 1)
    first = blk_n < nblk
    end_pos = (blk_n + 1) * BLK - 1
    dist_c = (pos_t - end_pos).astype(F32)
    mask_c = (dist_c >= 0) & first
    cur = pos_t // BLK
    row_w = lax.broadcasted_iota(jnp.int32, (W, W), 0)
    col_w = lax.broadcasted_iota(jnp.int32, (W, W), 1)
    perm = jnp.where((col_w == (row_w % 2) * npg + row_w // 2) & (row_w < nblk), 1.0, 0.0).astype(BF16)

    used = []
    for g in range(N_KV):
        ck = comp_ref[g, 0].astype(F32)
        cv = comp_ref[g, 1].astype(F32)
        pad = [jnp.zeros((W - nblk, HEAD_DIM), F32)] if nblk < W else []
        ckp = jnp.concatenate([ck[:, :HEAD_DIM], ck[:, HEAD_DIM:]] + pad, axis=0).astype(BF16)
        cvp = jnp.concatenate([cv[:, :HEAD_DIM], cv[:, HEAD_DIM:]] + pad, axis=0).astype(BF16)
        ckp = jnp.dot(perm, ckp, preferred_element_type=F32).astype(BF16)
        cvp = jnp.dot(perm, cvp, preferred_element_type=F32).astype(BF16)
        imp = jnp.zeros((tq, W), F32)
        for h in range(HPG):
            hh = g * HPG + h
            qh = qx[g, pl.ds(h * tq, tq), 0:HEAD_DIM]
            s = lax.dot_general(qh, ckp, NT_DIMS, preferred_element_type=F32)
            s = jnp.where(mask_c, s - _slope(hh) * dist_c, NEG)
            mx = jnp.max(s, axis=1, keepdims=True)
            e = jnp.where(mask_c, jnp.exp(s - mx), 0.0)
            p = e / jnp.maximum(jnp.sum(e, axis=1, keepdims=True), 1e-30)
            imp = imp + p
            oc = jnp.dot(p.astype(BF16), cvp, preferred_element_type=F32)
            gcol = gate_ref[0, :, hh:hh + 1]
            out_ref[:, hh * HEAD_DIM:(hh + 1) * HEAD_DIM] = gcol * oc
        forced = (blk_n == 0) | (blk_n == cur) | (blk_n == cur - 1)
        score = jnp.where(forced, FORCE_SCORE, imp)
        score = jnp.where(blk_n > cur, -1.0, score)
        sel = _rank_select_t(score, nblk, N_SELECT)
        picked = (sel > 0.5) & (blk_n <= cur)
        notsel = jnp.where(picked, 0.0, 1.0).astype(BF16)
        for h in range(HPG):
            qx[g, pl.ds(h * tq, tq), 2 * HEAD_DIM:3 * HEAD_DIM] = notsel[:, :HEAD_DIM]
        used.append(jnp.max(jnp.where(picked, 1, 0), axis=0, keepdims=True))

    flag_v[...] = jnp.concatenate(used + [jnp.zeros((8 - N_KV, W), jnp.int32)], axis=0)
    flag_copy = pltpu.make_async_copy(flag_v, flag_s, flag_sem.at[0])
    flag_copy.start()

    pos_rows = q0 + lax.broadcasted_iota(jnp.int32, (R, 1), 0) % tq

    ones_rows = jnp.where(lax.broadcasted_iota(jnp.int32, (HEAD_DIM, 1), 0) == 0, 1.0, 0.0)

    def attend_tile(kv_ref, k0, width, kc_rows, mode, groups=tuple(range(N_KV))):
        kpos = k0 + lax.broadcasted_iota(jnp.int32, (1, width), 1)
        kc_tile = kc_ref[0:kc_rows, pl.ds(k0, width)]
        ones_tile = jnp.broadcast_to(ones_rows, (HEAD_DIM, width)).astype(BF16)
        if mode == "causal":
            keep = kpos <= pos_rows
        elif mode == "recent":
            keep = kpos > pos_rows - WINDOW
        for g in groups:
            kt = kv_ref[0, 0, g * 2 * HEAD_DIM:g * 2 * HEAD_DIM + HEAD_DIM, pl.ds(k0, width)]
            kext = jnp.concatenate([kt, kc_tile], axis=0)
            vt = kv_ref[0, 0, g * 2 * HEAD_DIM + HEAD_DIM:(g + 1) * 2 * HEAD_DIM, pl.ds(k0, width)]
            vext = jnp.concatenate([vt, ones_tile], axis=0)
            for c in range(R // row_chunk):
                rc = pl.ds(c * row_chunk, row_chunk)
                s = jnp.dot(qx[g, rc, 0:HEAD_DIM + kc_rows], kext, preferred_element_type=F32)
                if mode != "full":
                    s = jnp.where(keep[c * row_chunk:(c + 1) * row_chunk], s, NEG)
                m_prev = m_ref[g, rc, :]
                m_next = jnp.maximum(m_prev, jnp.max(s, axis=1, keepdims=True))
                alpha = jnp.exp(m_prev - m_next)
                m_ref[g, rc, :] = m_next
                p = jnp.exp(s - jnp.concatenate([m_next] * (width // W), axis=1)).astype(BF16)
                pv = lax.dot_general(p, vext, NT_DIMS, preferred_element_type=F32)
                acc_ref[g, rc, :] = alpha * acc_ref[g, rc, :] + pv

    def reset():
        for g in range(N_KV):
            m_ref[g] = jnp.full((R, W), M_INIT, F32)
            acc_ref[g] = jnp.zeros((R, W), F32)

    def finish(gate_base):
        for g in range(N_KV):
            for h in range(HPG):
                hh = g * HPG + h
                a = acc_ref[g, pl.ds(h * tq, tq), :]
                o = a[:, :HEAD_DIM] / jnp.maximum(a[:, HEAD_DIM:HEAD_DIM + 1], 1e-30)
                gcol = gate_ref[0, :, gate_base + hh:gate_base + hh + 1]
                out_ref[:, hh * HEAD_DIM:(hh + 1) * HEAD_DIM] += gcol * o

    reset()
    wt = WINDOW // tq

    @pl.when(qi >= wt)
    def _():
        attend_tile(kw_ref, pl.multiple_of((qi - wt) * tq, tq), WINDOW, HEAD_DIM, "recent")

    @pl.when(qi < wt)
    def _():
        def win_body(j, carry):
            attend_tile(kw_ref, pl.multiple_of(j * tq, tq), tq, HEAD_DIM, "full")
            return carry

        lax.fori_loop(0, qi, win_body, 0)

    attend_tile(kw_ref, pl.multiple_of(q0, tq), tq, HEAD_DIM, "causal")
    finish(2 * N_HEADS)

    reset()
    flag_copy.wait()
    n_sel = (q0 + tq + tk - 1) // tk
    bpt = tk // BLK

    def sel_body(j, carry):
        for g in range(N_KV):
            hit = flag_s[g, j * bpt]
            for u in range(1, bpt):
                hit = hit | flag_s[g, j * bpt + u]

            @pl.when(hit != 0)
            def _():
                attend_tile(ks_ref, pl.multiple_of(j * tk, tk), tk, 3 * HEAD_DIM, "full", groups=(g,))

        return carry

    lax.fori_loop(0, n_sel - 1, sel_body, 0)
    attend_tile(ks_ref, pl.multiple_of((n_sel - 1) * tk, tk), tk, 3 * HEAD_DIM, "causal")
    finish(N_HEADS)
    o_ref[0] = out_ref[...].astype(BF16)


def _p_attn(q, gates, comp, kvtb, kconst, *, tq, tk):
    B, T, _ = q.shape
    npg = T // PAGE_SIZE
    assert WINDOW % tq == 0 and tk % tq == 0
    return pl.pallas_call(
        functools.partial(_p_attn_kernel, tq=tq, tk=tk, npg=npg, row_chunk=min(1024, HPG * tq)),
        grid=(B, T // tq),
        in_specs=[
            pl.BlockSpec((1, tq, ATTN_WIDTH), lambda b, i: (b, i, 0)),
            pl.BlockSpec((1, tq, N_GATE_PAD), lambda b, i: (b, i, 0)),
            pl.BlockSpec((N_KV, 2, npg, PAGE_SIZE), lambda b, i: (0, 0, b, 0)),
            pl.BlockSpec((1, 1, KV_WIDTH, T), lambda b, i: (0, b, 0, 0)),
            pl.BlockSpec((1, 1, KV_WIDTH, T), lambda b, i: (1, b, 0, 0)),
            pl.BlockSpec(kconst.shape, lambda b, i: (0, 0)),
        ],
        out_specs=pl.BlockSpec((1, tq, ATTN_WIDTH), lambda b, i: (b, i, 0)),
        out_shape=jax.ShapeDtypeStruct((B, T, ATTN_WIDTH), BF16),
        scratch_shapes=[
            pltpu.VMEM((N_KV, HPG * tq, 4 * HEAD_DIM), BF16),
            pltpu.VMEM((N_KV, HPG * tq, 128), F32),
            pltpu.VMEM((N_KV, HPG * tq, 128), F32),
            pltpu.VMEM((tq, ATTN_WIDTH), F32),
            pltpu.VMEM((8, 128), jnp.int32),
            pltpu.SMEM((8, 128), jnp.int32),
            pltpu.SemaphoreType.DMA((1,)),
        ],
        compiler_params=_cparams(("arbitrary", "arbitrary"), V7X_VMEM_LIMIT),
        name="p_attn",
    )(q, gates, comp, kvtb, kvtb, kconst)


def _stack_heads(q_ref, g, ts):
    parts = [q_ref[0, :, (g * HPG + h) * HEAD_DIM:(g * HPG + h + 1) * HEAD_DIM].astype(F32) for h in range(HPG)]
    return jnp.concatenate(parts, axis=0).astype(BF16)


def _row_consts(g, ts):
    R = HPG * ts
    row = lax.broadcasted_iota(jnp.int32, (R, 1), 0)
    t_row = row % ts
    h_row = row // ts
    slope = jnp.zeros((R, 1), F32)
    for h in range(HPG):
        slope = jnp.where(h_row == h, _slope(g * HPG + h), slope)
    return t_row, slope


def _s_select_kernel(q_ref, gate_ref, comp_ref, cnew_ref, oc_ref, bits_ref, out_s, *, past, ts):
    W = 128
    nb_past = past // BLK
    lane = lax.broadcasted_iota(jnp.int32, (1, W), 1)
    idx_fns = [lambda c: 2 * c, lambda c: 2 * c + 1, lambda c: jnp.where(c == 0, nb_past, (1 << 20) + c)]
    idx_tiles = [f(lane) for f in idx_fns]
    valid_n = lane == 0
    tok = lax.broadcasted_iota(jnp.int32, (ts, 1), 0)
    cur_t = (past + tok) // BLK
    for g in range(N_KV):
        qg = _stack_heads(q_ref, g, ts)
        t_row, slope = _row_consts(g, ts)
        pos = past + t_row
        ck = comp_ref[g, 0]
        cv = comp_ref[g, 1]
        cn = cnew_ref[g, 0, 0].astype(F32)[:, :HEAD_DIM]
        vn = cnew_ref[g, 1, 0].astype(F32)[:, :HEAD_DIM]
        s_t = [lax.dot_general(qg, ck[:, :HEAD_DIM], NT_DIMS, preferred_element_type=F32),
               lax.dot_general(qg, ck[:, HEAD_DIM:], NT_DIMS, preferred_element_type=F32),
               jnp.broadcast_to(jnp.sum(qg.astype(F32) * cn, axis=1, keepdims=True), (HPG * ts, W))]
        masks, es = [], []
        for k in range(3):
            end_pos = (idx_tiles[k] + 1) * BLK - 1
            dist = (pos - end_pos).astype(F32)
            mk = dist >= 0
            if k == 2:
                mk = mk & valid_n
            masks.append(mk)
            s_t[k] = jnp.where(mk, s_t[k] - slope * dist, NEG)
        mx = jnp.maximum(jnp.maximum(jnp.max(s_t[0], axis=1, keepdims=True),
                                     jnp.max(s_t[1], axis=1, keepdims=True)),
                         jnp.max(s_t[2], axis=1, keepdims=True))
        for k in range(3):
            es.append(jnp.where(masks[k], jnp.exp(s_t[k] - mx), 0.0))
        den = (jnp.sum(es[0], axis=1, keepdims=True) + jnp.sum(es[1], axis=1, keepdims=True)
               + jnp.sum(es[2], axis=1, keepdims=True))
        inv = 1.0 / jnp.maximum(den, 1e-30)
        ps = [e * inv for e in es]
        o_c = (jnp.dot(ps[0].astype(BF16), cv[:, :HEAD_DIM], preferred_element_type=F32)
               + jnp.dot(ps[1].astype(BF16), cv[:, HEAD_DIM:], preferred_element_type=F32)
               + ps[2][:, 0:1].astype(BF16).astype(F32) * vn)
        for h in range(HPG):
            hh = g * HPG + h
            gcol = gate_ref[0, :, hh:hh + 1]
            out_s[:, hh * HEAD_DIM:(hh + 1) * HEAD_DIM] = gcol * o_c[h * ts:(h + 1) * ts]
        scores = []
        for k in range(3):
            imp = ps[k][0:ts]
            for h in range(1, HPG):
                imp = imp + ps[k][h * ts:(h + 1) * ts]
            idx = idx_tiles[k]
            forced = (idx == 0) | (idx == cur_t) | (idx == cur_t - 1)
            sc = jnp.where(forced, FORCE_SCORE, imp)
            sc = jnp.where(idx > cur_t, -1.0, sc)
            if k == 2:
                sc = jnp.where(valid_n, sc, -2.0)
            scores.append(sc)
        s_new = scores[2][:, 0:1]
        parts = [[jnp.where(s_new > scores[ka], 1, 0), jnp.zeros((ts, W), jnp.int32)] for ka in range(2)]
        for kb in range(2):
            for r in range(W):
                y = pltpu.roll(scores[kb], r, 1) if r else scores[kb]
                yi = idx_fns[kb]((lane - r) & (W - 1))
                for ka in range(2):
                    if ka == kb and r == 0:
                        continue
                    ahead = (y > scores[ka]) | ((y == scores[ka]) & (yi < idx_tiles[ka]))
                    parts[ka][r % 2] = parts[ka][r % 2] + jnp.where(ahead, 1, 0)
        sel = [(parts[k][0] + parts[k][1]) < N_SELECT for k in range(2)]
        rank_new = (jnp.sum(jnp.where(scores[0] >= s_new, 1, 0), axis=1, keepdims=True)
                    + jnp.sum(jnp.where(scores[1] >= s_new, 1, 0), axis=1, keepdims=True))
        wt = jnp.left_shift(1, 2 * tok)
        page_bits = jnp.sum(jnp.where(sel[0], wt, 0) + jnp.where(sel[1], 2 * wt, 0), axis=0, keepdims=True)
        new_bits = jnp.sum(jnp.where((rank_new < N_SELECT) & valid_n, jnp.left_shift(1, tok), 0),
                           axis=0, keepdims=True)
        bits_ref[0, g, 0:1, :] = page_bits
        bits_ref[0, g, 1:2, :] = new_bits
    oc_ref[0] = out_s[...]


def _s_select(q, gates, comp, cnew, *, past):
    Bs, ts, _ = q.shape
    npg = past // PAGE_SIZE
    assert npg == 128, "one lane tile of pages per sequence"
    return pl.pallas_call(
        functools.partial(_s_select_kernel, past=past, ts=ts),
        grid=(Bs,),
        in_specs=[
            pl.BlockSpec((1, ts, ATTN_WIDTH), lambda b: (b, 0, 0)),
            pl.BlockSpec((1, ts, N_GATE_PAD), lambda b: (b, 0, 0)),
            pl.BlockSpec((N_KV, 2, npg, PAGE_SIZE), lambda b: (0, 0, b, 0)),
            pl.BlockSpec((N_KV, 2, 1, 1, PAGE_SIZE), lambda b: (0, 0, b, 0, 0)),
        ],
        out_specs=[
            pl.BlockSpec((1, ts, ATTN_WIDTH), lambda b: (b, 0, 0)),
            pl.BlockSpec((1, N_KV, 2, 128), lambda b: (b, 0, 0, 0)),
        ],
        out_shape=[
            jax.ShapeDtypeStruct((Bs, ts, ATTN_WIDTH), F32),
            jax.ShapeDtypeStruct((Bs, N_KV, 2, 128), jnp.int32),
        ],
        scratch_shapes=[pltpu.VMEM((ts, ATTN_WIDTH), F32)],
        compiler_params=_cparams(("arbitrary",), V7X_VMEM_LIMIT),
        name="s_select",
    )(q, gates, comp, cnew)


def _s_attn_kernel(pt_ref, bits_ref, q_ref, gate_ref, oc_ref, win_ref, knew_ref, slc_ref,
                   o_ref, wout_ref, buf, sem, plist, cnts, out_s, *, past, ts, npg, chunk):
    b = pl.program_id(0)
    g = pl.program_id(1)
    W = 128
    R = HPG * ts
    wk = win_ref.shape[2]
    lin = b * N_KV + g
    n_lin = pl.num_programs(0) * N_KV
    base = lin * 2 * W
    slot = lin % 2

    def page_copy(page, gg, sl, idx):
        return pltpu.make_async_copy(slc_ref.at[page, pl.ds(gg * 2 * HEAD_DIM, 2 * HEAD_DIM)],
                                     buf.at[sl, idx], sem.at[sl])

    def issue_all(ln, sl):
        bb = ln // N_KV
        gg = ln % N_KV

        def scan(p, cnt):
            plist[sl * npg + cnt] = p
            return cnt + (bits_ref[ln * 2 * W + p] != 0).astype(jnp.int32)

        cnt = lax.fori_loop(0, npg, scan, 0, unroll=8)
        cnts[sl] = cnt

        def start(s, c):
            page_copy(pt_ref[bb * npg + plist[sl * npg + s]], gg, sl, s).start()
            return c

        lax.fori_loop(0, cnt, start, 0)

    @pl.when(lin == 0)
    def _():
        buf[...] = jnp.zeros(buf.shape, F32)
        for k in range(2 * npg):
            plist[k] = 0
        issue_all(lin, slot)

    @pl.when(lin + 1 < n_lin)
    def _():
        issue_all(lin + 1, 1 - slot)

    cnt = cnts[slot]

    q_all = q_ref[0].astype(F32)
    qg = jnp.concatenate([q_all[:, h * HEAD_DIM:(h + 1) * HEAD_DIM] for h in range(HPG)], axis=0).astype(BF16)
    row = lax.broadcasted_iota(jnp.int32, (R, 1), 0)
    t_row = row % ts
    head_row = row // ts + g * HPG
    slope = jnp.zeros((R, 1), F32)
    for hh in range(N_HEADS):
        slope = jnp.where(head_row == hh, _slope(hh), slope)
    lane = lax.broadcasted_iota(jnp.int32, (1, W), 1)

    kw = win_ref[0, 0:HEAD_DIM, :].astype(BF16)
    vw = win_ref[0, HEAD_DIM:2 * HEAD_DIM, :].astype(BF16)
    kn = knew_ref[2, 0, 0:HEAD_DIM, :].astype(BF16)
    vn = knew_ref[2, 0, HEAD_DIM:2 * HEAD_DIM, :].astype(BF16)
    i_st = lax.broadcasted_iota(jnp.int32, (1, wk), 1)
    d_st = t_row + wk - i_st
    m_st = (d_st >= 0) & (d_st < WINDOW)
    d_nw = t_row - lane
    m_nw = (d_nw >= 0) & (d_nw < WINDOW) & (lane < ts)
    s_st = jnp.where(m_st, jnp.dot(qg, kw, preferred_element_type=F32) - slope * d_st.astype(F32), NEG)
    s_nw = jnp.where(m_nw, jnp.dot(qg, kn, preferred_element_type=F32) - slope * d_nw.astype(F32), NEG)
    mx = jnp.maximum(jnp.max(s_st, axis=1, keepdims=True), jnp.max(s_nw, axis=1, keepdims=True))
    e_st = jnp.where(m_st, jnp.exp(s_st - mx), 0.0)
    e_nw = jnp.where(m_nw, jnp.exp(s_nw - mx), 0.0)
    den = jnp.sum(e_st, axis=1, keepdims=True) + jnp.sum(e_nw, axis=1, keepdims=True)
    inv = 1.0 / jnp.maximum(den, 1e-30)
    o_w = (lax.dot_general((e_st * inv).astype(BF16), vw, NT_DIMS, preferred_element_type=F32)
           + lax.dot_general((e_nw * inv).astype(BF16), vn, NT_DIMS, preferred_element_type=F32))

    last = wk - W
    shifted = pltpu.roll(win_ref[0], wk - ts, 1)
    newr = pltpu.roll(knew_ref[2, 0], W - ts, 1)
    wout_ref[0, :, 0:last] = shifted[:, 0:last]
    wout_ref[0, :, last:wk] = jnp.where(lane >= W - ts, newr, shifted[:, last:wk])

    bits_new = bits_ref[base + W]
    ksn = knew_ref[1, 0, 0:HEAD_DIM, :].astype(BF16)
    vsn = knew_ref[1, 0, HEAD_DIM:2 * HEAD_DIM, :].astype(BF16)
    d_sn = t_row - lane
    m_sn = (d_sn >= 0) & (lane < ts) & ((jnp.right_shift(bits_new, t_row) & 1) == 1)
    s_sn = jnp.where(m_sn, jnp.dot(qg, ksn, preferred_element_type=F32) - slope * d_sn.astype(F32), NEG)
    m0 = jnp.max(s_sn, axis=1, keepdims=True)
    e0 = jnp.where(m_sn, jnp.exp(s_sn - m0), 0.0)
    l0 = jnp.sum(e0, axis=1, keepdims=True)
    a0 = lax.dot_general(e0.astype(BF16), vsn, NT_DIMS, preferred_element_type=F32)

    def wait_one(s, c):
        page_copy(0, 0, slot, s).wait()
        return c

    lax.fori_loop(0, cnt, wait_one, 0)
    sh = 2 * t_row + (lane >= BLK).astype(jnp.int32)
    sh_c = jnp.concatenate([sh] * chunk, axis=1)
    bslot = buf.at[slot]

    def chunk_step(c, carry):
        m, l, acc = carry
        kts, vts, kpos, bitv = [], [], [], []
        for u in range(chunk):
            s = c * chunk + u
            ok = s < cnt
            p = jnp.where(ok, plist[slot * npg + s], 0)
            bits = jnp.where(ok, bits_ref[base + p], 0)
            kts.append(bslot[s, 0:HEAD_DIM, :])
            vts.append(bslot[s, HEAD_DIM:2 * HEAD_DIM, :])
            kpos.append(p * PAGE_SIZE + lane)
            bitv.append(jnp.zeros((1, W), jnp.int32) + bits)
        kt = jnp.concatenate(kts, axis=1).astype(BF16)
        vt = jnp.concatenate(vts, axis=1).astype(BF16)
        dist = past + t_row - jnp.concatenate(kpos, axis=1)
        mask = ((jnp.right_shift(jnp.concatenate(bitv, axis=1), sh_c) & 1) == 1) & (dist >= 0)
        sc = jnp.where(mask, jnp.dot(qg, kt, preferred_element_type=F32) - slope * dist.astype(F32), NEG)
        m_new = jnp.maximum(m, jnp.max(sc, axis=1, keepdims=True))
        alpha = jnp.exp(m - m_new)
        e = jnp.where(mask, jnp.exp(sc - m_new), 0.0)
        l = alpha * l + jnp.sum(e, axis=1, keepdims=True)
        acc = alpha * acc + lax.dot_general(e.astype(BF16), vt, NT_DIMS, preferred_element_type=F32)
        return m_new, l, acc

    m, l, acc = lax.fori_loop(0, (cnt + chunk - 1) // chunk, chunk_step, (m0, l0, a0))
    o_s = acc / jnp.maximum(l, 1e-30)

    g_vec = jnp.zeros((ts, 1), jnp.int32) + g
    for h in range(HPG):
        rows = slice(h * ts, (h + 1) * ts)
        g_s = jnp.zeros((ts, 1), F32)
        g_w = jnp.zeros((ts, 1), F32)
        for gg in range(N_KV):
            hh = gg * HPG + h
            g_s = jnp.where(g_vec == gg, gate_ref[0, :, N_HEADS + hh:N_HEADS + hh + 1], g_s)
            g_w = jnp.where(g_vec == gg, gate_ref[0, :, 2 * N_HEADS + hh:2 * N_HEADS + hh + 1], g_w)
        out_s[:, h * HEAD_DIM:(h + 1) * HEAD_DIM] = g_s * o_s[rows] + g_w * o_w[rows]
    o_ref[0] = oc_ref[0] + out_s[...]


def _s_attn(page_table, bits, q, gates, oc, win_t, knew, slc_t, *, past):
    Bs, ts, _ = q.shape
    npg = past // PAGE_SIZE
    wk = win_t.shape[2]
    gw = HPG * HEAD_DIM
    grid_spec = pltpu.PrefetchScalarGridSpec(
        num_scalar_prefetch=2,
        grid=(Bs, N_KV),
        in_specs=[
            pl.BlockSpec((1, ts, gw), lambda b, g, *_: (b, 0, g)),
            pl.BlockSpec((1, ts, N_GATE_PAD), lambda b, g, *_: (b, 0, 0)),
            pl.BlockSpec((1, ts, gw), lambda b, g, *_: (b, 0, g)),
            pl.BlockSpec((1, 2 * HEAD_DIM, wk), lambda b, g, *_: (b, g, 0)),
            pl.BlockSpec((3, 1, 2 * HEAD_DIM, PAGE_SIZE), lambda b, g, *_: (0, b, g, 0)),
            pl.BlockSpec(memory_space=pl.ANY),
        ],
        out_specs=[
            pl.BlockSpec((1, ts, gw), lambda b, g, *_: (b, 0, g)),
            pl.BlockSpec((1, 2 * HEAD_DIM, wk), lambda b, g, *_: (b, g, 0)),
        ],
        scratch_shapes=[
            pltpu.VMEM((2, npg, 2 * HEAD_DIM, PAGE_SIZE), F32),
            pltpu.SemaphoreType.DMA((2,)),
            pltpu.SMEM((2 * npg,), jnp.int32),
            pltpu.SMEM((2,), jnp.int32),
            pltpu.VMEM((ts, gw), F32),
        ],
    )
    chunk = 8
    assert npg % chunk == 0
    return pl.pallas_call(
        functools.partial(_s_attn_kernel, past=past, ts=ts, npg=npg, chunk=chunk),
        grid_spec=grid_spec,
        out_shape=[
            jax.ShapeDtypeStruct((Bs, ts, ATTN_WIDTH), F32),
            jax.ShapeDtypeStruct(win_t.shape, F32),
        ],
        compiler_params=_cparams(("arbitrary", "arbitrary"), V7X_VMEM_LIMIT),
        name="s_attn",
    )(page_table.reshape(-1), bits.reshape(-1), q, gates, oc, win_t, knew, slc_t)


def _pool_kernel(u_ref, halo_ref, wp_ref, ps_ref, o_ref, ext, *, nb, tp, tiles_per_seq, pos_base):
    i = pl.program_id(0)
    ext[:, 0:HALO, :] = halo_ref[...]
    ext[:, HALO:HALO + tp, :] = u_ref[...]
    tile = i * nb + lax.broadcasted_iota(jnp.int32, (nb, tp, 1), 0)
    pos = pos_base + (tile % tiles_per_seq) * tp + lax.broadcasted_iota(jnp.int32, (nb, tp, 1), 1)
    for gi, w in enumerate(POOL_WINDOWS):
        lanes = slice(gi * POOL_GROUP, (gi + 1) * POOL_GROUP)
        own = ext[:, HALO:HALO + tp, lanes]
        tot = ext[:, :, lanes]
        span = 1
        while span < w:
            tot = tot[:, span:, :] + tot[:, :-span, :]
            span *= 2
        tot = tot[:, HALO - w + 1:HALO - w + 1 + tp, :]
        cnt = jnp.minimum(w, pos + 1).astype(F32)
        d = (tot / cnt - own).astype(BF16).reshape(nb * tp, POOL_GROUP)
        y = jnp.dot(d, wp_ref[gi], preferred_element_type=F32) * ps_ref[:, lanes]
        o_ref[:, :, lanes] = y.reshape(nb, tp, POOL_GROUP).astype(o_ref.dtype)


def _pool(u, halo, w_pool, pool_scale, *, nb, tiles_per_seq, pos_base, out_dtype):
    n_tiles, tp, _ = u.shape
    return pl.pallas_call(
        functools.partial(_pool_kernel, nb=nb, tp=tp, tiles_per_seq=tiles_per_seq, pos_base=pos_base),
        grid=(n_tiles // nb,),
        in_specs=[
            pl.BlockSpec((nb, tp, POOL_WIDTH), lambda i: (i, 0, 0)),
            pl.BlockSpec((nb, HALO, POOL_WIDTH), lambda i: (i, 0, 0)),
            pl.BlockSpec(w_pool.shape, lambda i: (0, 0, 0)),
            pl.BlockSpec((1, POOL_WIDTH), lambda i: (0, 0)),
        ],
        out_specs=pl.BlockSpec((nb, tp, POOL_WIDTH), lambda i: (i, 0, 0)),
        out_shape=jax.ShapeDtypeStruct((n_tiles, tp, POOL_WIDTH), out_dtype),
        scratch_shapes=[pltpu.VMEM((nb, HALO + tp, POOL_WIDTH), F32)],
        compiler_params=_cparams(("arbitrary",), V7X_VMEM_LIMIT),
        name="pool",
    )(u, halo, w_pool, pool_scale)


def _rms(x, g):
    return (x * lax.rsqrt(jnp.mean(x * x, axis=-1, keepdims=True) + NORM_EPS)) * g


def _mlp_kernel(x_ref, o_ref, p_ref, wo_ref, gm_ref, wu_ref, wd_ref, gf_ref, y_ref, *, ff_chunk):
    mix = jnp.concatenate([o_ref[...], p_ref[...]], axis=1)
    h = x_ref[...] + jnp.dot(mix, wo_ref[...], preferred_element_type=F32)
    hn = _rms(h, gm_ref[...]).astype(BF16)
    acc = jnp.zeros(h.shape, F32)
    for c in range(D_FF // ff_chunk):
        a = jnp.dot(hn, wu_ref[:, c * ff_chunk:(c + 1) * ff_chunk], preferred_element_type=F32)
        a = jnp.square(jnp.maximum(a, 0.0)).astype(BF16)
        acc = acc + jnp.dot(a, wd_ref[c * ff_chunk:(c + 1) * ff_chunk, :], preferred_element_type=F32)
    y_ref[...] = _rms(h + acc, gf_ref[...])


def _mlp(x, o, p, w_out, g_mlp, w_up, w_down, g_final, *, tm):
    N = x.shape[0]
    const = lambda i: (0, 0)
    resident = lambda a: pl.BlockSpec(a.shape, const, pipeline_mode=pl.Buffered(1))
    return pl.pallas_call(
        functools.partial(_mlp_kernel, ff_chunk=1024),
        grid=(N // tm,),
        in_specs=[
            pl.BlockSpec((tm, D_MODEL), lambda i: (i, 0)),
            pl.BlockSpec((tm, ATTN_WIDTH), lambda i: (i, 0)),
            pl.BlockSpec((tm, POOL_WIDTH), lambda i: (i, 0)),
            resident(w_out), resident(g_mlp), resident(w_up), resident(w_down), resident(g_final),
        ],
        out_specs=pl.BlockSpec((tm, D_MODEL), lambda i: (i, 0)),
        out_shape=jax.ShapeDtypeStruct((N, D_MODEL), F32),
        compiler_params=_cparams(("arbitrary",), V7X_VMEM_LIMIT),
        name="mlp",
    )(x, o, p, w_out, g_mlp, w_up, w_down, g_final)


def _rows_from_kt(kt):
    B, _, S = kt.shape
    return jnp.transpose(kt.reshape(B, N_KV, 2, HEAD_DIM, S), (0, 4, 1, 2, 3))[None]


def _prep_weights(w_in, cmp_pe, cmp_w1, cmp_w2):
    w_t = jnp.transpose(w_in)
    a, b = ATTN_WIDTH, ATTN_WIDTH + 3 * KV_WIDTH
    wq, wkv = w_t[:a], w_t[a:b]
    wg, wu = w_t[b:b + 3 * N_HEADS], w_t[b + 3 * N_HEADS:]
    wg = jnp.pad(wg, ((0, N_GATE_PAD - 3 * N_HEADS), (0, 0)))
    wtok = jnp.concatenate([wq, wu, wg], axis=0).astype(BF16)
    wtok_kv = jnp.concatenate([wq, wu, wg, wkv], axis=0).astype(BF16)
    return wtok, wtok_kv, wkv.astype(BF16), _compress_weights(cmp_pe, cmp_w1, cmp_w2)


def _prompt_path(x, wts, g_attn, w_pool, pool_scale, w_out, g_mlp, w_up, w_down, g_final):
    wtok, _, wkv, (pet, w1bd, w2bd) = wts
    B, T, _ = x.shape
    tm = min(512, T)
    q, gates, u, kc_t, ks_t, kw_t, kvtb, pages = _proj(x, g_attn, wtok, wkv, tm=tm, emit_tok_kv=False,
                                                        emit_pages=True)
    n_pages = B * T // PAGE_SIZE
    comp = _compress(jnp.arange(n_pages, dtype=jnp.int32), pages, pet, w1bd, w2bd, P=min(64, n_pages))
    o = _p_attn(q, gates, comp, kvtb, _key_consts(T, T // PAGE_SIZE), tq=min(256, T), tk=min(512, T))
    nt = T // tm
    u4 = u.reshape(B, nt, tm, POOL_WIDTH)
    halo = jnp.concatenate([jnp.zeros((B, 1, HALO, POOL_WIDTH), F32), u4[:, :-1, tm - HALO:, :]], axis=1)
    pool = _pool(u.reshape(B * nt, tm, POOL_WIDTH), halo.reshape(B * nt, HALO, POOL_WIDTH),
                 w_pool, pool_scale, nb=1, tiles_per_seq=nt, pos_base=0, out_dtype=BF16)
    N = B * T
    y = _mlp(x.reshape(N, D_MODEL), o.reshape(N, ATTN_WIDTH), pool.reshape(N, POOL_WIDTH),
             w_out, g_mlp, w_up, w_down, g_final, tm=tm)
    wk = min(WINDOW, T)
    return (y.reshape(B, T, D_MODEL), _rows_from_kt(kc_t), _rows_from_kt(ks_t),
            _rows_from_kt(kw_t[:, :, T - wk:]), u[None, :, T - POOL_HIST:])


def _kt_view(rows):
    n, s = rows.shape[:2]
    return jnp.transpose(rows, (0, 2, 3, 4, 1)).reshape(n, KV_WIDTH, s)


def _sample_path(x, cache_cmp, cache_slc, state_win, state_pool, page_table, wts, g_attn,
                 w_pool, pool_scale, w_out, g_mlp, w_up, w_down, g_final):
    _, wtok_kv, wkv, (pet, w1bd, w2bd) = wts
    Bs, ts, _ = x.shape
    N = Bs * ts
    npg = page_table.shape[1]
    past = npg * PAGE_SIZE
    q, gates, u, _, _, _, _, kv_tok, knew = _proj(x.reshape(1, N, D_MODEL), g_attn, wtok_kv, wkv, tm=N,
                                                  emit_tok_kv=True, emit_pages=False, seq_len=ts)
    comp_past = _compress(page_table.reshape(-1), _kt_view(cache_cmp), pet, w1bd, w2bd, P=min(128, Bs * npg))
    comp_new = _compress(jnp.arange(Bs, dtype=jnp.int32), knew[0], pet, w1bd, w2bd, P=Bs)
    qf = q.reshape(Bs, ts, ATTN_WIDTH).astype(F32)
    gts = gates.reshape(Bs, ts, N_GATE_PAD)
    oc, bits = _s_select(qf, gts, comp_past, comp_new.astype(F32).reshape(N_KV, 2, Bs, 1, PAGE_SIZE), past=past)
    o, win_new = _s_attn(page_table, bits, qf, gts, oc, _kt_view(state_win), knew, _kt_view(cache_slc), past=past)
    u3 = u.reshape(Bs, ts, POOL_WIDTH)
    halo = jnp.pad(state_pool, ((0, 0), (HALO - POOL_HIST, 0), (0, 0)))
    pool = _pool(u3, halo, w_pool, pool_scale, nb=Bs, tiles_per_seq=1, pos_base=past, out_dtype=F32)
    y = _mlp(x.reshape(N, D_MODEL), o.reshape(N, ATTN_WIDTH).astype(BF16), pool.reshape(N, POOL_WIDTH).astype(BF16),
             w_out, g_mlp, w_up, w_down, g_final, tm=N)
    kv_rows = kv_tok.reshape(Bs, ts, 3, N_KV, 2, HEAD_DIM)
    s_pool = jnp.concatenate([state_pool, u3], axis=1)[None, :, ts:]
    return (y.reshape(Bs, ts, D_MODEL), kv_rows[None, :, :, 0], kv_rows[None, :, :, 1],
            _rows_from_kt(win_new), s_pool)


def kernel(x_prompt, x_sample, cache_cmp, cache_slc, state_win, state_pool, page_table, g_attn, w_in,
           cmp_pe, cmp_w1, cmp_w2, w_pool, pool_scale, w_out, g_mlp, w_up, w_down, g_final):
    assert g_attn.shape[0] == 1, "single-layer trunk"
    wts = _prep_weights(w_in[0], cmp_pe[0], cmp_w1[0], cmp_w2[0])
    mlp_w = (w_out[0].astype(BF16), g_mlp, w_up[0].astype(BF16), w_down[0].astype(BF16), g_final[None])
    y_p, p_cmp, p_slc, p_win, p_pool = _prompt_path(x_prompt, wts, g_attn, w_pool[0], pool_scale, *mlp_w)
    y_s, s_cmp, s_slc, s_win, s_pool = _sample_path(x_sample, cache_cmp[0], cache_slc[0], state_win[0],
                                                    state_pool[0], page_table, wts, g_attn, w_pool[0],
                                                    pool_scale, *mlp_w)
    return (y_p, y_s, p_cmp, p_slc, p_win, p_pool, s_cmp, s_slc, s_win, s_pool)
```

```python
import functools
import math

import jax
import jax.numpy as jnp
from jax import lax
from jax.experimental import pallas as pl
from jax.experimental.pallas import tpu as pltpu

D_MODEL = 1024
N_HEADS = 8
N_KV = 2
HPG = N_HEADS // N_KV
HEAD_DIM = 64
ATTN_WIDTH = N_HEADS * HEAD_DIM
POOL_WIDTH = D_MODEL - ATTN_WIDTH
POOL_WINDOWS = (2, 4, 8, 16)
POOL_GROUP = POOL_WIDTH // len(POOL_WINDOWS)
POOL_HIST = max(POOL_WINDOWS) - 1
BLK = 64
N_SELECT = 16
WINDOW = 512
CMP_HIDDEN = 2 * HEAD_DIM
D_FF = 4 * D_MODEL
KV_WIDTH = N_KV * 2 * HEAD_DIM
PAGE_SIZE = 128
BLOCKS_PER_PAGE = PAGE_SIZE // BLK
SCALE = HEAD_DIM ** -0.5
FORCE_SCORE = float(HPG + 1)
NORM_EPS = 1e-6
NEG = -1e30
N_GATE_PAD = 128
HALO = 16

F32 = jnp.float32
BF16 = jnp.bfloat16
NT_DIMS = (((1,), (1,)), ((), ()))

V7X_VMEM_LIMIT = 56 * 1024 * 1024

TOKEN_TILE = 512
ATTN_Q_TILE = 256
ATTN_K_TILE = 512
PROMPT_CMP_PAGES = 64
DECODE_CMP_PAGES = 128
DECODE_PAGE_CHUNK = 8
FF_CHUNK = 1024


def _slope(head):
    return float(2.0 ** (-8.0 * (head + 1) / N_HEADS))


def _cparams(sem, vmem=None, flags=None):
    return pltpu.CompilerParams(dimension_semantics=sem, vmem_limit_bytes=vmem, flags=flags)


def _proj_kernel(x_ref, g_ref, wtok_ref, wkv_ref, q_ref, gate_ref, u_ref, kc_ref, ks_ref, kw_ref, kvtb_ref, *rest,
                 tm, emit_tok_kv, emit_pages, seq_len):
    x = x_ref[0]
    ms = jnp.mean(x * x, axis=-1, keepdims=True)
    xn = ((x * lax.rsqrt(ms + NORM_EPS)) * g_ref[...]).astype(BF16)
    tok = lax.dot_general(xn, wtok_ref[...], NT_DIMS, preferred_element_type=F32)
    q_ref[0] = (tok[:, :ATTN_WIDTH] * SCALE).astype(BF16)
    u_ref[0] = tok[:, ATTN_WIDTH:ATTN_WIDTH + POOL_WIDTH]
    gl = tok[:, ATTN_WIDTH + POOL_WIDTH:ATTN_WIDTH + POOL_WIDTH + N_GATE_PAD]
    gate_ref[0] = 1.0 / (1.0 + jnp.exp(-gl))
    kvt = lax.dot_general(wkv_ref[...], xn, NT_DIMS, preferred_element_type=F32)
    for br, ref in enumerate((kc_ref, ks_ref, kw_ref)):
        ref[0] = kvt[br * KV_WIDTH:(br + 1) * KV_WIDTH]
    for br in range(2):
        kvtb_ref[br, 0] = kvt[(br + 1) * KV_WIDTH:(br + 2) * KV_WIDTH].astype(BF16)
    k = 0
    if emit_tok_kv:
        base = ATTN_WIDTH + POOL_WIDTH + N_GATE_PAD
        rest[k][0] = tok[:, base:base + 3 * KV_WIDTH]
        lane = lax.broadcasted_iota(jnp.int32, (1, PAGE_SIZE), 1)
        kvp = kvt if tm >= PAGE_SIZE else jnp.concatenate(
            [kvt, jnp.zeros((3 * KV_WIDTH, PAGE_SIZE - tm), F32)], axis=1)
        for s in range(tm // seq_len):
            grp, off = divmod(s * seq_len, PAGE_SIZE)
            blk = kvp[:, grp * PAGE_SIZE:(grp + 1) * PAGE_SIZE]
            if off:
                blk = pltpu.roll(blk, PAGE_SIZE - off, 1)
            blk = jnp.where(lane < seq_len, blk, 0.0)
            for br in range(3):
                rest[k + 1][br, s] = blk[br * KV_WIDTH:(br + 1) * KV_WIDTH]
        k += 2
    if emit_pages:
        for pg in range(tm // PAGE_SIZE):
            rest[k][pg] = kvt[:KV_WIDTH, pg * PAGE_SIZE:(pg + 1) * PAGE_SIZE]


def _proj(x, g_attn, wtok, wkv, *, tm, emit_tok_kv, emit_pages, seq_len=None):
    B, T, _ = x.shape
    nt = T // tm
    out_shape = [
        jax.ShapeDtypeStruct((B, T, ATTN_WIDTH), BF16),
        jax.ShapeDtypeStruct((B, T, N_GATE_PAD), F32),
        jax.ShapeDtypeStruct((B, T, POOL_WIDTH), F32),
    ] + [jax.ShapeDtypeStruct((B, KV_WIDTH, T), F32)] * 3 + [
        jax.ShapeDtypeStruct((2, B, KV_WIDTH, T), BF16),
    ]
    out_specs = [
        pl.BlockSpec((1, tm, ATTN_WIDTH), lambda b, i: (b, i, 0)),
        pl.BlockSpec((1, tm, N_GATE_PAD), lambda b, i: (b, i, 0)),
        pl.BlockSpec((1, tm, POOL_WIDTH), lambda b, i: (b, i, 0)),
    ] + [pl.BlockSpec((1, KV_WIDTH, tm), lambda b, i: (b, 0, i))] * 3 + [
        pl.BlockSpec((2, 1, KV_WIDTH, tm), lambda b, i: (0, b, 0, i)),
    ]
    if emit_tok_kv:
        out_shape.append(jax.ShapeDtypeStruct((B, T, 3 * KV_WIDTH), F32))
        out_specs.append(pl.BlockSpec((1, tm, 3 * KV_WIDTH), lambda b, i: (b, i, 0)))
        assert B == 1 and nt == 1 and tm % seq_len == 0
        out_shape.append(jax.ShapeDtypeStruct((3, tm // seq_len, KV_WIDTH, PAGE_SIZE), F32))
        out_specs.append(pl.BlockSpec((3, tm // seq_len, KV_WIDTH, PAGE_SIZE), lambda b, i: (0, 0, 0, 0)))
    if emit_pages:
        ppt = tm // PAGE_SIZE
        out_shape.append(jax.ShapeDtypeStruct((B * T // PAGE_SIZE, KV_WIDTH, PAGE_SIZE), F32))
        out_specs.append(pl.BlockSpec((ppt, KV_WIDTH, PAGE_SIZE), lambda b, i: (b * nt + i, 0, 0)))
    return pl.pallas_call(
        functools.partial(_proj_kernel, tm=tm, emit_tok_kv=emit_tok_kv, emit_pages=emit_pages, seq_len=seq_len),
        grid=(B, nt),
        in_specs=[
            pl.BlockSpec((1, tm, D_MODEL), lambda b, i: (b, i, 0)),
            pl.BlockSpec((1, D_MODEL), lambda b, i: (0, 0)),
            pl.BlockSpec(wtok.shape, lambda b, i: (0, 0)),
            pl.BlockSpec(wkv.shape, lambda b, i: (0, 0)),
        ],
        out_specs=out_specs,
        out_shape=out_shape,
        compiler_params=_cparams(("arbitrary", "arbitrary"), V7X_VMEM_LIMIT),
        name="proj",
    )(x, g_attn, wtok, wkv)


def _gelu_tanh(x):
    c = math.sqrt(2.0 / math.pi)
    return 0.5 * x * (1.0 + jnp.tanh(c * (x + 0.044715 * (x * x * x))))


def _compress_kernel(ids_ref, src_ref, pet_ref, w1_ref, w2_ref, out_ref, buf, sem, *, P, n_steps):
    i = pl.program_id(0)

    def page_copy(step, slot, p):
        return pltpu.make_async_copy(src_ref.at[ids_ref[step * P + p]], buf.at[slot, :, p, :], sem.at[slot])

    def start(step, slot):
        for p in range(P):
            page_copy(step, slot, p).start()

    @pl.when(i == 0)
    def _():
        start(0, 0)

    @pl.when(i + 1 < n_steps)
    def _():
        start(i + 1, (i + 1) % 2)

    slot = i % 2
    for p in range(P):
        page_copy(i, slot, p).wait()

    bref = buf.at[slot]
    for c in range(2):
        acc = jnp.zeros((N_KV * P, 2 * CMP_HIDDEN), F32)
        for dp in range(HEAD_DIM // 2):
            rows = []
            for g in range(N_KV):
                halves = []
                for dd in range(2):
                    r = g * 2 * HEAD_DIM + c * HEAD_DIM + 2 * dp + dd
                    pr = c * HEAD_DIM + 2 * dp + dd
                    halves.append(bref[r] + pet_ref[pr:pr + 1, :])
                rows.append(jnp.concatenate(halves, axis=1))
            lhs = jnp.concatenate(rows, axis=0).astype(BF16)
            zero = jnp.zeros((BLK, CMP_HIDDEN), BF16)
            w_rows = []
            for dd in range(2):
                w = w1_ref[c, dp, dd]
                w_rows += [jnp.concatenate([w, zero], axis=1), jnp.concatenate([zero, w], axis=1)]
            acc = acc + jnp.dot(lhs, jnp.concatenate(w_rows, axis=0), preferred_element_type=F32)
        hid = _gelu_tanh(acc).astype(BF16)
        oc = jnp.dot(hid, w2_ref[c], preferred_element_type=F32).astype(BF16)
        for g in range(N_KV):
            out_ref[g, c] = oc[g * P:(g + 1) * P]


def _compress(ids, src, pet, w1bd, w2bd, *, P):
    n_total = ids.shape[0]
    n_steps = n_total // P
    grid_spec = pltpu.PrefetchScalarGridSpec(
        num_scalar_prefetch=1,
        grid=(n_steps,),
        in_specs=[
            pl.BlockSpec(memory_space=pl.ANY),
            pl.BlockSpec(pet.shape, lambda i, ids: (0, 0)),
            pl.BlockSpec(w1bd.shape, lambda i, ids: (0, 0, 0, 0, 0), pipeline_mode=pl.Buffered(1)),
            pl.BlockSpec(w2bd.shape, lambda i, ids: (0, 0, 0)),
        ],
        out_specs=pl.BlockSpec((N_KV, 2, P, PAGE_SIZE), lambda i, ids: (0, 0, i, 0)),
        scratch_shapes=[
            pltpu.VMEM((2, KV_WIDTH, P, PAGE_SIZE), F32),
            pltpu.SemaphoreType.DMA((2,)),
        ],
    )
    return pl.pallas_call(
        functools.partial(_compress_kernel, P=P, n_steps=n_steps),
        grid_spec=grid_spec,
        out_shape=jax.ShapeDtypeStruct((N_KV, 2, n_total, PAGE_SIZE), BF16),
        compiler_params=_cparams(("arbitrary",), V7X_VMEM_LIMIT),
        name="compress",
    )(ids, src, pet, w1bd, w2bd)


def _compress_weights(cmp_pe, cmp_w1, cmp_w2):
    eye = jnp.eye(BLOCKS_PER_PAGE, dtype=F32)
    pet = jnp.tile(jnp.transpose(cmp_pe, (1, 2, 0)).reshape(2 * HEAD_DIM, BLK), (1, BLOCKS_PER_PAGE))
    w1 = cmp_w1.reshape(2, BLK, HEAD_DIM // 2, 2, CMP_HIDDEN)
    w1bd = jnp.transpose(w1, (0, 2, 3, 1, 4)).astype(BF16)
    w2bd = jnp.einsum('chd,jk->cjhkd', cmp_w2, eye).reshape(
        2, BLOCKS_PER_PAGE * CMP_HIDDEN, BLOCKS_PER_PAGE * HEAD_DIM).astype(BF16)
    return pet, w1bd, w2bd


def _rank_select_t(score, nblk, n_select):
    tq, W = score.shape
    x = score.T
    groups = [x[8 * k:8 * k + 8] for k in range(nblk // 8)]
    ranks = [jnp.zeros((8, tq), jnp.int32) for _ in groups]
    row = lax.broadcasted_iota(jnp.int32, (8, tq), 0)
    for i in range(nblk):
        xi = groups[i // 8][i % 8:i % 8 + 1, :]
        for k, xg in enumerate(groups):
            if 8 * k + 7 < i:
                ahead = xi > xg
            elif 8 * k > i:
                ahead = xi >= xg
            else:
                ahead = (xi > xg) | ((xi == xg) & (row > i - 8 * k))
            ranks[k] = ranks[k] + jnp.where(ahead, 1, 0)
    sel = [jnp.where(r < n_select, 1.0, 0.0) for r in ranks]
    if nblk < W:
        sel.append(jnp.zeros((W - nblk, tq), F32))
    return jnp.concatenate(sel, axis=0).T


def _key_consts(T, npg):
    assert T <= 256 * BLK and npg * BLOCKS_PER_PAGE <= HEAD_DIM
    k = jnp.arange(T, dtype=jnp.int32)[None, :]
    r = jnp.arange(HEAD_DIM, dtype=jnp.int32)[:, None]
    pos = jnp.where(r == 0, k // BLK, jnp.where(r == 1, k % BLK, 0)).astype(F32)
    en = jnp.where(k // BLK == r, NEG, 0.0)
    return jnp.concatenate([pos, en, jnp.zeros((HEAD_DIM, T), F32)], axis=0).astype(BF16)


M_INIT = -3e38


def _p_attn_kernel(q_ref, gate_ref, comp_ref, ks_ref, kw_ref, kc_ref, o_ref,
                   qx, m_ref, acc_ref, out_ref, flag_v, flag_s, flag_sem, *, tq, tk, npg):
    qi = pl.program_id(1)
    q0 = qi * tq
    pos_t = q0 + lax.broadcasted_iota(jnp.int32, (tq, 1), 0)
    nblk = npg * BLOCKS_PER_PAGE
    W = 128
    R = HPG * tq
    lane_h = lax.broadcasted_iota(jnp.int32, (1, HEAD_DIM), 1)

    for hh in range(N_HEADS):
        g, h = divmod(hh, HPG)
        rows = pl.ds(h * tq, tq)
        qx[g, rows, 0:HEAD_DIM] = q_ref[0, :, hh * HEAD_DIM:(hh + 1) * HEAD_DIM]
        posc = jnp.where(lane_h == 0, _slope(hh) * BLK, jnp.where(lane_h == 1, _slope(hh), 0.0))
        qx[g, rows, HEAD_DIM:2 * HEAD_DIM] = jnp.broadcast_to(posc, (tq, HEAD_DIM)).astype(BF16)
        qx[g, rows, 3 * HEAD_DIM:4 * HEAD_DIM] = jnp.zeros((tq, HEAD_DIM), BF16)

    blk_n = lax.broadcasted_iota(jnp.int32, (1, W), 1)
    first = blk_n < nblk
    end_pos = (blk_n + 1) * BLK - 1
    dist_c = (pos_t - end_pos).astype(F32)
    mask_c = (dist_c >= 0) & first
    cur = pos_t // BLK
    row_w = lax.broadcasted_iota(jnp.int32, (W, W), 0)
    col_w = lax.broadcasted_iota(jnp.int32, (W, W), 1)
    perm = jnp.where((col_w == (row_w % 2) * npg + row_w // 2) & (row_w < nblk), 1.0, 0.0).astype(BF16)

    used = []
    for g in range(N_KV):
        ck = comp_ref[g, 0].astype(F32)
        cv = comp_ref[g, 1].astype(F32)
        pad = [jnp.zeros((W - nblk, HEAD_DIM), F32)] if nblk < W else []
        ckp = jnp.concatenate([ck[:, :HEAD_DIM], ck[:, HEAD_DIM:]] + pad, axis=0).astype(BF16)
        cvp = jnp.concatenate([cv[:, :HEAD_DIM], cv[:, HEAD_DIM:]] + pad, axis=0).astype(BF16)
        ckp = jnp.dot(perm, ckp, preferred_element_type=F32).astype(BF16)
        cvp = jnp.dot(perm, cvp, preferred_element_type=F32).astype(BF16)
        imp = jnp.zeros((tq, W), F32)
        for h in range(HPG):
            hh = g * HPG + h
            qh = qx[g, pl.ds(h * tq, tq), 0:HEAD_DIM]
            s = lax.dot_general(qh, ckp, NT_DIMS, preferred_element_type=F32)
            s = jnp.where(mask_c, s - _slope(hh) * dist_c, NEG)
            mx = jnp.max(s, axis=1, keepdims=True)
            e = jnp.where(mask_c, jnp.exp(s - mx), 0.0)
            p = e / jnp.maximum(jnp.sum(e, axis=1, keepdims=True), 1e-30)
            imp = imp + p
            oc = jnp.dot(p.astype(BF16), cvp, preferred_element_type=F32)
            gcol = gate_ref[0, :, hh:hh + 1]
            out_ref[:, hh * HEAD_DIM:(hh + 1) * HEAD_DIM] = gcol * oc
        forced = (blk_n == 0) | (blk_n == cur) | (blk_n == cur - 1)
        score = jnp.where(forced, FORCE_SCORE, imp)
        score = jnp.where(blk_n > cur, -1.0, score)
        sel = _rank_select_t(score, nblk, N_SELECT)
        picked = (sel > 0.5) & (blk_n <= cur)
        notsel = jnp.where(picked, 0.0, 1.0).astype(BF16)
        for h in range(HPG):
            qx[g, pl.ds(h * tq, tq), 2 * HEAD_DIM:3 * HEAD_DIM] = notsel[:, :HEAD_DIM]
        used.append(jnp.max(jnp.where(picked, 1, 0), axis=0, keepdims=True))

    flag_v[...] = jnp.concatenate(used + [jnp.zeros((8 - N_KV, W), jnp.int32)], axis=0)
    flag_copy = pltpu.make_async_copy(flag_v, flag_s, flag_sem.at[0])
    flag_copy.start()

    pos_rows = q0 + lax.broadcasted_iota(jnp.int32, (R, 1), 0) % tq

    ones_rows = jnp.where(lax.broadcasted_iota(jnp.int32, (HEAD_DIM, 1), 0) == 0, 1.0, 0.0)

    def attend_tile(kv_ref, k0, width, kc_rows, mode, groups=tuple(range(N_KV))):
        kpos = k0 + lax.broadcasted_iota(jnp.int32, (1, width), 1)
        kc_tile = kc_ref[0:kc_rows, pl.ds(k0, width)]
        ones_tile = jnp.broadcast_to(ones_rows, (HEAD_DIM, width)).astype(BF16)
        if mode == "causal":
            keep = kpos <= pos_rows
        elif mode == "recent":
            keep = kpos > pos_rows - WINDOW
        for g in groups:
            kt = kv_ref[0, 0, g * 2 * HEAD_DIM:g * 2 * HEAD_DIM + HEAD_DIM, pl.ds(k0, width)]
            kext = jnp.concatenate([kt, kc_tile], axis=0)
            vt = kv_ref[0, 0, g * 2 * HEAD_DIM + HEAD_DIM:(g + 1) * 2 * HEAD_DIM, pl.ds(k0, width)]
            vext = jnp.concatenate([vt, ones_tile], axis=0)
            s = jnp.dot(qx[g, :, 0:HEAD_DIM + kc_rows], kext, preferred_element_type=F32)
            if mode != "full":
                s = jnp.where(keep, s, NEG)
            m_prev = m_ref[g]
            m_next = jnp.maximum(m_prev, jnp.max(s, axis=1, keepdims=True))
            alpha = jnp.exp(m_prev - m_next)
            m_ref[g] = m_next
            p = jnp.exp(s - jnp.concatenate([m_next] * (width // W), axis=1)).astype(BF16)
            pv = lax.dot_general(p, vext, NT_DIMS, preferred_element_type=F32)
            acc_ref[g] = alpha * acc_ref[g] + pv

    def reset():
        for g in range(N_KV):
            m_ref[g] = jnp.full((R, W), M_INIT, F32)
            acc_ref[g] = jnp.zeros((R, W), F32)

    def finish(gate_base):
        for g in range(N_KV):
            for h in range(HPG):
                hh = g * HPG + h
                a = acc_ref[g, pl.ds(h * tq, tq), :]
                o = a[:, :HEAD_DIM] / jnp.maximum(a[:, HEAD_DIM:HEAD_DIM + 1], 1e-30)
                gcol = gate_ref[0, :, gate_base + hh:gate_base + hh + 1]
                out_ref[:, hh * HEAD_DIM:(hh + 1) * HEAD_DIM] += gcol * o

    reset()
    wt = WINDOW // tq

    @pl.when(qi >= wt)
    def _():
        attend_tile(kw_ref, pl.multiple_of((qi - wt) * tq, tq), WINDOW, HEAD_DIM, "recent")

    @pl.when(qi < wt)
    def _():
        def win_body(j, carry):
            attend_tile(kw_ref, pl.multiple_of(j * tq, tq), tq, HEAD_DIM, "full")
            return carry

        lax.fori_loop(0, qi, win_body, 0)

    attend_tile(kw_ref, pl.multiple_of(q0, tq), tq, HEAD_DIM, "causal")
    finish(2 * N_HEADS)

    reset()
    flag_copy.wait()
    n_sel = (q0 + tq + tk - 1) // tk
    bpt = tk // BLK

    def sel_body(j, carry):
        hits = []
        for g in range(N_KV):
            hit = flag_s[g, j * bpt]
            for u in range(1, bpt):
                hit = hit | flag_s[g, j * bpt + u]
            hits.append(hit != 0)
        k0 = pl.multiple_of(j * tk, tk)

        @pl.when(hits[0] & hits[1])
        def _():
            attend_tile(ks_ref, k0, tk, 3 * HEAD_DIM, "full")

        @pl.when(hits[0] & jnp.logical_not(hits[1]))
        def _():
            attend_tile(ks_ref, k0, tk, 3 * HEAD_DIM, "full", groups=(0,))

        @pl.when(jnp.logical_not(hits[0]) & hits[1])
        def _():
            attend_tile(ks_ref, k0, tk, 3 * HEAD_DIM, "full", groups=(1,))

        return carry

    lax.fori_loop(0, n_sel - 1, sel_body, 0)
    attend_tile(ks_ref, pl.multiple_of((n_sel - 1) * tk, tk), tk, 3 * HEAD_DIM, "causal")
    finish(N_HEADS)
    o_ref[0] = out_ref[...].astype(BF16)


def _p_attn(q, gates, comp, kvtb, kconst, *, tq, tk):
    B, T, _ = q.shape
    npg = T // PAGE_SIZE
    assert WINDOW % tq == 0 and tk % tq == 0
    return pl.pallas_call(
        functools.partial(_p_attn_kernel, tq=tq, tk=tk, npg=npg),
        grid=(B, T // tq),
        in_specs=[
            pl.BlockSpec((1, tq, ATTN_WIDTH), lambda b, i: (b, i, 0)),
            pl.BlockSpec((1, tq, N_GATE_PAD), lambda b, i: (b, i, 0)),
            pl.BlockSpec((N_KV, 2, npg, PAGE_SIZE), lambda b, i: (0, 0, b, 0)),
            pl.BlockSpec((1, 1, KV_WIDTH, T), lambda b, i: (0, b, 0, 0)),
            pl.BlockSpec((1, 1, KV_WIDTH, T), lambda b, i: (1, b, 0, 0)),
            pl.BlockSpec(kconst.shape, lambda b, i: (0, 0)),
        ],
        out_specs=pl.BlockSpec((1, tq, ATTN_WIDTH), lambda b, i: (b, i, 0)),
        out_shape=jax.ShapeDtypeStruct((B, T, ATTN_WIDTH), BF16),
        scratch_shapes=[
            pltpu.VMEM((N_KV, HPG * tq, 4 * HEAD_DIM), BF16),
            pltpu.VMEM((N_KV, HPG * tq, 128), F32),
            pltpu.VMEM((N_KV, HPG * tq, 128), F32),
            pltpu.VMEM((tq, ATTN_WIDTH), F32),
            pltpu.VMEM((8, 128), jnp.int32),
            pltpu.SMEM((8, 128), jnp.int32),
            pltpu.SemaphoreType.DMA((1,)),
        ],
        compiler_params=_cparams(("arbitrary", "arbitrary"), V7X_VMEM_LIMIT),
        name="p_attn",
    )(q, gates, comp, kvtb, kvtb, kconst)


def _stack_heads(q_ref, g, ts):
    parts = [q_ref[0, :, (g * HPG + h) * HEAD_DIM:(g * HPG + h + 1) * HEAD_DIM].astype(F32) for h in range(HPG)]
    return jnp.concatenate(parts, axis=0).astype(BF16)


def _row_consts(g, ts):
    R = HPG * ts
    row = lax.broadcasted_iota(jnp.int32, (R, 1), 0)
    t_row = row % ts
    h_row = row // ts
    slope = jnp.zeros((R, 1), F32)
    for h in range(HPG):
        slope = jnp.where(h_row == h, _slope(g * HPG + h), slope)
    return t_row, slope


def _s_select_kernel(q_ref, gate_ref, comp_ref, cnew_ref, oc_ref, bits_ref, out_s, *, past, ts):
    W = 128
    nb_past = past // BLK
    lane = lax.broadcasted_iota(jnp.int32, (1, W), 1)
    idx_fns = [lambda c: 2 * c, lambda c: 2 * c + 1, lambda c: jnp.where(c == 0, nb_past, (1 << 20) + c)]
    idx_tiles = [f(lane) for f in idx_fns]
    valid_n = lane == 0
    tok = lax.broadcasted_iota(jnp.int32, (ts, 1), 0)
    cur_t = (past + tok) // BLK
    for g in range(N_KV):
        qg = _stack_heads(q_ref, g, ts)
        t_row, slope = _row_consts(g, ts)
        pos = past + t_row
        ck = comp_ref[g, 0]
        cv = comp_ref[g, 1]
        cn = cnew_ref[g, 0, 0].astype(F32)[:, :HEAD_DIM]
        vn = cnew_ref[g, 1, 0].astype(F32)[:, :HEAD_DIM]
        s_t = [lax.dot_general(qg, ck[:, :HEAD_DIM], NT_DIMS, preferred_element_type=F32),
               lax.dot_general(qg, ck[:, HEAD_DIM:], NT_DIMS, preferred_element_type=F32),
               jnp.broadcast_to(jnp.sum(qg.astype(F32) * cn, axis=1, keepdims=True), (HPG * ts, W))]
        masks, es = [], []
        for k in range(3):
            end_pos = (idx_tiles[k] + 1) * BLK - 1
            dist = (pos - end_pos).astype(F32)
            mk = dist >= 0
            if k == 2:
                mk = mk & valid_n
            masks.append(mk)
            s_t[k] = jnp.where(mk, s_t[k] - slope * dist, NEG)
        mx = jnp.maximum(jnp.maximum(jnp.max(s_t[0], axis=1, keepdims=True),
                                     jnp.max(s_t[1], axis=1, keepdims=True)),
                         jnp.max(s_t[2], axis=1, keepdims=True))
        for k in range(3):
            es.append(jnp.where(masks[k], jnp.exp(s_t[k] - mx), 0.0))
        den = (jnp.sum(es[0], axis=1, keepdims=True) + jnp.sum(es[1], axis=1, keepdims=True)
               + jnp.sum(es[2], axis=1, keepdims=True))
        inv = 1.0 / jnp.maximum(den, 1e-30)
        ps = [e * inv for e in es]
        o_c = (jnp.dot(ps[0].astype(BF16), cv[:, :HEAD_DIM], preferred_element_type=F32)
               + jnp.dot(ps[1].astype(BF16), cv[:, HEAD_DIM:], preferred_element_type=F32)
               + ps[2][:, 0:1].astype(BF16).astype(F32) * vn)
        for h in range(HPG):
            hh = g * HPG + h
            gcol = gate_ref[0, :, hh:hh + 1]
            out_s[:, hh * HEAD_DIM:(hh + 1) * HEAD_DIM] = gcol * o_c[h * ts:(h + 1) * ts]
        scores = []
        for k in range(3):
            imp = ps[k][0:ts]
            for h in range(1, HPG):
                imp = imp + ps[k][h * ts:(h + 1) * ts]
            idx = idx_tiles[k]
            forced = (idx == 0) | (idx == cur_t) | (idx == cur_t - 1)
            sc = jnp.where(forced, FORCE_SCORE, imp)
            sc = jnp.where(idx > cur_t, -1.0, sc)
            if k == 2:
                sc = jnp.where(valid_n, sc, -2.0)
            scores.append(sc)
        s_new = scores[2][:, 0:1]
        parts = [[jnp.where(s_new > scores[ka], 1, 0), jnp.zeros((ts, W), jnp.int32)] for ka in range(2)]
        for kb in range(2):
            for r in range(W):
                y = pltpu.roll(scores[kb], r, 1) if r else scores[kb]
                yi = idx_fns[kb]((lane - r) & (W - 1))
                for ka in range(2):
                    if ka == kb and r == 0:
                        continue
                    ahead = (y > scores[ka]) | ((y == scores[ka]) & (yi < idx_tiles[ka]))
                    parts[ka][r % 2] = parts[ka][r % 2] + jnp.where(ahead, 1, 0)
        sel = [(parts[k][0] + parts[k][1]) < N_SELECT for k in range(2)]
        rank_new = (jnp.sum(jnp.where(scores[0] >= s_new, 1, 0), axis=1, keepdims=True)
                    + jnp.sum(jnp.where(scores[1] >= s_new, 1, 0), axis=1, keepdims=True))
        wt = jnp.left_shift(1, 2 * tok)
        page_bits = jnp.sum(jnp.where(sel[0], wt, 0) + jnp.where(sel[1], 2 * wt, 0), axis=0, keepdims=True)
        new_bits = jnp.sum(jnp.where((rank_new < N_SELECT) & valid_n, jnp.left_shift(1, tok), 0),
                           axis=0, keepdims=True)
        bits_ref[0, g, 0:1, :] = page_bits
        bits_ref[0, g, 1:2, :] = new_bits
    oc_ref[0] = out_s[...]


def _s_select(q, gates, comp, cnew, *, past):
    Bs, ts, _ = q.shape
    npg = past // PAGE_SIZE
    assert npg == 128, "one lane tile of pages per sequence"
    return pl.pallas_call(
        functools.partial(_s_select_kernel, past=past, ts=ts),
        grid=(Bs,),
        in_specs=[
            pl.BlockSpec((1, ts, ATTN_WIDTH), lambda b: (b, 0, 0)),
            pl.BlockSpec((1, ts, N_GATE_PAD), lambda b: (b, 0, 0)),
            pl.BlockSpec((N_KV, 2, npg, PAGE_SIZE), lambda b: (0, 0, b, 0)),
            pl.BlockSpec((N_KV, 2, 1, 1, PAGE_SIZE), lambda b: (0, 0, b, 0, 0)),
        ],
        out_specs=[
            pl.BlockSpec((1, ts, ATTN_WIDTH), lambda b: (b, 0, 0)),
            pl.BlockSpec((1, N_KV, 2, 128), lambda b: (b, 0, 0, 0)),
        ],
        out_shape=[
            jax.ShapeDtypeStruct((Bs, ts, ATTN_WIDTH), F32),
            jax.ShapeDtypeStruct((Bs, N_KV, 2, 128), jnp.int32),
        ],
        scratch_shapes=[pltpu.VMEM((ts, ATTN_WIDTH), F32)],
        compiler_params=_cparams(("arbitrary",), V7X_VMEM_LIMIT),
        name="s_select",
    )(q, gates, comp, cnew)


def _s_attn_kernel(pt_ref, bits_ref, q_ref, gate_ref, oc_ref, win_ref, knew_ref, slc_ref,
                   o_ref, wout_ref, buf, sem, plist, cnts, out_s, *, past, ts, npg, chunk):
    b = pl.program_id(0)
    g = pl.program_id(1)
    W = 128
    R = HPG * ts
    wk = win_ref.shape[2]
    lin = b * N_KV + g
    n_lin = pl.num_programs(0) * N_KV
    base = lin * 2 * W
    slot = lin % 2

    def page_copy(page, gg, sl, idx):
        return pltpu.make_async_copy(slc_ref.at[page, pl.ds(gg * 2 * HEAD_DIM, 2 * HEAD_DIM)],
                                     buf.at[sl, idx], sem.at[sl])

    def issue_all(ln, sl):
        bb = ln // N_KV
        gg = ln % N_KV

        def scan(p, cnt):
            plist[sl * npg + cnt] = p
            return cnt + (bits_ref[ln * 2 * W + p] != 0).astype(jnp.int32)

        cnt = lax.fori_loop(0, npg, scan, 0, unroll=8)
        cnts[sl] = cnt

        def start(s, c):
            page_copy(pt_ref[bb * npg + plist[sl * npg + s]], gg, sl, s).start()
            return c

        lax.fori_loop(0, cnt, start, 0)

    @pl.when(lin == 0)
    def _():
        buf[...] = jnp.zeros(buf.shape, F32)
        for k in range(2 * npg):
            plist[k] = 0
        issue_all(lin, slot)

    @pl.when(lin + 1 < n_lin)
    def _():
        issue_all(lin + 1, 1 - slot)

    cnt = cnts[slot]

    q_all = q_ref[0].astype(F32)
    qg = jnp.concatenate([q_all[:, h * HEAD_DIM:(h + 1) * HEAD_DIM] for h in range(HPG)], axis=0).astype(BF16)
    row = lax.broadcasted_iota(jnp.int32, (R, 1), 0)
    t_row = row % ts
    head_row = row // ts + g * HPG
    slope = jnp.zeros((R, 1), F32)
    for hh in range(N_HEADS):
        slope = jnp.where(head_row == hh, _slope(hh), slope)
    lane = lax.broadcasted_iota(jnp.int32, (1, W), 1)

    kw = win_ref[0, 0:HEAD_DIM, :].astype(BF16)
    vw = win_ref[0, HEAD_DIM:2 * HEAD_DIM, :].astype(BF16)
    kn = knew_ref[2, 0, 0:HEAD_DIM, :].astype(BF16)
    vn = knew_ref[2, 0, HEAD_DIM:2 * HEAD_DIM, :].astype(BF16)
    i_st = lax.broadcasted_iota(jnp.int32, (1, wk), 1)
    d_st = t_row + wk - i_st
    m_st = (d_st >= 0) & (d_st < WINDOW)
    d_nw = t_row - lane
    m_nw = (d_nw >= 0) & (d_nw < WINDOW) & (lane < ts)
    s_st = jnp.where(m_st, jnp.dot(qg, kw, preferred_element_type=F32) - slope * d_st.astype(F32), NEG)
    s_nw = jnp.where(m_nw, jnp.dot(qg, kn, preferred_element_type=F32) - slope * d_nw.astype(F32), NEG)
    mx = jnp.maximum(jnp.max(s_st, axis=1, keepdims=True), jnp.max(s_nw, axis=1, keepdims=True))
    e_st = jnp.where(m_st, jnp.exp(s_st - mx), 0.0)
    e_nw = jnp.where(m_nw, jnp.exp(s_nw - mx), 0.0)
    den = jnp.sum(e_st, axis=1, keepdims=True) + jnp.sum(e_nw, axis=1, keepdims=True)
    inv = 1.0 / jnp.maximum(den, 1e-30)
    o_w = (lax.dot_general((e_st * inv).astype(BF16), vw, NT_DIMS, preferred_element_type=F32)
           + lax.dot_general((e_nw * inv).astype(BF16), vn, NT_DIMS, preferred_element_type=F32))

    last = wk - W
    shifted = pltpu.roll(win_ref[0], wk - ts, 1)
    newr = pltpu.roll(knew_ref[2, 0], W - ts, 1)
    wout_ref[0, :, 0:last] = shifted[:, 0:last]
    wout_ref[0, :, last:wk] = jnp.where(lane >= W - ts, newr, shifted[:, last:wk])

    bits_new = bits_ref[base + W]
    ksn = knew_ref[1, 0, 0:HEAD_DIM, :].astype(BF16)
    vsn = knew_ref[1, 0, HEAD_DIM:2 * HEAD_DIM, :].astype(BF16)
    d_sn = t_row - lane
    m_sn = (d_sn >= 0) & (lane < ts) & ((jnp.right_shift(bits_new, t_row) & 1) == 1)
    s_sn = jnp.where(m_sn, jnp.dot(qg, ksn, preferred_element_type=F32) - slope * d_sn.astype(F32), NEG)
    m0 = jnp.max(s_sn, axis=1, keepdims=True)
    e0 = jnp.where(m_sn, jnp.exp(s_sn - m0), 0.0)
    l0 = jnp.sum(e0, axis=1, keepdims=True)
    a0 = lax.dot_general(e0.astype(BF16), vsn, NT_DIMS, preferred_element_type=F32)

    def wait_one(s, c):
        page_copy(0, 0, slot, s).wait()
        return c

    lax.fori_loop(0, cnt, wait_one, 0)
    sh = 2 * t_row + (lane >= BLK).astype(jnp.int32)
    sh_c = jnp.concatenate([sh] * chunk, axis=1)
    bslot = buf.at[slot]

    def chunk_step(c, carry):
        m, l, acc = carry
        kts, vts, kpos, bitv = [], [], [], []
        for u in range(chunk):
            s = c * chunk + u
            ok = s < cnt
            p = jnp.where(ok, plist[slot * npg + s], 0)
            bits = jnp.where(ok, bits_ref[base + p], 0)
            kts.append(bslot[s, 0:HEAD_DIM, :])
            vts.append(bslot[s, HEAD_DIM:2 * HEAD_DIM, :])
            kpos.append(p * PAGE_SIZE + lane)
            bitv.append(jnp.zeros((1, W), jnp.int32) + bits)
        kt = jnp.concatenate(kts, axis=1).astype(BF16)
        vt = jnp.concatenate(vts, axis=1).astype(BF16)
        dist = past + t_row - jnp.concatenate(kpos, axis=1)
        mask = ((jnp.right_shift(jnp.concatenate(bitv, axis=1), sh_c) & 1) == 1) & (dist >= 0)
        sc = jnp.where(mask, jnp.dot(qg, kt, preferred_element_type=F32) - slope * dist.astype(F32), NEG)
        m_new = jnp.maximum(m, jnp.max(sc, axis=1, keepdims=True))
        alpha = jnp.exp(m - m_new)
        e = jnp.where(mask, jnp.exp(sc - m_new), 0.0)
        l = alpha * l + jnp.sum(e, axis=1, keepdims=True)
        acc = alpha * acc + lax.dot_general(e.astype(BF16), vt, NT_DIMS, preferred_element_type=F32)
        return m_new, l, acc

    m, l, acc = lax.fori_loop(0, (cnt + chunk - 1) // chunk, chunk_step, (m0, l0, a0))
    o_s = acc / jnp.maximum(l, 1e-30)

    g_vec = jnp.zeros((ts, 1), jnp.int32) + g
    for h in range(HPG):
        rows = slice(h * ts, (h + 1) * ts)
        g_s = jnp.zeros((ts, 1), F32)
        g_w = jnp.zeros((ts, 1), F32)
        for gg in range(N_KV):
            hh = gg * HPG + h
            g_s = jnp.where(g_vec == gg, gate_ref[0, :, N_HEADS + hh:N_HEADS + hh + 1], g_s)
            g_w = jnp.where(g_vec == gg, gate_ref[0, :, 2 * N_HEADS + hh:2 * N_HEADS + hh + 1], g_w)
        out_s[:, h * HEAD_DIM:(h + 1) * HEAD_DIM] = g_s * o_s[rows] + g_w * o_w[rows]
    o_ref[0] = oc_ref[0] + out_s[...]


def _s_attn(page_table, bits, q, gates, oc, win_t, knew, slc_t, *, past):
    Bs, ts, _ = q.shape
    npg = past // PAGE_SIZE
    wk = win_t.shape[2]
    gw = HPG * HEAD_DIM
    grid_spec = pltpu.PrefetchScalarGridSpec(
        num_scalar_prefetch=2,
        grid=(Bs, N_KV),
        in_specs=[
            pl.BlockSpec((1, ts, gw), lambda b, g, *_: (b, 0, g)),
            pl.BlockSpec((1, ts, N_GATE_PAD), lambda b, g, *_: (b, 0, 0)),
            pl.BlockSpec((1, ts, gw), lambda b, g, *_: (b, 0, g)),
            pl.BlockSpec((1, 2 * HEAD_DIM, wk), lambda b, g, *_: (b, g, 0)),
            pl.BlockSpec((3, 1, 2 * HEAD_DIM, PAGE_SIZE), lambda b, g, *_: (0, b, g, 0)),
            pl.BlockSpec(memory_space=pl.ANY),
        ],
        out_specs=[
            pl.BlockSpec((1, ts, gw), lambda b, g, *_: (b, 0, g)),
            pl.BlockSpec((1, 2 * HEAD_DIM, wk), lambda b, g, *_: (b, g, 0)),
        ],
        scratch_shapes=[
            pltpu.VMEM((2, npg, 2 * HEAD_DIM, PAGE_SIZE), F32),
            pltpu.SemaphoreType.DMA((2,)),
            pltpu.SMEM((2 * npg,), jnp.int32),
            pltpu.SMEM((2,), jnp.int32),
            pltpu.VMEM((ts, gw), F32),
        ],
    )
    chunk = DECODE_PAGE_CHUNK
    assert npg % chunk == 0
    return pl.pallas_call(
        functools.partial(_s_attn_kernel, past=past, ts=ts, npg=npg, chunk=chunk),
        grid_spec=grid_spec,
        out_shape=[
            jax.ShapeDtypeStruct((Bs, ts, ATTN_WIDTH), F32),
            jax.ShapeDtypeStruct(win_t.shape, F32),
        ],
        compiler_params=_cparams(("arbitrary", "arbitrary"), V7X_VMEM_LIMIT),
        name="s_attn",
    )(page_table.reshape(-1), bits.reshape(-1), q, gates, oc, win_t, knew, slc_t)


def _pool_kernel(u_ref, halo_ref, wp_ref, ps_ref, o_ref, ext, *, nb, tp, tiles_per_seq, pos_base):
    i = pl.program_id(0)
    ext[:, 0:HALO, :] = halo_ref[...]
    ext[:, HALO:HALO + tp, :] = u_ref[...]
    tile = i * nb + lax.broadcasted_iota(jnp.int32, (nb, tp, 1), 0)
    pos = pos_base + (tile % tiles_per_seq) * tp + lax.broadcasted_iota(jnp.int32, (nb, tp, 1), 1)
    for gi, w in enumerate(POOL_WINDOWS):
        lanes = slice(gi * POOL_GROUP, (gi + 1) * POOL_GROUP)
        own = ext[:, HALO:HALO + tp, lanes]
        tot = ext[:, :, lanes]
        span = 1
        while span < w:
            tot = tot[:, span:, :] + tot[:, :-span, :]
            span *= 2
        tot = tot[:, HALO - w + 1:HALO - w + 1 + tp, :]
        cnt = jnp.minimum(w, pos + 1).astype(F32)
        d = (tot / cnt - own).astype(BF16).reshape(nb * tp, POOL_GROUP)
        y = jnp.dot(d, wp_ref[gi], preferred_element_type=F32) * ps_ref[:, lanes]
        o_ref[:, :, lanes] = y.reshape(nb, tp, POOL_GROUP).astype(o_ref.dtype)


def _pool(u, halo, w_pool, pool_scale, *, nb, tiles_per_seq, pos_base, out_dtype):
    n_tiles, tp, _ = u.shape
    return pl.pallas_call(
        functools.partial(_pool_kernel, nb=nb, tp=tp, tiles_per_seq=tiles_per_seq, pos_base=pos_base),
        grid=(n_tiles // nb,),
        in_specs=[
            pl.BlockSpec((nb, tp, POOL_WIDTH), lambda i: (i, 0, 0)),
            pl.BlockSpec((nb, HALO, POOL_WIDTH), lambda i: (i, 0, 0)),
            pl.BlockSpec(w_pool.shape, lambda i: (0, 0, 0)),
            pl.BlockSpec((1, POOL_WIDTH), lambda i: (0, 0)),
        ],
        out_specs=pl.BlockSpec((nb, tp, POOL_WIDTH), lambda i: (i, 0, 0)),
        out_shape=jax.ShapeDtypeStruct((n_tiles, tp, POOL_WIDTH), out_dtype),
        scratch_shapes=[pltpu.VMEM((nb, HALO + tp, POOL_WIDTH), F32)],
        compiler_params=_cparams(("arbitrary",), V7X_VMEM_LIMIT),
        name="pool",
    )(u, halo, w_pool, pool_scale)


def _rms(x, g):
    return (x * lax.rsqrt(jnp.mean(x * x, axis=-1, keepdims=True) + NORM_EPS)) * g


def _mlp_kernel(x_ref, o_ref, p_ref, wo_ref, gm_ref, wu_ref, wd_ref, gf_ref, y_ref, *, ff_chunk):
    mix = jnp.concatenate([o_ref[...], p_ref[...]], axis=1)
    h = x_ref[...] + jnp.dot(mix, wo_ref[...], preferred_element_type=F32)
    hn = _rms(h, gm_ref[...]).astype(BF16)
    acc = jnp.zeros(h.shape, F32)
    for c in range(D_FF // ff_chunk):
        a = jnp.dot(hn, wu_ref[:, c * ff_chunk:(c + 1) * ff_chunk], preferred_element_type=F32)
        a = jnp.square(jnp.maximum(a, 0.0)).astype(BF16)
        acc = acc + jnp.dot(a, wd_ref[c * ff_chunk:(c + 1) * ff_chunk, :], preferred_element_type=F32)
    y_ref[...] = _rms(h + acc, gf_ref[...])


def _mlp(x, o, p, w_out, g_mlp, w_up, w_down, g_final, *, tm):
    N = x.shape[0]
    const = lambda i: (0, 0)
    resident = lambda a: pl.BlockSpec(a.shape, const, pipeline_mode=pl.Buffered(1))
    return pl.pallas_call(
        functools.partial(_mlp_kernel, ff_chunk=FF_CHUNK),
        grid=(N // tm,),
        in_specs=[
            pl.BlockSpec((tm, D_MODEL), lambda i: (i, 0)),
            pl.BlockSpec((tm, ATTN_WIDTH), lambda i: (i, 0)),
            pl.BlockSpec((tm, POOL_WIDTH), lambda i: (i, 0)),
            resident(w_out), resident(g_mlp), resident(w_up), resident(w_down), resident(g_final),
        ],
        out_specs=pl.BlockSpec((tm, D_MODEL), lambda i: (i, 0)),
        out_shape=jax.ShapeDtypeStruct((N, D_MODEL), F32),
        compiler_params=_cparams(("arbitrary",), V7X_VMEM_LIMIT),
        name="mlp",
    )(x, o, p, w_out, g_mlp, w_up, w_down, g_final)


def _rows_from_kt(kt):
    B, _, S = kt.shape
    return jnp.transpose(kt.reshape(B, N_KV, 2, HEAD_DIM, S), (0, 4, 1, 2, 3))[None]


def _prep_weights(w_in, cmp_pe, cmp_w1, cmp_w2):
    w_t = jnp.transpose(w_in)
    a, b = ATTN_WIDTH, ATTN_WIDTH + 3 * KV_WIDTH
    wq, wkv = w_t[:a], w_t[a:b]
    wg, wu = w_t[b:b + 3 * N_HEADS], w_t[b + 3 * N_HEADS:]
    wg = jnp.pad(wg, ((0, N_GATE_PAD - 3 * N_HEADS), (0, 0)))
    wtok = jnp.concatenate([wq, wu, wg], axis=0).astype(BF16)
    wtok_kv = jnp.concatenate([wq, wu, wg, wkv], axis=0).astype(BF16)
    return wtok, wtok_kv, wkv.astype(BF16), _compress_weights(cmp_pe, cmp_w1, cmp_w2)


def _prompt_path(x, wts, g_attn, w_pool, pool_scale, w_out, g_mlp, w_up, w_down, g_final):
    wtok, _, wkv, (pet, w1bd, w2bd) = wts
    B, T, _ = x.shape
    tm = min(TOKEN_TILE, T)
    q, gates, u, kc_t, ks_t, kw_t, kvtb, pages = _proj(x, g_attn, wtok, wkv, tm=tm, emit_tok_kv=False,
                                                        emit_pages=True)
    n_pages = B * T // PAGE_SIZE
    comp = _compress(jnp.arange(n_pages, dtype=jnp.int32), pages, pet, w1bd, w2bd, P=min(PROMPT_CMP_PAGES, n_pages))
    o = _p_attn(q, gates, comp, kvtb, _key_consts(T, T // PAGE_SIZE), tq=min(ATTN_Q_TILE, T), tk=min(ATTN_K_TILE, T))
    nt = T // tm
    u4 = u.reshape(B, nt, tm, POOL_WIDTH)
    halo = jnp.concatenate([jnp.zeros((B, 1, HALO, POOL_WIDTH), F32), u4[:, :-1, tm - HALO:, :]], axis=1)
    pool = _pool(u.reshape(B * nt, tm, POOL_WIDTH), halo.reshape(B * nt, HALO, POOL_WIDTH),
                 w_pool, pool_scale, nb=1, tiles_per_seq=nt, pos_base=0, out_dtype=BF16)
    N = B * T
    y = _mlp(x.reshape(N, D_MODEL), o.reshape(N, ATTN_WIDTH), pool.reshape(N, POOL_WIDTH),
             w_out, g_mlp, w_up, w_down, g_final, tm=tm)
    wk = min(WINDOW, T)
    return (y.reshape(B, T, D_MODEL), _rows_from_kt(kc_t), _rows_from_kt(ks_t),
            _rows_from_kt(kw_t[:, :, T - wk:]), u[None, :, T - POOL_HIST:])


def _kt_view(rows):
    n, s = rows.shape[:2]
    return jnp.transpose(rows, (0, 2, 3, 4, 1)).reshape(n, KV_WIDTH, s)


def _sample_path(x, cache_cmp, cache_slc, state_win, state_pool, page_table, wts, g_attn,
                 w_pool, pool_scale, w_out, g_mlp, w_up, w_down, g_final):
    _, wtok_kv, wkv, (pet, w1bd, w2bd) = wts
    Bs, ts, _ = x.shape
    N = Bs * ts
    npg = page_table.shape[1]
    past = npg * PAGE_SIZE
    q, gates, u, _, _, _, _, kv_tok, knew = _proj(x.reshape(1, N, D_MODEL), g_attn, wtok_kv, wkv, tm=N,
                                                  emit_tok_kv=True, emit_pages=False, seq_len=ts)
    comp_past = _compress(page_table.reshape(-1), _kt_view(cache_cmp), pet, w1bd, w2bd, P=min(DECODE_CMP_PAGES, Bs * npg))
    comp_new = _compress(jnp.arange(Bs, dtype=jnp.int32), knew[0], pet, w1bd, w2bd, P=Bs)
    qf = q.reshape(Bs, ts, ATTN_WIDTH).astype(F32)
    gts = gates.reshape(Bs, ts, N_GATE_PAD)
    oc, bits = _s_select(qf, gts, comp_past, comp_new.astype(F32).reshape(N_KV, 2, Bs, 1, PAGE_SIZE), past=past)
    o, win_new = _s_attn(page_table, bits, qf, gts, oc, _kt_view(state_win), knew, _kt_view(cache_slc), past=past)
    u3 = u.reshape(Bs, ts, POOL_WIDTH)
    halo = jnp.pad(state_pool, ((0, 0), (HALO - POOL_HIST, 0), (0, 0)))
    pool = _pool(u3, halo, w_pool, pool_scale, nb=Bs, tiles_per_seq=1, pos_base=past, out_dtype=F32)
    y = _mlp(x.reshape(N, D_MODEL), o.reshape(N, ATTN_WIDTH).astype(BF16), pool.reshape(N, POOL_WIDTH).astype(BF16),
             w_out, g_mlp, w_up, w_down, g_final, tm=N)
    kv_rows = kv_tok.reshape(Bs, ts, 3, N_KV, 2, HEAD_DIM)
    s_pool = jnp.concatenate([state_pool, u3], axis=1)[None, :, ts:]
    return (y.reshape(Bs, ts, D_MODEL), kv_rows[None, :, :, 0], kv_rows[None, :, :, 1],
            _rows_from_kt(win_new), s_pool)


def kernel(x_prompt, x_sample, cache_cmp, cache_slc, state_win, state_pool, page_table, g_attn, w_in,
           cmp_pe, cmp_w1, cmp_w2, w_pool, pool_scale, w_out, g_mlp, w_up, w_down, g_final):
    assert g_attn.shape[0] == 1, "single-layer trunk"
    wts = _prep_weights(w_in[0], cmp_pe[0], cmp_w1[0], cmp_w2[0])
    mlp_w = (w_out[0].astype(BF16), g_mlp, w_up[0].astype(BF16), w_down[0].astype(BF16), g_final[None])
    y_p, p_cmp, p_slc, p_win, p_pool = _prompt_path(x_prompt, wts, g_attn, w_pool[0], pool_scale, *mlp_w)
    y_s, s_cmp, s_slc, s_win, s_pool = _sample_path(x_sample, cache_cmp[0], cache_slc[0], state_win[0],
                                                    state_pool[0], page_table, wts, g_attn, w_pool[0],
                                                    pool_scale, *mlp_w)
    return (y_p, y_s, p_cmp, p_slc, p_win, p_pool, s_cmp, s_slc, s_win, s_pool)
```

```python
import functools
import math

import jax
import jax.numpy as jnp
from jax import lax
from jax.experimental import pallas as pl
from jax.experimental.pallas import tpu as pltpu

D_MODEL = 1024
N_HEADS = 8
N_KV = 2
HPG = N_HEADS // N_KV
HEAD_DIM = 64
ATTN_WIDTH = N_HEADS * HEAD_DIM
POOL_WIDTH = D_MODEL - ATTN_WIDTH
POOL_WINDOWS = (2, 4, 8, 16)
POOL_GROUP = POOL_WIDTH // len(POOL_WINDOWS)
POOL_HIST = max(POOL_WINDOWS) - 1
BLK = 64
N_SELECT = 16
WINDOW = 512
CMP_HIDDEN = 2 * HEAD_DIM
D_FF = 4 * D_MODEL
KV_WIDTH = N_KV * 2 * HEAD_DIM
PAGE_SIZE = 128
BLOCKS_PER_PAGE = PAGE_SIZE // BLK
SCALE = HEAD_DIM ** -0.5
FORCE_SCORE = float(HPG + 1)
NORM_EPS = 1e-6
NEG = -1e30
N_GATE_PAD = 128
HALO = 16

F32 = jnp.float32
BF16 = jnp.bfloat16
NT_DIMS = (((1,), (1,)), ((), ()))

V7X_VMEM_LIMIT = 56 * 1024 * 1024

TOKEN_TILE = 512
ATTN_Q_TILE = 256
ATTN_K_TILE = 512
PROMPT_CMP_PAGES = 64
DECODE_CMP_PAGES = 128
DECODE_PAGE_CHUNK = 8
FF_CHUNK = 1024


def _slope(head):
    return float(2.0 ** (-8.0 * (head + 1) / N_HEADS))


def _cparams(sem, vmem=None, flags=None):
    return pltpu.CompilerParams(dimension_semantics=sem, vmem_limit_bytes=vmem, flags=flags)


def _proj_kernel(x_ref, g_ref, wtok_ref, wkv_ref, q_ref, gate_ref, u_ref, kc_ref, ks_ref, kw_ref, kvtb_ref, *rest,
                 tm, emit_tok_kv, emit_pages, seq_len):
    x = x_ref[0]
    ms = jnp.mean(x * x, axis=-1, keepdims=True)
    xn = ((x * lax.rsqrt(ms + NORM_EPS)) * g_ref[...]).astype(BF16)
    tok = lax.dot_general(xn, wtok_ref[...], NT_DIMS, preferred_element_type=F32)
    q_ref[0] = (tok[:, :ATTN_WIDTH] * SCALE).astype(BF16)
    u_ref[0] = tok[:, ATTN_WIDTH:ATTN_WIDTH + POOL_WIDTH]
    gl = tok[:, ATTN_WIDTH + POOL_WIDTH:ATTN_WIDTH + POOL_WIDTH + N_GATE_PAD]
    gate_ref[0] = 1.0 / (1.0 + jnp.exp(-gl))
    kvt = lax.dot_general(wkv_ref[...], xn, NT_DIMS, preferred_element_type=F32)
    for br, ref in enumerate((kc_ref, ks_ref, kw_ref)):
        ref[0] = kvt[br * KV_WIDTH:(br + 1) * KV_WIDTH]
    for br in range(2):
        kvtb_ref[br, 0] = kvt[(br + 1) * KV_WIDTH:(br + 2) * KV_WIDTH].astype(BF16)
    k = 0
    if emit_tok_kv:
        base = ATTN_WIDTH + POOL_WIDTH + N_GATE_PAD
        rest[k][0] = tok[:, base:base + 3 * KV_WIDTH]
        lane = lax.broadcasted_iota(jnp.int32, (1, PAGE_SIZE), 1)
        kvp = kvt if tm >= PAGE_SIZE else jnp.concatenate(
            [kvt, jnp.zeros((3 * KV_WIDTH, PAGE_SIZE - tm), F32)], axis=1)
        for s in range(tm // seq_len):
            grp, off = divmod(s * seq_len, PAGE_SIZE)
            blk = kvp[:, grp * PAGE_SIZE:(grp + 1) * PAGE_SIZE]
            if off:
                blk = pltpu.roll(blk, PAGE_SIZE - off, 1)
            blk = jnp.where(lane < seq_len, blk, 0.0)
            for br in range(3):
                rest[k + 1][br, s] = blk[br * KV_WIDTH:(br + 1) * KV_WIDTH]
        k += 2
    if emit_pages:
        for pg in range(tm // PAGE_SIZE):
            rest[k][pg] = kvt[:KV_WIDTH, pg * PAGE_SIZE:(pg + 1) * PAGE_SIZE]


def _proj(x, g_attn, wtok, wkv, *, tm, emit_tok_kv, emit_pages, seq_len=None):
    B, T, _ = x.shape
    nt = T // tm
    out_shape = [
        jax.ShapeDtypeStruct((B, T, ATTN_WIDTH), BF16),
        jax.ShapeDtypeStruct((B, T, N_GATE_PAD), F32),
        jax.ShapeDtypeStruct((B, T, POOL_WIDTH), F32),
    ] + [jax.ShapeDtypeStruct((B, KV_WIDTH, T), F32)] * 3 + [
        jax.ShapeDtypeStruct((2, B, KV_WIDTH, T), BF16),
    ]
    out_specs = [
        pl.BlockSpec((1, tm, ATTN_WIDTH), lambda b, i: (b, i, 0)),
        pl.BlockSpec((1, tm, N_GATE_PAD), lambda b, i: (b, i, 0)),
        pl.BlockSpec((1, tm, POOL_WIDTH), lambda b, i: (b, i, 0)),
    ] + [pl.BlockSpec((1, KV_WIDTH, tm), lambda b, i: (b, 0, i))] * 3 + [
        pl.BlockSpec((2, 1, KV_WIDTH, tm), lambda b, i: (0, b, 0, i)),
    ]
    if emit_tok_kv:
        out_shape.append(jax.ShapeDtypeStruct((B, T, 3 * KV_WIDTH), F32))
        out_specs.append(pl.BlockSpec((1, tm, 3 * KV_WIDTH), lambda b, i: (b, i, 0)))
        assert B == 1 and nt == 1 and tm % seq_len == 0
        out_shape.append(jax.ShapeDtypeStruct((3, tm // seq_len, KV_WIDTH, PAGE_SIZE), F32))
        out_specs.append(pl.BlockSpec((3, tm // seq_len, KV_WIDTH, PAGE_SIZE), lambda b, i: (0, 0, 0, 0)))
    if emit_pages:
        ppt = tm // PAGE_SIZE
        out_shape.append(jax.ShapeDtypeStruct((B * T // PAGE_SIZE, KV_WIDTH, PAGE_SIZE), F32))
        out_specs.append(pl.BlockSpec((ppt, KV_WIDTH, PAGE_SIZE), lambda b, i: (b * nt + i, 0, 0)))
    return pl.pallas_call(
        functools.partial(_proj_kernel, tm=tm, emit_tok_kv=emit_tok_kv, emit_pages=emit_pages, seq_len=seq_len),
        grid=(B, nt),
        in_specs=[
            pl.BlockSpec((1, tm, D_MODEL), lambda b, i: (b, i, 0)),
            pl.BlockSpec((1, D_MODEL), lambda b, i: (0, 0)),
            pl.BlockSpec(wtok.shape, lambda b, i: (0, 0)),
            pl.BlockSpec(wkv.shape, lambda b, i: (0, 0)),
        ],
        out_specs=out_specs,
        out_shape=out_shape,
        compiler_params=_cparams(("arbitrary", "arbitrary"), V7X_VMEM_LIMIT),
        name="proj",
    )(x, g_attn, wtok, wkv)


def _gelu_tanh(x):
    c = math.sqrt(2.0 / math.pi)
    return 0.5 * x * (1.0 + jnp.tanh(c * (x + 0.044715 * (x * x * x))))


def _compress_kernel(ids_ref, src_ref, pet_ref, w1_ref, w2_ref, out_ref, buf, sem, *, P, n_steps):
    i = pl.program_id(0)

    def page_copy(step, slot, p):
        return pltpu.make_async_copy(src_ref.at[ids_ref[step * P + p]], buf.at[slot, :, p, :], sem.at[slot])

    def start(step, slot):
        for p in range(P):
            page_copy(step, slot, p).start()

    @pl.when(i == 0)
    def _():
        start(0, 0)

    @pl.when(i + 1 < n_steps)
    def _():
        start(i + 1, (i + 1) % 2)

    slot = i % 2
    for p in range(P):
        page_copy(i, slot, p).wait()

    bref = buf.at[slot]
    for c in range(2):
        accs = [jnp.zeros((P, 2 * CMP_HIDDEN), F32) for _ in range(N_KV)]
        for dp in range(HEAD_DIM // 2):
            zero = jnp.zeros((BLK, CMP_HIDDEN), BF16)
            w_rows = []
            for dd in range(2):
                w = w1_ref[c, dp, dd]
                w_rows += [jnp.concatenate([w, zero], axis=1), jnp.concatenate([zero, w], axis=1)]
            w_tile = jnp.concatenate(w_rows, axis=0)
            for g in range(N_KV):
                halves = []
                for dd in range(2):
                    r = g * 2 * HEAD_DIM + c * HEAD_DIM + 2 * dp + dd
                    pr = c * HEAD_DIM + 2 * dp + dd
                    halves.append(bref[r] + pet_ref[pr:pr + 1, :])
                lhs = jnp.concatenate(halves, axis=1).astype(BF16)
                accs[g] = accs[g] + jnp.dot(lhs, w_tile, preferred_element_type=F32)
        acc = jnp.concatenate(accs, axis=0)
        hid = _gelu_tanh(acc).astype(BF16)
        oc = jnp.dot(hid, w2_ref[c], preferred_element_type=F32).astype(BF16)
        for g in range(N_KV):
            out_ref[g, c] = oc[g * P:(g + 1) * P]


def _compress(ids, src, pet, w1bd, w2bd, *, P):
    n_total = ids.shape[0]
    n_steps = n_total // P
    grid_spec = pltpu.PrefetchScalarGridSpec(
        num_scalar_prefetch=1,
        grid=(n_steps,),
        in_specs=[
            pl.BlockSpec(memory_space=pl.ANY),
            pl.BlockSpec(pet.shape, lambda i, ids: (0, 0)),
            pl.BlockSpec(w1bd.shape, lambda i, ids: (0, 0, 0, 0, 0), pipeline_mode=pl.Buffered(1)),
            pl.BlockSpec(w2bd.shape, lambda i, ids: (0, 0, 0)),
        ],
        out_specs=pl.BlockSpec((N_KV, 2, P, PAGE_SIZE), lambda i, ids: (0, 0, i, 0)),
        scratch_shapes=[
            pltpu.VMEM((2, KV_WIDTH, P, PAGE_SIZE), F32),
            pltpu.SemaphoreType.DMA((2,)),
        ],
    )
    return pl.pallas_call(
        functools.partial(_compress_kernel, P=P, n_steps=n_steps),
        grid_spec=grid_spec,
        out_shape=jax.ShapeDtypeStruct((N_KV, 2, n_total, PAGE_SIZE), BF16),
        compiler_params=_cparams(("arbitrary",), V7X_VMEM_LIMIT),
        name="compress",
    )(ids, src, pet, w1bd, w2bd)


def _compress_weights(cmp_pe, cmp_w1, cmp_w2):
    eye = jnp.eye(BLOCKS_PER_PAGE, dtype=F32)
    pet = jnp.tile(jnp.transpose(cmp_pe, (1, 2, 0)).reshape(2 * HEAD_DIM, BLK), (1, BLOCKS_PER_PAGE))
    w1 = cmp_w1.reshape(2, BLK, HEAD_DIM // 2, 2, CMP_HIDDEN)
    w1bd = jnp.transpose(w1, (0, 2, 3, 1, 4)).astype(BF16)
    w2bd = jnp.einsum('chd,jk->cjhkd', cmp_w2, eye).reshape(
        2, BLOCKS_PER_PAGE * CMP_HIDDEN, BLOCKS_PER_PAGE * HEAD_DIM).astype(BF16)
    return pet, w1bd, w2bd


def _rank_select_t(score, nblk, n_select):
    tq, W = score.shape
    x = score.T
    groups = [x[8 * k:8 * k + 8] for k in range(nblk // 8)]
    ranks = [jnp.zeros((8, tq), jnp.int32) for _ in groups]
    row = lax.broadcasted_iota(jnp.int32, (8, tq), 0)
    for i in range(nblk):
        xi = groups[i // 8][i % 8:i % 8 + 1, :]
        for k, xg in enumerate(groups):
            if 8 * k + 7 < i:
                ahead = xi > xg
            elif 8 * k > i:
                ahead = xi >= xg
            else:
                ahead = (xi > xg) | ((xi == xg) & (row > i - 8 * k))
            ranks[k] = ranks[k] + jnp.where(ahead, 1, 0)
    sel = [jnp.where(r < n_select, 1.0, 0.0) for r in ranks]
    if nblk < W:
        sel.append(jnp.zeros((W - nblk, tq), F32))
    return jnp.concatenate(sel, axis=0).T


def _key_consts(T, npg):
    assert T <= 256 * BLK and npg * BLOCKS_PER_PAGE <= HEAD_DIM
    k = jnp.arange(T, dtype=jnp.int32)[None, :]
    r = jnp.arange(HEAD_DIM, dtype=jnp.int32)[:, None]
    pos = jnp.where(r == 0, k // BLK, jnp.where(r == 1, k % BLK, 0)).astype(F32)
    en = jnp.where(k // BLK == r, NEG, 0.0)
    return jnp.concatenate([pos, en, jnp.zeros((HEAD_DIM, T), F32)], axis=0).astype(BF16)


M_INIT = -3e38


def _p_attn_kernel(q_ref, gate_ref, comp_ref, ks_ref, kw_ref, kc_ref, o_ref,
                   qx, m_ref, acc_ref, out_ref, flag_v, flag_s, flag_sem, *, tq, tk, npg):
    qi = pl.program_id(1)
    q0 = qi * tq
    pos_t = q0 + lax.broadcasted_iota(jnp.int32, (tq, 1), 0)
    nblk = npg * BLOCKS_PER_PAGE
    W = 128
    R = HPG * tq
    lane_h = lax.broadcasted_iota(jnp.int32, (1, HEAD_DIM), 1)

    for hh in range(N_HEADS):
        g, h = divmod(hh, HPG)
        rows = pl.ds(h * tq, tq)
        qx[g, rows, 0:HEAD_DIM] = q_ref[0, :, hh * HEAD_DIM:(hh + 1) * HEAD_DIM]
        posc = jnp.where(lane_h == 0, _slope(hh) * BLK, jnp.where(lane_h == 1, _slope(hh), 0.0))
        qx[g, rows, HEAD_DIM:2 * HEAD_DIM] = jnp.broadcast_to(posc, (tq, HEAD_DIM)).astype(BF16)
        qx[g, rows, 3 * HEAD_DIM:4 * HEAD_DIM] = jnp.zeros((tq, HEAD_DIM), BF16)

    blk_n = lax.broadcasted_iota(jnp.int32, (1, W), 1)
    first = blk_n < nblk
    end_pos = (blk_n + 1) * BLK - 1
    dist_c = (pos_t - end_pos).astype(F32)
    mask_c = (dist_c >= 0) & first
    cur = pos_t // BLK
    row_w = lax.broadcasted_iota(jnp.int32, (W, W), 0)
    col_w = lax.broadcasted_iota(jnp.int32, (W, W), 1)
    perm = jnp.where((col_w == (row_w % 2) * npg + row_w // 2) & (row_w < nblk), 1.0, 0.0).astype(BF16)

    used = []
    for g in range(N_KV):
        ck = comp_ref[g, 0].astype(F32)
        cv = comp_ref[g, 1].astype(F32)
        pad = [jnp.zeros((W - nblk, HEAD_DIM), F32)] if nblk < W else []
        ckp = jnp.concatenate([ck[:, :HEAD_DIM], ck[:, HEAD_DIM:]] + pad, axis=0).astype(BF16)
        cvp = jnp.concatenate([cv[:, :HEAD_DIM], cv[:, HEAD_DIM:]] + pad, axis=0).astype(BF16)
        ckp = jnp.dot(perm, ckp, preferred_element_type=F32).astype(BF16)
        cvp = jnp.dot(perm, cvp, preferred_element_type=F32).astype(BF16)
        imp = jnp.zeros((tq, W), F32)
        for h in range(HPG):
            hh = g * HPG + h
            qh = qx[g, pl.ds(h * tq, tq), 0:HEAD_DIM]
            s = lax.dot_general(qh, ckp, NT_DIMS, preferred_element_type=F32)
            s = jnp.where(mask_c, s - _slope(hh) * dist_c, NEG)
            mx = jnp.max(s, axis=1, keepdims=True)
            e = jnp.where(mask_c, jnp.exp(s - mx), 0.0)
            p = e / jnp.maximum(jnp.sum(e, axis=1, keepdims=True), 1e-30)
            imp = imp + p
            oc = jnp.dot(p.astype(BF16), cvp, preferred_element_type=F32)
            gcol = gate_ref[0, :, hh:hh + 1]
            out_ref[:, hh * HEAD_DIM:(hh + 1) * HEAD_DIM] = gcol * oc
        forced = (blk_n == 0) | (blk_n == cur) | (blk_n == cur - 1)
        score = jnp.where(forced, FORCE_SCORE, imp)
        score = jnp.where(blk_n > cur, -1.0, score)
        sel = _rank_select_t(score, nblk, N_SELECT)
        picked = (sel > 0.5) & (blk_n <= cur)
        notsel = jnp.where(picked, 0.0, 1.0).astype(BF16)
        for h in range(HPG):
            qx[g, pl.ds(h * tq, tq), 2 * HEAD_DIM:3 * HEAD_DIM] = notsel[:, :HEAD_DIM]
        used.append(jnp.max(jnp.where(picked, 1, 0), axis=0, keepdims=True))

    flag_v[...] = jnp.concatenate(used + [jnp.zeros((8 - N_KV, W), jnp.int32)], axis=0)
    flag_copy = pltpu.make_async_copy(flag_v, flag_s, flag_sem.at[0])
    flag_copy.start()

    pos_rows = q0 + lax.broadcasted_iota(jnp.int32, (R, 1), 0) % tq

    ones_rows = jnp.where(lax.broadcasted_iota(jnp.int32, (HEAD_DIM, 1), 0) == 0, 1.0, 0.0)

    def attend_tile(kv_ref, k0, width, kc_rows, mode, groups=tuple(range(N_KV))):
        kpos = k0 + lax.broadcasted_iota(jnp.int32, (1, width), 1)
        kc_tile = kc_ref[0:kc_rows, pl.ds(k0, width)]
        ones_tile = jnp.broadcast_to(ones_rows, (HEAD_DIM, width)).astype(BF16)
        if mode == "causal":
            keep = kpos <= pos_rows
        elif mode == "recent":
            keep = kpos > pos_rows - WINDOW
        for g in groups:
            kt = kv_ref[0, 0, g * 2 * HEAD_DIM:g * 2 * HEAD_DIM + HEAD_DIM, pl.ds(k0, width)]
            kext = jnp.concatenate([kt, kc_tile], axis=0)
            vt = kv_ref[0, 0, g * 2 * HEAD_DIM + HEAD_DIM:(g + 1) * 2 * HEAD_DIM, pl.ds(k0, width)]
            vext = jnp.concatenate([vt, ones_tile], axis=0)
            s = jnp.dot(qx[g, :, 0:HEAD_DIM + kc_rows], kext, preferred_element_type=F32)
            if mode != "full":
                s = jnp.where(keep, s, NEG)
            m_prev = m_ref[g]
            m_next = jnp.maximum(m_prev, jnp.max(s, axis=1, keepdims=True))
            alpha = jnp.exp(m_prev - m_next)
            m_ref[g] = m_next
            p = jnp.exp(s - jnp.concatenate([m_next] * (width // W), axis=1)).astype(BF16)
            pv = lax.dot_general(p, vext, NT_DIMS, preferred_element_type=F32)
            acc_ref[g] = alpha * acc_ref[g] + pv

    def reset():
        for g in range(N_KV):
            m_ref[g] = jnp.full((R, W), M_INIT, F32)
            acc_ref[g] = jnp.zeros((R, W), F32)

    def finish(gate_base):
        for g in range(N_KV):
            for h in range(HPG):
                hh = g * HPG + h
                a = acc_ref[g, pl.ds(h * tq, tq), :]
                o = a[:, :HEAD_DIM] / jnp.maximum(a[:, HEAD_DIM:HEAD_DIM + 1], 1e-30)
                gcol = gate_ref[0, :, gate_base + hh:gate_base + hh + 1]
                out_ref[:, hh * HEAD_DIM:(hh + 1) * HEAD_DIM] += gcol * o

    reset()
    wt = WINDOW // tq

    @pl.when(qi >= wt)
    def _():
        attend_tile(kw_ref, pl.multiple_of((qi - wt) * tq, tq), WINDOW, HEAD_DIM, "recent")

    @pl.when(qi < wt)
    def _():
        def win_body(j, carry):
            attend_tile(kw_ref, pl.multiple_of(j * tq, tq), tq, HEAD_DIM, "full")
            return carry

        lax.fori_loop(0, qi, win_body, 0)

    attend_tile(kw_ref, pl.multiple_of(q0, tq), tq, HEAD_DIM, "causal")
    finish(2 * N_HEADS)

    reset()
    flag_copy.wait()
    n_sel = (q0 + tq + tk - 1) // tk
    bpt = tk // BLK

    def sel_body(j, carry):
        hits = []
        for g in range(N_KV):
            hit = flag_s[g, j * bpt]
            for u in range(1, bpt):
                hit = hit | flag_s[g, j * bpt + u]
            hits.append(hit != 0)
        k0 = pl.multiple_of(j * tk, tk)

        @pl.when(hits[0] & hits[1])
        def _():
            attend_tile(ks_ref, k0, tk, 3 * HEAD_DIM, "full")

        @pl.when(hits[0] & jnp.logical_not(hits[1]))
        def _():
            attend_tile(ks_ref, k0, tk, 3 * HEAD_DIM, "full", groups=(0,))

        @pl.when(jnp.logical_not(hits[0]) & hits[1])
        def _():
            attend_tile(ks_ref, k0, tk, 3 * HEAD_DIM, "full", groups=(1,))

        return carry

    lax.fori_loop(0, n_sel - 1, sel_body, 0)
    attend_tile(ks_ref, pl.multiple_of((n_sel - 1) * tk, tk), tk, 3 * HEAD_DIM, "causal")
    finish(N_HEADS)
    o_ref[0] = out_ref[...].astype(BF16)


def _p_attn(q, gates, comp, kvtb, kconst, *, tq, tk):
    B, T, _ = q.shape
    npg = T // PAGE_SIZE
    assert WINDOW % tq == 0 and tk % tq == 0
    return pl.pallas_call(
        functools.partial(_p_attn_kernel, tq=tq, tk=tk, npg=npg),
        grid=(B, T // tq),
        in_specs=[
            pl.BlockSpec((1, tq, ATTN_WIDTH), lambda b, i: (b, i, 0)),
            pl.BlockSpec((1, tq, N_GATE_PAD), lambda b, i: (b, i, 0)),
            pl.BlockSpec((N_KV, 2, npg, PAGE_SIZE), lambda b, i: (0, 0, b, 0)),
            pl.BlockSpec((1, 1, KV_WIDTH, T), lambda b, i: (0, b, 0, 0)),
            pl.BlockSpec((1, 1, KV_WIDTH, T), lambda b, i: (1, b, 0, 0)),
            pl.BlockSpec(kconst.shape, lambda b, i: (0, 0)),
        ],
        out_specs=pl.BlockSpec((1, tq, ATTN_WIDTH), lambda b, i: (b, i, 0)),
        out_shape=jax.ShapeDtypeStruct((B, T, ATTN_WIDTH), BF16),
        scratch_shapes=[
            pltpu.VMEM((N_KV, HPG * tq, 4 * HEAD_DIM), BF16),
            pltpu.VMEM((N_KV, HPG * tq, 128), F32),
            pltpu.VMEM((N_KV, HPG * tq, 128), F32),
            pltpu.VMEM((tq, ATTN_WIDTH), F32),
            pltpu.VMEM((8, 128), jnp.int32),
            pltpu.SMEM((8, 128), jnp.int32),
            pltpu.SemaphoreType.DMA((1,)),
        ],
        compiler_params=_cparams(("arbitrary", "arbitrary"), V7X_VMEM_LIMIT),
        name="p_attn",
    )(q, gates, comp, kvtb, kvtb, kconst)


def _stack_heads(q_ref, g, ts):
    parts = [q_ref[0, :, (g * HPG + h) * HEAD_DIM:(g * HPG + h + 1) * HEAD_DIM].astype(F32) for h in range(HPG)]
    return jnp.concatenate(parts, axis=0).astype(BF16)


def _row_consts(g, ts):
    R = HPG * ts
    row = lax.broadcasted_iota(jnp.int32, (R, 1), 0)
    t_row = row % ts
    h_row = row // ts
    slope = jnp.zeros((R, 1), F32)
    for h in range(HPG):
        slope = jnp.where(h_row == h, _slope(g * HPG + h), slope)
    return t_row, slope


def _s_select_kernel(q_ref, gate_ref, comp_ref, cnew_ref, oc_ref, bits_ref, out_s, *, past, ts):
    W = 128
    nb_past = past // BLK
    lane = lax.broadcasted_iota(jnp.int32, (1, W), 1)
    idx_fns = [lambda c: 2 * c, lambda c: 2 * c + 1, lambda c: jnp.where(c == 0, nb_past, (1 << 20) + c)]
    idx_tiles = [f(lane) for f in idx_fns]
    valid_n = lane == 0
    tok = lax.broadcasted_iota(jnp.int32, (ts, 1), 0)
    cur_t = (past + tok) // BLK
    for g in range(N_KV):
        qg = _stack_heads(q_ref, g, ts)
        t_row, slope = _row_consts(g, ts)
        pos = past + t_row
        ck = comp_ref[g, 0]
        cv = comp_ref[g, 1]
        cn = cnew_ref[g, 0, 0].astype(F32)[:, :HEAD_DIM]
        vn = cnew_ref[g, 1, 0].astype(F32)[:, :HEAD_DIM]
        s_t = [lax.dot_general(qg, ck[:, :HEAD_DIM], NT_DIMS, preferred_element_type=F32),
               lax.dot_general(qg, ck[:, HEAD_DIM:], NT_DIMS, preferred_element_type=F32),
               jnp.broadcast_to(jnp.sum(qg.astype(F32) * cn, axis=1, keepdims=True), (HPG * ts, W))]
        masks, es = [], []
        for k in range(3):
            end_pos = (idx_tiles[k] + 1) * BLK - 1
            dist = (pos - end_pos).astype(F32)
            mk = dist >= 0
            if k == 2:
                mk = mk & valid_n
            masks.append(mk)
            s_t[k] = jnp.where(mk, s_t[k] - slope * dist, NEG)
        mx = jnp.maximum(jnp.maximum(jnp.max(s_t[0], axis=1, keepdims=True),
                                     jnp.max(s_t[1], axis=1, keepdims=True)),
                         jnp.max(s_t[2], axis=1, keepdims=True))
        for k in range(3):
            es.append(jnp.where(masks[k], jnp.exp(s_t[k] - mx), 0.0))
        den = (jnp.sum(es[0], axis=1, keepdims=True) + jnp.sum(es[1], axis=1, keepdims=True)
               + jnp.sum(es[2], axis=1, keepdims=True))
        inv = 1.0 / jnp.maximum(den, 1e-30)
        ps = [e * inv for e in es]
        o_c = (jnp.dot(ps[0].astype(BF16), cv[:, :HEAD_DIM], preferred_element_type=F32)
               + jnp.dot(ps[1].astype(BF16), cv[:, HEAD_DIM:], preferred_element_type=F32)
               + ps[2][:, 0:1].astype(BF16).astype(F32) * vn)
        for h in range(HPG):
            hh = g * HPG + h
            gcol = gate_ref[0, :, hh:hh + 1]
            out_s[:, hh * HEAD_DIM:(hh + 1) * HEAD_DIM] = gcol * o_c[h * ts:(h + 1) * ts]
        scores = []
        for k in range(3):
            imp = ps[k][0:ts]
            for h in range(1, HPG):
                imp = imp + ps[k][h * ts:(h + 1) * ts]
            idx = idx_tiles[k]
            forced = (idx == 0) | (idx == cur_t) | (idx == cur_t - 1)
            sc = jnp.where(forced, FORCE_SCORE, imp)
            sc = jnp.where(idx > cur_t, -1.0, sc)
            if k == 2:
                sc = jnp.where(valid_n, sc, -2.0)
            scores.append(sc)
        s_new = scores[2][:, 0:1]
        parts = [[jnp.where(s_new > scores[ka], 1, 0), jnp.zeros((ts, W), jnp.int32)] for ka in range(2)]
        for kb in range(2):
            for r in range(W):
                y = pltpu.roll(scores[kb], r, 1) if r else scores[kb]
                yi = idx_fns[kb]((lane - r) & (W - 1))
                for ka in range(2):
                    if ka == kb and r == 0:
                        continue
                    ahead = (y > scores[ka]) | ((y == scores[ka]) & (yi < idx_tiles[ka]))
                    parts[ka][r % 2] = parts[ka][r % 2] + jnp.where(ahead, 1, 0)
        sel = [(parts[k][0] + parts[k][1]) < N_SELECT for k in range(2)]
        rank_new = (jnp.sum(jnp.where(scores[0] >= s_new, 1, 0), axis=1, keepdims=True)
                    + jnp.sum(jnp.where(scores[1] >= s_new, 1, 0), axis=1, keepdims=True))
        wt = jnp.left_shift(1, 2 * tok)
        page_bits = jnp.sum(jnp.where(sel[0], wt, 0) + jnp.where(sel[1], 2 * wt, 0), axis=0, keepdims=True)
        new_bits = jnp.sum(jnp.where((rank_new < N_SELECT) & valid_n, jnp.left_shift(1, tok), 0),
                           axis=0, keepdims=True)
        bits_ref[0, g, 0:1, :] = page_bits
        bits_ref[0, g, 1:2, :] = new_bits
    oc_ref[0] = out_s[...]


def _s_select(q, gates, comp, cnew, *, past):
    Bs, ts, _ = q.shape
    npg = past // PAGE_SIZE
    assert npg == 128, "one lane tile of pages per sequence"
    return pl.pallas_call(
        functools.partial(_s_select_kernel, past=past, ts=ts),
        grid=(Bs,),
        in_specs=[
            pl.BlockSpec((1, ts, ATTN_WIDTH), lambda b: (b, 0, 0)),
            pl.BlockSpec((1, ts, N_GATE_PAD), lambda b: (b, 0, 0)),
            pl.BlockSpec((N_KV, 2, npg, PAGE_SIZE), lambda b: (0, 0, b, 0)),
            pl.BlockSpec((N_KV, 2, 1, 1, PAGE_SIZE), lambda b: (0, 0, b, 0, 0)),
        ],
        out_specs=[
            pl.BlockSpec((1, ts, ATTN_WIDTH), lambda b: (b, 0, 0)),
            pl.BlockSpec((1, N_KV, 2, 128), lambda b: (b, 0, 0, 0)),
        ],
        out_shape=[
            jax.ShapeDtypeStruct((Bs, ts, ATTN_WIDTH), F32),
            jax.ShapeDtypeStruct((Bs, N_KV, 2, 128), jnp.int32),
        ],
        scratch_shapes=[pltpu.VMEM((ts, ATTN_WIDTH), F32)],
        compiler_params=_cparams(("arbitrary",), V7X_VMEM_LIMIT),
        name="s_select",
    )(q, gates, comp, cnew)


def _s_attn_kernel(pt_ref, bits_ref, q_ref, gate_ref, oc_ref, win_ref, knew_ref, slc_ref,
                   o_ref, wout_ref, buf, sem, plist, cnts, out_s, *, past, ts, npg, chunk):
    b = pl.program_id(0)
    g = pl.program_id(1)
    W = 128
    R = HPG * ts
    wk = win_ref.shape[2]
    lin = b * N_KV + g
    n_lin = pl.num_programs(0) * N_KV
    base = lin * 2 * W
    slot = lin % 2

    def page_copy(page, gg, sl, idx):
        return pltpu.make_async_copy(slc_ref.at[page, pl.ds(gg * 2 * HEAD_DIM, 2 * HEAD_DIM)],
                                     buf.at[sl, idx], sem.at[sl])

    def issue_all(ln, sl):
        bb = ln // N_KV
        gg = ln % N_KV

        def scan(p, cnt):
            plist[sl * npg + cnt] = p
            return cnt + (bits_ref[ln * 2 * W + p] != 0).astype(jnp.int32)

        cnt = lax.fori_loop(0, npg, scan, 0, unroll=8)
        cnts[sl] = cnt

        def start(s, c):
            page_copy(pt_ref[bb * npg + plist[sl * npg + s]], gg, sl, s).start()
            return c

        lax.fori_loop(0, cnt, start, 0)

    @pl.when(lin == 0)
    def _():
        buf[...] = jnp.zeros(buf.shape, F32)
        for k in range(2 * npg):
            plist[k] = 0
        issue_all(lin, slot)

    @pl.when(lin + 1 < n_lin)
    def _():
        issue_all(lin + 1, 1 - slot)

    cnt = cnts[slot]

    q_all = q_ref[0].astype(F32)
    qg = jnp.concatenate([q_all[:, h * HEAD_DIM:(h + 1) * HEAD_DIM] for h in range(HPG)], axis=0).astype(BF16)
    row = lax.broadcasted_iota(jnp.int32, (R, 1), 0)
    t_row = row % ts
    head_row = row // ts + g * HPG
    slope = jnp.zeros((R, 1), F32)
    for hh in range(N_HEADS):
        slope = jnp.where(head_row == hh, _slope(hh), slope)
    lane = lax.broadcasted_iota(jnp.int32, (1, W), 1)

    kw = win_ref[0, 0:HEAD_DIM, :].astype(BF16)
    vw = win_ref[0, HEAD_DIM:2 * HEAD_DIM, :].astype(BF16)
    kn = knew_ref[2, 0, 0:HEAD_DIM, :].astype(BF16)
    vn = knew_ref[2, 0, HEAD_DIM:2 * HEAD_DIM, :].astype(BF16)
    i_st = lax.broadcasted_iota(jnp.int32, (1, wk), 1)
    d_st = t_row + wk - i_st
    m_st = (d_st >= 0) & (d_st < WINDOW)
    d_nw = t_row - lane
    m_nw = (d_nw >= 0) & (d_nw < WINDOW) & (lane < ts)
    s_st = jnp.where(m_st, jnp.dot(qg, kw, preferred_element_type=F32) - slope * d_st.astype(F32), NEG)
    s_nw = jnp.where(m_nw, jnp.dot(qg, kn, preferred_element_type=F32) - slope * d_nw.astype(F32), NEG)
    mx = jnp.maximum(jnp.max(s_st, axis=1, keepdims=True), jnp.max(s_nw, axis=1, keepdims=True))
    e_st = jnp.where(m_st, jnp.exp(s_st - mx), 0.0)
    e_nw = jnp.where(m_nw, jnp.exp(s_nw - mx), 0.0)
    den = jnp.sum(e_st, axis=1, keepdims=True) + jnp.sum(e_nw, axis=1, keepdims=True)
    inv = 1.0 / jnp.maximum(den, 1e-30)
    o_w = (lax.dot_general((e_st * inv).astype(BF16), vw, NT_DIMS, preferred_element_type=F32)
           + lax.dot_general((e_nw * inv).astype(BF16), vn, NT_DIMS, preferred_element_type=F32))

    last = wk - W
    shifted = pltpu.roll(win_ref[0], wk - ts, 1)
    newr = pltpu.roll(knew_ref[2, 0], W - ts, 1)
    wout_ref[0, :, 0:last] = shifted[:, 0:last]
    wout_ref[0, :, last:wk] = jnp.where(lane >= W - ts, newr, shifted[:, last:wk])

    bits_new = bits_ref[base + W]
    ksn = knew_ref[1, 0, 0:HEAD_DIM, :].astype(BF16)
    vsn = knew_ref[1, 0, HEAD_DIM:2 * HEAD_DIM, :].astype(BF16)
    d_sn = t_row - lane
    m_sn = (d_sn >= 0) & (lane < ts) & ((jnp.right_shift(bits_new, t_row) & 1) == 1)
    s_sn = jnp.where(m_sn, jnp.dot(qg, ksn, preferred_element_type=F32) - slope * d_sn.astype(F32), NEG)
    m0 = jnp.max(s_sn, axis=1, keepdims=True)
    e0 = jnp.where(m_sn, jnp.exp(s_sn - m0), 0.0)
    l0 = jnp.sum(e0, axis=1, keepdims=True)
    a0 = lax.dot_general(e0.astype(BF16), vsn, NT_DIMS, preferred_element_type=F32)

    def wait_one(s, c):
        page_copy(0, 0, slot, s).wait()
        return c

    lax.fori_loop(0, cnt, wait_one, 0)
    sh = 2 * t_row + (lane >= BLK).astype(jnp.int32)
    sh_c = jnp.concatenate([sh] * chunk, axis=1)
    bslot = buf.at[slot]

    def chunk_step(c, carry):
        m, l, acc = carry
        kts, vts, kpos, bitv = [], [], [], []
        for u in range(chunk):
            s = c * chunk + u
            ok = s < cnt
            p = jnp.where(ok, plist[slot * npg + s], 0)
            bits = jnp.where(ok, bits_ref[base + p], 0)
            kts.append(bslot[s, 0:HEAD_DIM, :])
            vts.append(bslot[s, HEAD_DIM:2 * HEAD_DIM, :])
            kpos.append(p * PAGE_SIZE + lane)
            bitv.append(jnp.zeros((1, W), jnp.int32) + bits)
        kt = jnp.concatenate(kts, axis=1).astype(BF16)
        vt = jnp.concatenate(vts, axis=1).astype(BF16)
        dist = past + t_row - jnp.concatenate(kpos, axis=1)
        mask = ((jnp.right_shift(jnp.concatenate(bitv, axis=1), sh_c) & 1) == 1) & (dist >= 0)
        sc = jnp.where(mask, jnp.dot(qg, kt, preferred_element_type=F32) - slope * dist.astype(F32), NEG)
        m_new = jnp.maximum(m, jnp.max(sc, axis=1, keepdims=True))
        alpha = jnp.exp(m - m_new)
        e = jnp.where(mask, jnp.exp(sc - m_new), 0.0)
        l = alpha * l + jnp.sum(e, axis=1, keepdims=True)
        acc = alpha * acc + lax.dot_general(e.astype(BF16), vt, NT_DIMS, preferred_element_type=F32)
        return m_new, l, acc

    m, l, acc = lax.fori_loop(0, (cnt + chunk - 1) // chunk, chunk_step, (m0, l0, a0))
    o_s = acc / jnp.maximum(l, 1e-30)

    g_vec = jnp.zeros((ts, 1), jnp.int32) + g
    for h in range(HPG):
        rows = slice(h * ts, (h + 1) * ts)
        g_s = jnp.zeros((ts, 1), F32)
        g_w = jnp.zeros((ts, 1), F32)
        for gg in range(N_KV):
            hh = gg * HPG + h
            g_s = jnp.where(g_vec == gg, gate_ref[0, :, N_HEADS + hh:N_HEADS + hh + 1], g_s)
            g_w = jnp.where(g_vec == gg, gate_ref[0, :, 2 * N_HEADS + hh:2 * N_HEADS + hh + 1], g_w)
        out_s[:, h * HEAD_DIM:(h + 1) * HEAD_DIM] = g_s * o_s[rows] + g_w * o_w[rows]
    o_ref[0] = oc_ref[0] + out_s[...]


def _s_attn(page_table, bits, q, gates, oc, win_t, knew, slc_t, *, past):
    Bs, ts, _ = q.shape
    npg = past // PAGE_SIZE
    wk = win_t.shape[2]
    gw = HPG * HEAD_DIM
    grid_spec = pltpu.PrefetchScalarGridSpec(
        num_scalar_prefetch=2,
        grid=(Bs, N_KV),
        in_specs=[
            pl.BlockSpec((1, ts, gw), lambda b, g, *_: (b, 0, g)),
            pl.BlockSpec((1, ts, N_GATE_PAD), lambda b, g, *_: (b, 0, 0)),
            pl.BlockSpec((1, ts, gw), lambda b, g, *_: (b, 0, g)),
            pl.BlockSpec((1, 2 * HEAD_DIM, wk), lambda b, g, *_: (b, g, 0)),
            pl.BlockSpec((3, 1, 2 * HEAD_DIM, PAGE_SIZE), lambda b, g, *_: (0, b, g, 0)),
            pl.BlockSpec(memory_space=pl.ANY),
        ],
        out_specs=[
            pl.BlockSpec((1, ts, gw), lambda b, g, *_: (b, 0, g)),
            pl.BlockSpec((1, 2 * HEAD_DIM, wk), lambda b, g, *_: (b, g, 0)),
        ],
        scratch_shapes=[
            pltpu.VMEM((2, npg, 2 * HEAD_DIM, PAGE_SIZE), F32),
            pltpu.SemaphoreType.DMA((2,)),
            pltpu.SMEM((2 * npg,), jnp.int32),
            pltpu.SMEM((2,), jnp.int32),
            pltpu.VMEM((ts, gw), F32),
        ],
    )
    chunk = DECODE_PAGE_CHUNK
    assert npg % chunk == 0
    return pl.pallas_call(
        functools.partial(_s_attn_kernel, past=past, ts=ts, npg=npg, chunk=chunk),
        grid_spec=grid_spec,
        out_shape=[
            jax.ShapeDtypeStruct((Bs, ts, ATTN_WIDTH), F32),
            jax.ShapeDtypeStruct(win_t.shape, F32),
        ],
        compiler_params=_cparams(("arbitrary", "arbitrary"), V7X_VMEM_LIMIT),
        name="s_attn",
    )(page_table.reshape(-1), bits.reshape(-1), q, gates, oc, win_t, knew, slc_t)


def _pool_kernel(u_ref, halo_ref, wp_ref, ps_ref, o_ref, ext, *, nb, tp, tiles_per_seq, pos_base):
    i = pl.program_id(0)
    ext[:, 0:HALO, :] = halo_ref[...]
    ext[:, HALO:HALO + tp, :] = u_ref[...]
    tile = i * nb + lax.broadcasted_iota(jnp.int32, (nb, tp, 1), 0)
    pos = pos_base + (tile % tiles_per_seq) * tp + lax.broadcasted_iota(jnp.int32, (nb, tp, 1), 1)
    for gi, w in enumerate(POOL_WINDOWS):
        lanes = slice(gi * POOL_GROUP, (gi + 1) * POOL_GROUP)
        own = ext[:, HALO:HALO + tp, lanes]
        tot = ext[:, :, lanes]
        span = 1
        while span < w:
            tot = tot[:, span:, :] + tot[:, :-span, :]
            span *= 2
        tot = tot[:, HALO - w + 1:HALO - w + 1 + tp, :]
        cnt = jnp.minimum(w, pos + 1).astype(F32)
        d = (tot / cnt - own).astype(BF16).reshape(nb * tp, POOL_GROUP)
        y = jnp.dot(d, wp_ref[gi], preferred_element_type=F32) * ps_ref[:, lanes]
        o_ref[:, :, lanes] = y.reshape(nb, tp, POOL_GROUP).astype(o_ref.dtype)


def _pool(u, halo, w_pool, pool_scale, *, nb, tiles_per_seq, pos_base, out_dtype):
    n_tiles, tp, _ = u.shape
    return pl.pallas_call(
        functools.partial(_pool_kernel, nb=nb, tp=tp, tiles_per_seq=tiles_per_seq, pos_base=pos_base),
        grid=(n_tiles // nb,),
        in_specs=[
            pl.BlockSpec((nb, tp, POOL_WIDTH), lambda i: (i, 0, 0)),
            pl.BlockSpec((nb, HALO, POOL_WIDTH), lambda i: (i, 0, 0)),
            pl.BlockSpec(w_pool.shape, lambda i: (0, 0, 0)),
            pl.BlockSpec((1, POOL_WIDTH), lambda i: (0, 0)),
        ],
        out_specs=pl.BlockSpec((nb, tp, POOL_WIDTH), lambda i: (i, 0, 0)),
        out_shape=jax.ShapeDtypeStruct((n_tiles, tp, POOL_WIDTH), out_dtype),
        scratch_shapes=[pltpu.VMEM((nb, HALO + tp, POOL_WIDTH), F32)],
        compiler_params=_cparams(("arbitrary",), V7X_VMEM_LIMIT),
        name="pool",
    )(u, halo, w_pool, pool_scale)


def _rms(x, g):
    return (x * lax.rsqrt(jnp.mean(x * x, axis=-1, keepdims=True) + NORM_EPS)) * g


def _mlp_kernel(x_ref, o_ref, p_ref, wo_ref, gm_ref, wu_ref, wd_ref, gf_ref, y_ref, *, ff_chunk):
    mix = jnp.concatenate([o_ref[...], p_ref[...]], axis=1)
    h = x_ref[...] + jnp.dot(mix, wo_ref[...], preferred_element_type=F32)
    hn = _rms(h, gm_ref[...]).astype(BF16)
    acc = jnp.zeros(h.shape, F32)
    for c in range(D_FF // ff_chunk):
        a = jnp.dot(hn, wu_ref[:, c * ff_chunk:(c + 1) * ff_chunk], preferred_element_type=F32)
        a = jnp.square(jnp.maximum(a, 0.0)).astype(BF16)
        acc = acc + jnp.dot(a, wd_ref[c * ff_chunk:(c + 1) * ff_chunk, :], preferred_element_type=F32)
    y_ref[...] = _rms(h + acc, gf_ref[...])


def _mlp(x, o, p, w_out, g_mlp, w_up, w_down, g_final, *, tm):
    N = x.shape[0]
    const = lambda i: (0, 0)
    resident = lambda a: pl.BlockSpec(a.shape, const, pipeline_mode=pl.Buffered(1))
    return pl.pallas_call(
        functools.partial(_mlp_kernel, ff_chunk=FF_CHUNK),
        grid=(N // tm,),
        in_specs=[
            pl.BlockSpec((tm, D_MODEL), lambda i: (i, 0)),
            pl.BlockSpec((tm, ATTN_WIDTH), lambda i: (i, 0)),
            pl.BlockSpec((tm, POOL_WIDTH), lambda i: (i, 0)),
            resident(w_out), resident(g_mlp), resident(w_up), resident(w_down), resident(g_final),
        ],
        out_specs=pl.BlockSpec((tm, D_MODEL), lambda i: (i, 0)),
        out_shape=jax.ShapeDtypeStruct((N, D_MODEL), F32),
        compiler_params=_cparams(("arbitrary",), V7X_VMEM_LIMIT),
        name="mlp",
    )(x, o, p, w_out, g_mlp, w_up, w_down, g_final)


def _rows_from_kt(kt):
    B, _, S = kt.shape
    return jnp.transpose(kt.reshape(B, N_KV, 2, HEAD_DIM, S), (0, 4, 1, 2, 3))[None]


def _prep_weights(w_in, cmp_pe, cmp_w1, cmp_w2):
    w_t = jnp.transpose(w_in)
    a, b = ATTN_WIDTH, ATTN_WIDTH + 3 * KV_WIDTH
    wq, wkv = w_t[:a], w_t[a:b]
    wg, wu = w_t[b:b + 3 * N_HEADS], w_t[b + 3 * N_HEADS:]
    wg = jnp.pad(wg, ((0, N_GATE_PAD - 3 * N_HEADS), (0, 0)))
    wtok = jnp.concatenate([wq, wu, wg], axis=0).astype(BF16)
    wtok_kv = jnp.concatenate([wq, wu, wg, wkv], axis=0).astype(BF16)
    return wtok, wtok_kv, wkv.astype(BF16), _compress_weights(cmp_pe, cmp_w1, cmp_w2)


def _prompt_path(x, wts, g_attn, w_pool, pool_scale, w_out, g_mlp, w_up, w_down, g_final):
    wtok, _, wkv, (pet, w1bd, w2bd) = wts
    B, T, _ = x.shape
    tm = min(TOKEN_TILE, T)
    q, gates, u, kc_t, ks_t, kw_t, kvtb, pages = _proj(x, g_attn, wtok, wkv, tm=tm, emit_tok_kv=False,
                                                        emit_pages=True)
    n_pages = B * T // PAGE_SIZE
    comp = _compress(jnp.arange(n_pages, dtype=jnp.int32), pages, pet, w1bd, w2bd, P=min(PROMPT_CMP_PAGES, n_pages))
    o = _p_attn(q, gates, comp, kvtb, _key_consts(T, T // PAGE_SIZE), tq=min(ATTN_Q_TILE, T), tk=min(ATTN_K_TILE, T))
    nt = T // tm
    u4 = u.reshape(B, nt, tm, POOL_WIDTH)
    halo = jnp.concatenate([jnp.zeros((B, 1, HALO, POOL_WIDTH), F32), u4[:, :-1, tm - HALO:, :]], axis=1)
    pool = _pool(u.reshape(B * nt, tm, POOL_WIDTH), halo.reshape(B * nt, HALO, POOL_WIDTH),
                 w_pool, pool_scale, nb=1, tiles_per_seq=nt, pos_base=0, out_dtype=BF16)
    N = B * T
    y = _mlp(x.reshape(N, D_MODEL), o.reshape(N, ATTN_WIDTH), pool.reshape(N, POOL_WIDTH),
             w_out, g_mlp, w_up, w_down, g_final, tm=tm)
    wk = min(WINDOW, T)
    return (y.reshape(B, T, D_MODEL), _rows_from_kt(kc_t), _rows_from_kt(ks_t),
            _rows_from_kt(kw_t[:, :, T - wk:]), u[None, :, T - POOL_HIST:])


def _kt_view(rows):
    n, s = rows.shape[:2]
    return jnp.transpose(rows, (0, 2, 3, 4, 1)).reshape(n, KV_WIDTH, s)


def _sample_path(x, cache_cmp, cache_slc, state_win, state_pool, page_table, wts, g_attn,
                 w_pool, pool_scale, w_out, g_mlp, w_up, w_down, g_final):
    _, wtok_kv, wkv, (pet, w1bd, w2bd) = wts
    Bs, ts, _ = x.shape
    N = Bs * ts
    npg = page_table.shape[1]
    past = npg * PAGE_SIZE
    q, gates, u, _, _, _, _, kv_tok, knew = _proj(x.reshape(1, N, D_MODEL), g_attn, wtok_kv, wkv, tm=N,
                                                  emit_tok_kv=True, emit_pages=False, seq_len=ts)
    comp_past = _compress(page_table.reshape(-1), _kt_view(cache_cmp), pet, w1bd, w2bd, P=min(DECODE_CMP_PAGES, Bs * npg))
    comp_new = _compress(jnp.arange(Bs, dtype=jnp.int32), knew[0], pet, w1bd, w2bd, P=Bs)
    qf = q.reshape(Bs, ts, ATTN_WIDTH).astype(F32)
    gts = gates.reshape(Bs, ts, N_GATE_PAD)
    oc, bits = _s_select(qf, gts, comp_past, comp_new.astype(F32).reshape(N_KV, 2, Bs, 1, PAGE_SIZE), past=past)
    o, win_new = _s_attn(page_table, bits, qf, gts, oc, _kt_view(state_win), knew, _kt_view(cache_slc), past=past)
    u3 = u.reshape(Bs, ts, POOL_WIDTH)
    halo = jnp.pad(state_pool, ((0, 0), (HALO - POOL_HIST, 0), (0, 0)))
    pool = _pool(u3, halo, w_pool, pool_scale, nb=Bs, tiles_per_seq=1, pos_base=past, out_dtype=F32)
    y = _mlp(x.reshape(N, D_MODEL), o.reshape(N, ATTN_WIDTH).astype(BF16), pool.reshape(N, POOL_WIDTH).astype(BF16),
             w_out, g_mlp, w_up, w_down, g_final, tm=N)
    kv_rows = kv_tok.reshape(Bs, ts, 3, N_KV, 2, HEAD_DIM)
    s_pool = jnp.concatenate([state_pool, u3], axis=1)[None, :, ts:]
    return (y.reshape(Bs, ts, D_MODEL), kv_rows[None, :, :, 0], kv_rows[None, :, :, 1],
            _rows_from_kt(win_new), s_pool)


def kernel(x_prompt, x_sample, cache_cmp, cache_slc, state_win, state_pool, page_table, g_attn, w_in,
           cmp_pe, cmp_w1, cmp_w2, w_pool, pool_scale, w_out, g_mlp, w_up, w_down, g_final):
    assert g_attn.shape[0] == 1, "single-layer trunk"
    wts = _prep_weights(w_in[0], cmp_pe[0], cmp_w1[0], cmp_w2[0])
    mlp_w = (w_out[0].astype(BF16), g_mlp, w_up[0].astype(BF16), w_down[0].astype(BF16), g_final[None])
    y_p, p_cmp, p_slc, p_win, p_pool = _prompt_path(x_prompt, wts, g_attn, w_pool[0], pool_scale, *mlp_w)
    y_s, s_cmp, s_slc, s_win, s_pool = _sample_path(x_sample, cache_cmp[0], cache_slc[0], state_win[0],
                                                    state_pool[0], page_table, wts, g_attn, w_pool[0],
                                                    pool_scale, *mlp_w)
    return (y_p, y_s, p_cmp, p_slc, p_win, p_pool, s_cmp, s_slc, s_win, s_pool)
```

```python
import functools
import math

import jax
import jax.numpy as jnp
from jax import lax
from jax.experimental import pallas as pl
from jax.experimental.pallas import tpu as pltpu

D_MODEL = 1024
N_HEADS = 8
N_KV = 2
HPG = N_HEADS // N_KV
HEAD_DIM = 64
ATTN_WIDTH = N_HEADS * HEAD_DIM
POOL_WIDTH = D_MODEL - ATTN_WIDTH
POOL_WINDOWS = (2, 4, 8, 16)
POOL_GROUP = POOL_WIDTH // len(POOL_WINDOWS)
POOL_HIST = max(POOL_WINDOWS) - 1
BLK = 64
N_SELECT = 16
WINDOW = 512
CMP_HIDDEN = 2 * HEAD_DIM
D_FF = 4 * D_MODEL
KV_WIDTH = N_KV * 2 * HEAD_DIM
PAGE_SIZE = 128
BLOCKS_PER_PAGE = PAGE_SIZE // BLK
SCALE = HEAD_DIM ** -0.5
FORCE_SCORE = float(HPG + 1)
NORM_EPS = 1e-6
NEG = -1e30
N_GATE_PAD = 128
HALO = 16

F32 = jnp.float32
BF16 = jnp.bfloat16
NT_DIMS = (((1,), (1,)), ((), ()))

V7X_VMEM_LIMIT = 56 * 1024 * 1024

TOKEN_TILE = 512
ATTN_Q_TILE = 256
ATTN_K_TILE = 512
PROMPT_CMP_PAGES = 64
DECODE_CMP_PAGES = 128
DECODE_PAGE_CHUNK = 16
FF_CHUNK = 1024


def _slope(head):
    return float(2.0 ** (-8.0 * (head + 1) / N_HEADS))


def _cparams(sem, vmem=None, flags=None):
    return pltpu.CompilerParams(dimension_semantics=sem, vmem_limit_bytes=vmem, flags=flags)


def _proj_kernel(x_ref, g_ref, wtok_ref, wkv_ref, q_ref, gate_ref, u_ref, kc_ref, ks_ref, kw_ref, kvtb_ref, *rest,
                 tm, emit_tok_kv, emit_pages, seq_len):
    x = x_ref[0]
    ms = jnp.mean(x * x, axis=-1, keepdims=True)
    xn = ((x * lax.rsqrt(ms + NORM_EPS)) * g_ref[...]).astype(BF16)
    tok = lax.dot_general(xn, wtok_ref[...], NT_DIMS, preferred_element_type=F32)
    q_ref[0] = (tok[:, :ATTN_WIDTH] * SCALE).astype(BF16)
    u_ref[0] = tok[:, ATTN_WIDTH:ATTN_WIDTH + POOL_WIDTH]
    gl = tok[:, ATTN_WIDTH + POOL_WIDTH:ATTN_WIDTH + POOL_WIDTH + N_GATE_PAD]
    gate_ref[0] = 1.0 / (1.0 + jnp.exp(-gl))
    kvt = lax.dot_general(wkv_ref[...], xn, NT_DIMS, preferred_element_type=F32)
    for br, ref in enumerate((kc_ref, ks_ref, kw_ref)):
        ref[0] = kvt[br * KV_WIDTH:(br + 1) * KV_WIDTH]
    for br in range(2):
        kvtb_ref[br, 0] = kvt[(br + 1) * KV_WIDTH:(br + 2) * KV_WIDTH].astype(BF16)
    k = 0
    if emit_tok_kv:
        base = ATTN_WIDTH + POOL_WIDTH + N_GATE_PAD
        rest[k][0] = tok[:, base:base + 3 * KV_WIDTH]
        lane = lax.broadcasted_iota(jnp.int32, (1, PAGE_SIZE), 1)
        kvp = kvt if tm >= PAGE_SIZE else jnp.concatenate(
            [kvt, jnp.zeros((3 * KV_WIDTH, PAGE_SIZE - tm), F32)], axis=1)
        for s in range(tm // seq_len):
            grp, off = divmod(s * seq_len, PAGE_SIZE)
            blk = kvp[:, grp * PAGE_SIZE:(grp + 1) * PAGE_SIZE]
            if off:
                blk = pltpu.roll(blk, PAGE_SIZE - off, 1)
            blk = jnp.where(lane < seq_len, blk, 0.0)
            for br in range(3):
                rest[k + 1][br, s] = blk[br * KV_WIDTH:(br + 1) * KV_WIDTH]
        k += 2
    if emit_pages:
        for pg in range(tm // PAGE_SIZE):
            rest[k][pg] = kvt[:KV_WIDTH, pg * PAGE_SIZE:(pg + 1) * PAGE_SIZE]


def _proj(x, g_attn, wtok, wkv, *, tm, emit_tok_kv, emit_pages, seq_len=None):
    B, T, _ = x.shape
    nt = T // tm
    out_shape = [
        jax.ShapeDtypeStruct((B, T, ATTN_WIDTH), BF16),
        jax.ShapeDtypeStruct((B, T, N_GATE_PAD), F32),
        jax.ShapeDtypeStruct((B, T, POOL_WIDTH), F32),
    ] + [jax.ShapeDtypeStruct((B, KV_WIDTH, T), F32)] * 3 + [
        jax.ShapeDtypeStruct((2, B, KV_WIDTH, T), BF16),
    ]
    out_specs = [
        pl.BlockSpec((1, tm, ATTN_WIDTH), lambda b, i: (b, i, 0)),
        pl.BlockSpec((1, tm, N_GATE_PAD), lambda b, i: (b, i, 0)),
        pl.BlockSpec((1, tm, POOL_WIDTH), lambda b, i: (b, i, 0)),
    ] + [pl.BlockSpec((1, KV_WIDTH, tm), lambda b, i: (b, 0, i))] * 3 + [
        pl.BlockSpec((2, 1, KV_WIDTH, tm), lambda b, i: (0, b, 0, i)),
    ]
    if emit_tok_kv:
        out_shape.append(jax.ShapeDtypeStruct((B, T, 3 * KV_WIDTH), F32))
        out_specs.append(pl.BlockSpec((1, tm, 3 * KV_WIDTH), lambda b, i: (b, i, 0)))
        assert B == 1 and nt == 1 and tm % seq_len == 0
        out_shape.append(jax.ShapeDtypeStruct((3, tm // seq_len, KV_WIDTH, PAGE_SIZE), F32))
        out_specs.append(pl.BlockSpec((3, tm // seq_len, KV_WIDTH, PAGE_SIZE), lambda b, i: (0, 0, 0, 0)))
    if emit_pages:
        ppt = tm // PAGE_SIZE
        out_shape.append(jax.ShapeDtypeStruct((B * T // PAGE_SIZE, KV_WIDTH, PAGE_SIZE), F32))
        out_specs.append(pl.BlockSpec((ppt, KV_WIDTH, PAGE_SIZE), lambda b, i: (b * nt + i, 0, 0)))
    return pl.pallas_call(
        functools.partial(_proj_kernel, tm=tm, emit_tok_kv=emit_tok_kv, emit_pages=emit_pages, seq_len=seq_len),
        grid=(B, nt),
        in_specs=[
            pl.BlockSpec((1, tm, D_MODEL), lambda b, i: (b, i, 0)),
            pl.BlockSpec((1, D_MODEL), lambda b, i: (0, 0)),
            pl.BlockSpec(wtok.shape, lambda b, i: (0, 0)),
            pl.BlockSpec(wkv.shape, lambda b, i: (0, 0)),
        ],
        out_specs=out_specs,
        out_shape=out_shape,
        compiler_params=_cparams(("arbitrary", "arbitrary"), V7X_VMEM_LIMIT),
        name="proj",
    )(x, g_attn, wtok, wkv)


def _gelu_tanh(x):
    c = math.sqrt(2.0 / math.pi)
    return 0.5 * x * (1.0 + jnp.tanh(c * (x + 0.044715 * (x * x * x))))


def _compress_kernel(ids_ref, src_ref, pet_ref, w1_ref, w2_ref, out_ref, buf, sem, *, P, n_steps):
    i = pl.program_id(0)

    def page_copy(step, slot, p):
        return pltpu.make_async_copy(src_ref.at[ids_ref[step * P + p]], buf.at[slot, :, p, :], sem.at[slot])

    def start(step, slot):
        for p in range(P):
            page_copy(step, slot, p).start()

    @pl.when(i == 0)
    def _():
        start(0, 0)

    @pl.when(i + 1 < n_steps)
    def _():
        start(i + 1, (i + 1) % 2)

    slot = i % 2
    for p in range(P):
        page_copy(i, slot, p).wait()

    bref = buf.at[slot]
    for c in range(2):
        accs = [jnp.zeros((P, 2 * CMP_HIDDEN), F32) for _ in range(N_KV)]
        for dp in range(HEAD_DIM // 2):
            zero = jnp.zeros((BLK, CMP_HIDDEN), BF16)
            w_rows = []
            for dd in range(2):
                w = w1_ref[c, dp, dd]
                w_rows += [jnp.concatenate([w, zero], axis=1), jnp.concatenate([zero, w], axis=1)]
            w_tile = jnp.concatenate(w_rows, axis=0)
            for g in range(N_KV):
                halves = []
                for dd in range(2):
                    r = g * 2 * HEAD_DIM + c * HEAD_DIM + 2 * dp + dd
                    pr = c * HEAD_DIM + 2 * dp + dd
                    halves.append(bref[r] + pet_ref[pr:pr + 1, :])
                lhs = jnp.concatenate(halves, axis=1).astype(BF16)
                accs[g] = accs[g] + jnp.dot(lhs, w_tile, preferred_element_type=F32)
        acc = jnp.concatenate(accs, axis=0)
        hid = _gelu_tanh(acc).astype(BF16)
        oc = jnp.dot(hid, w2_ref[c], preferred_element_type=F32).astype(BF16)
        for g in range(N_KV):
            out_ref[g, c] = oc[g * P:(g + 1) * P]


def _compress(ids, src, pet, w1bd, w2bd, *, P):
    n_total = ids.shape[0]
    n_steps = n_total // P
    grid_spec = pltpu.PrefetchScalarGridSpec(
        num_scalar_prefetch=1,
        grid=(n_steps,),
        in_specs=[
            pl.BlockSpec(memory_space=pl.ANY),
            pl.BlockSpec(pet.shape, lambda i, ids: (0, 0)),
            pl.BlockSpec(w1bd.shape, lambda i, ids: (0, 0, 0, 0, 0), pipeline_mode=pl.Buffered(1)),
            pl.BlockSpec(w2bd.shape, lambda i, ids: (0, 0, 0)),
        ],
        out_specs=pl.BlockSpec((N_KV, 2, P, PAGE_SIZE), lambda i, ids: (0, 0, i, 0)),
        scratch_shapes=[
            pltpu.VMEM((2, KV_WIDTH, P, PAGE_SIZE), F32),
            pltpu.SemaphoreType.DMA((2,)),
        ],
    )
    return pl.pallas_call(
        functools.partial(_compress_kernel, P=P, n_steps=n_steps),
        grid_spec=grid_spec,
        out_shape=jax.ShapeDtypeStruct((N_KV, 2, n_total, PAGE_SIZE), BF16),
        compiler_params=_cparams(("arbitrary",), V7X_VMEM_LIMIT),
        name="compress",
    )(ids, src, pet, w1bd, w2bd)


def _compress_weights(cmp_pe, cmp_w1, cmp_w2):
    eye = jnp.eye(BLOCKS_PER_PAGE, dtype=F32)
    pet = jnp.tile(jnp.transpose(cmp_pe, (1, 2, 0)).reshape(2 * HEAD_DIM, BLK), (1, BLOCKS_PER_PAGE))
    w1 = cmp_w1.reshape(2, BLK, HEAD_DIM // 2, 2, CMP_HIDDEN)
    w1bd = jnp.transpose(w1, (0, 2, 3, 1, 4)).astype(BF16)
    w2bd = jnp.einsum('chd,jk->cjhkd', cmp_w2, eye).reshape(
        2, BLOCKS_PER_PAGE * CMP_HIDDEN, BLOCKS_PER_PAGE * HEAD_DIM).astype(BF16)
    return pet, w1bd, w2bd


def _rank_select_t(score, nblk, n_select):
    tq, W = score.shape
    x = score.T
    groups = [x[8 * k:8 * k + 8] for k in range(nblk // 8)]
    ranks = [jnp.zeros((8, tq), jnp.int32) for _ in groups]
    row = lax.broadcasted_iota(jnp.int32, (8, tq), 0)
    for i in range(nblk):
        xi = groups[i // 8][i % 8:i % 8 + 1, :]
        for k, xg in enumerate(groups):
            if 8 * k + 7 < i:
                ahead = xi > xg
            elif 8 * k > i:
                ahead = xi >= xg
            else:
                ahead = (xi > xg) | ((xi == xg) & (row > i - 8 * k))
            ranks[k] = ranks[k] + jnp.where(ahead, 1, 0)
    sel = [jnp.where(r < n_select, 1.0, 0.0) for r in ranks]
    if nblk < W:
        sel.append(jnp.zeros((W - nblk, tq), F32))
    return jnp.concatenate(sel, axis=0).T


def _key_consts(T, npg):
    assert T <= 256 * BLK and npg * BLOCKS_PER_PAGE <= HEAD_DIM
    k = jnp.arange(T, dtype=jnp.int32)[None, :]
    r = jnp.arange(HEAD_DIM, dtype=jnp.int32)[:, None]
    pos = jnp.where(r == 0, k // BLK, jnp.where(r == 1, k % BLK, 0)).astype(F32)
    en = jnp.where(k // BLK == r, NEG, 0.0)
    return jnp.concatenate([pos, en, jnp.zeros((HEAD_DIM, T), F32)], axis=0).astype(BF16)


M_INIT = -3e38


def _p_attn_kernel(q_ref, gate_ref, comp_ref, ks_ref, kw_ref, kc_ref, o_ref,
                   qx, m_ref, acc_ref, out_ref, flag_v, flag_s, flag_sem, *, tq, tk, npg):
    qi = pl.program_id(1)
    q0 = qi * tq
    pos_t = q0 + lax.broadcasted_iota(jnp.int32, (tq, 1), 0)
    nblk = npg * BLOCKS_PER_PAGE
    W = 128
    R = HPG * tq
    lane_h = lax.broadcasted_iota(jnp.int32, (1, HEAD_DIM), 1)

    for hh in range(N_HEADS):
        g, h = divmod(hh, HPG)
        rows = pl.ds(h * tq, tq)
        qx[g, rows, 0:HEAD_DIM] = q_ref[0, :, hh * HEAD_DIM:(hh + 1) * HEAD_DIM]
        posc = jnp.where(lane_h == 0, _slope(hh) * BLK, jnp.where(lane_h == 1, _slope(hh), 0.0))
        qx[g, rows, HEAD_DIM:2 * HEAD_DIM] = jnp.broadcast_to(posc, (tq, HEAD_DIM)).astype(BF16)
        qx[g, rows, 3 * HEAD_DIM:4 * HEAD_DIM] = jnp.zeros((tq, HEAD_DIM), BF16)

    blk_n = lax.broadcasted_iota(jnp.int32, (1, W), 1)
    first = blk_n < nblk
    end_pos = (blk_n + 1) * BLK - 1
    dist_c = (pos_t - end_pos).astype(F32)
    mask_c = (dist_c >= 0) & first
    cur = pos_t // BLK
    row_w = lax.broadcasted_iota(jnp.int32, (W, W), 0)
    col_w = lax.broadcasted_iota(jnp.int32, (W, W), 1)
    perm = jnp.where((col_w == (row_w % 2) * npg + row_w // 2) & (row_w < nblk), 1.0, 0.0).astype(BF16)

    used = []
    for g in range(N_KV):
        ck = comp_ref[g, 0].astype(F32)
        cv = comp_ref[g, 1].astype(F32)
        pad = [jnp.zeros((W - nblk, HEAD_DIM), F32)] if nblk < W else []
        ckp = jnp.concatenate([ck[:, :HEAD_DIM], ck[:, HEAD_DIM:]] + pad, axis=0).astype(BF16)
        cvp = jnp.concatenate([cv[:, :HEAD_DIM], cv[:, HEAD_DIM:]] + pad, axis=0).astype(BF16)
        ckp = jnp.dot(perm, ckp, preferred_element_type=F32).astype(BF16)
        cvp = jnp.dot(perm, cvp, preferred_element_type=F32).astype(BF16)
        imp = jnp.zeros((tq, W), F32)
        for h in range(HPG):
            hh = g * HPG + h
            qh = qx[g, pl.ds(h * tq, tq), 0:HEAD_DIM]
            s = lax.dot_general(qh, ckp, NT_DIMS, preferred_element_type=F32)
            s = jnp.where(mask_c, s - _slope(hh) * dist_c, NEG)
            mx = jnp.max(s, axis=1, keepdims=True)
            e = jnp.where(mask_c, jnp.exp(s - mx), 0.0)
            p = e / jnp.maximum(jnp.sum(e, axis=1, keepdims=True), 1e-30)
            imp = imp + p
            oc = jnp.dot(p.astype(BF16), cvp, preferred_element_type=F32)
            gcol = gate_ref[0, :, hh:hh + 1]
            out_ref[:, hh * HEAD_DIM:(hh + 1) * HEAD_DIM] = gcol * oc
        forced = (blk_n == 0) | (blk_n == cur) | (blk_n == cur - 1)
        score = jnp.where(forced, FORCE_SCORE, imp)
        score = jnp.where(blk_n > cur, -1.0, score)
        sel = _rank_select_t(score, nblk, N_SELECT)
        picked = (sel > 0.5) & (blk_n <= cur)
        notsel = jnp.where(picked, 0.0, 1.0).astype(BF16)
        for h in range(HPG):
            qx[g, pl.ds(h * tq, tq), 2 * HEAD_DIM:3 * HEAD_DIM] = notsel[:, :HEAD_DIM]
        used.append(jnp.max(jnp.where(picked, 1, 0), axis=0, keepdims=True))

    flag_v[...] = jnp.concatenate(used + [jnp.zeros((8 - N_KV, W), jnp.int32)], axis=0)
    flag_copy = pltpu.make_async_copy(flag_v, flag_s, flag_sem.at[0])
    flag_copy.start()

    pos_rows = q0 + lax.broadcasted_iota(jnp.int32, (R, 1), 0) % tq

    ones_rows = jnp.where(lax.broadcasted_iota(jnp.int32, (HEAD_DIM, 1), 0) == 0, 1.0, 0.0)

    def attend_tile(kv_ref, k0, width, kc_rows, mode, groups=tuple(range(N_KV))):
        kpos = k0 + lax.broadcasted_iota(jnp.int32, (1, width), 1)
        kc_tile = kc_ref[0:kc_rows, pl.ds(k0, width)]
        ones_tile = jnp.broadcast_to(ones_rows, (HEAD_DIM, width)).astype(BF16)
        if mode == "causal":
            keep = kpos <= pos_rows
        elif mode == "recent":
            keep = kpos > pos_rows - WINDOW
        for g in groups:
            kt = kv_ref[0, 0, g * 2 * HEAD_DIM:g * 2 * HEAD_DIM + HEAD_DIM, pl.ds(k0, width)]
            kext = jnp.concatenate([kt, kc_tile], axis=0)
            vt = kv_ref[0, 0, g * 2 * HEAD_DIM + HEAD_DIM:(g + 1) * 2 * HEAD_DIM, pl.ds(k0, width)]
            vext = jnp.concatenate([vt, ones_tile], axis=0)
            s = jnp.dot(qx[g, :, 0:HEAD_DIM + kc_rows], kext, preferred_element_type=F32)
            if mode != "full":
                s = jnp.where(keep, s, NEG)
            m_prev = m_ref[g]
            m_next = jnp.maximum(m_prev, jnp.max(s, axis=1, keepdims=True))
            alpha = jnp.exp(m_prev - m_next)
            m_ref[g] = m_next
            p = jnp.exp(s - jnp.concatenate([m_next] * (width // W), axis=1)).astype(BF16)
            pv = lax.dot_general(p, vext, NT_DIMS, preferred_element_type=F32)
            acc_ref[g] = alpha * acc_ref[g] + pv

    def reset():
        for g in range(N_KV):
            m_ref[g] = jnp.full((R, W), M_INIT, F32)
            acc_ref[g] = jnp.zeros((R, W), F32)

    def finish(gate_base):
        for g in range(N_KV):
            for h in range(HPG):
                hh = g * HPG + h
                a = acc_ref[g, pl.ds(h * tq, tq), :]
                o = a[:, :HEAD_DIM] / jnp.maximum(a[:, HEAD_DIM:HEAD_DIM + 1], 1e-30)
                gcol = gate_ref[0, :, gate_base + hh:gate_base + hh + 1]
                out_ref[:, hh * HEAD_DIM:(hh + 1) * HEAD_DIM] += gcol * o

    reset()
    wt = WINDOW // tq

    @pl.when(qi >= wt)
    def _():
        attend_tile(kw_ref, pl.multiple_of((qi - wt) * tq, tq), WINDOW, HEAD_DIM, "recent")

    @pl.when(qi < wt)
    def _():
        def win_body(j, carry):
            attend_tile(kw_ref, pl.multiple_of(j * tq, tq), tq, HEAD_DIM, "full")
            return carry

        lax.fori_loop(0, qi, win_body, 0)

    attend_tile(kw_ref, pl.multiple_of(q0, tq), tq, HEAD_DIM, "causal")
    finish(2 * N_HEADS)

    reset()
    flag_copy.wait()
    n_sel = (q0 + tq + tk - 1) // tk
    bpt = tk // BLK

    def sel_body(j, carry):
        hits = []
        for g in range(N_KV):
            hit = flag_s[g, j * bpt]
            for u in range(1, bpt):
                hit = hit | flag_s[g, j * bpt + u]
            hits.append(hit != 0)
        k0 = pl.multiple_of(j * tk, tk)

        @pl.when(hits[0] & hits[1])
        def _():
            attend_tile(ks_ref, k0, tk, 3 * HEAD_DIM, "full")

        @pl.when(hits[0] & jnp.logical_not(hits[1]))
        def _():
            attend_tile(ks_ref, k0, tk, 3 * HEAD_DIM, "full", groups=(0,))

        @pl.when(jnp.logical_not(hits[0]) & hits[1])
        def _():
            attend_tile(ks_ref, k0, tk, 3 * HEAD_DIM, "full", groups=(1,))

        return carry

    lax.fori_loop(0, n_sel - 1, sel_body, 0)
    attend_tile(ks_ref, pl.multiple_of((n_sel - 1) * tk, tk), tk, 3 * HEAD_DIM, "causal")
    finish(N_HEADS)
    o_ref[0] = out_ref[...].astype(BF16)


def _p_attn(q, gates, comp, kvtb, kconst, *, tq, tk):
    B, T, _ = q.shape
    npg = T // PAGE_SIZE
    assert WINDOW % tq == 0 and tk % tq == 0
    return pl.pallas_call(
        functools.partial(_p_attn_kernel, tq=tq, tk=tk, npg=npg),
        grid=(B, T // tq),
        in_specs=[
            pl.BlockSpec((1, tq, ATTN_WIDTH), lambda b, i: (b, i, 0)),
            pl.BlockSpec((1, tq, N_GATE_PAD), lambda b, i: (b, i, 0)),
            pl.BlockSpec((N_KV, 2, npg, PAGE_SIZE), lambda b, i: (0, 0, b, 0)),
            pl.BlockSpec((1, 1, KV_WIDTH, T), lambda b, i: (0, b, 0, 0)),
            pl.BlockSpec((1, 1, KV_WIDTH, T), lambda b, i: (1, b, 0, 0)),
            pl.BlockSpec(kconst.shape, lambda b, i: (0, 0)),
        ],
        out_specs=pl.BlockSpec((1, tq, ATTN_WIDTH), lambda b, i: (b, i, 0)),
        out_shape=jax.ShapeDtypeStruct((B, T, ATTN_WIDTH), BF16),
        scratch_shapes=[
            pltpu.VMEM((N_KV, HPG * tq, 4 * HEAD_DIM), BF16),
            pltpu.VMEM((N_KV, HPG * tq, 128), F32),
            pltpu.VMEM((N_KV, HPG * tq, 128), F32),
            pltpu.VMEM((tq, ATTN_WIDTH), F32),
            pltpu.VMEM((8, 128), jnp.int32),
            pltpu.SMEM((8, 128), jnp.int32),
            pltpu.SemaphoreType.DMA((1,)),
        ],
        compiler_params=_cparams(("arbitrary", "arbitrary"), V7X_VMEM_LIMIT),
        name="p_attn",
    )(q, gates, comp, kvtb, kvtb, kconst)


def _stack_heads(q_ref, g, ts):
    parts = [q_ref[0, :, (g * HPG + h) * HEAD_DIM:(g * HPG + h + 1) * HEAD_DIM].astype(F32) for h in range(HPG)]
    return jnp.concatenate(parts, axis=0).astype(BF16)


def _row_consts(g, ts):
    R = HPG * ts
    row = lax.broadcasted_iota(jnp.int32, (R, 1), 0)
    t_row = row % ts
    h_row = row // ts
    slope = jnp.zeros((R, 1), F32)
    for h in range(HPG):
        slope = jnp.where(h_row == h, _slope(g * HPG + h), slope)
    return t_row, slope


def _s_select_kernel(q_ref, gate_ref, comp_ref, cnew_ref, oc_ref, bits_ref, out_s, *, past, ts):
    W = 128
    nb_past = past // BLK
    lane = lax.broadcasted_iota(jnp.int32, (1, W), 1)
    idx_fns = [lambda c: 2 * c, lambda c: 2 * c + 1, lambda c: jnp.where(c == 0, nb_past, (1 << 20) + c)]
    idx_tiles = [f(lane) for f in idx_fns]
    valid_n = lane == 0
    tok = lax.broadcasted_iota(jnp.int32, (ts, 1), 0)
    cur_t = (past + tok) // BLK
    for g in range(N_KV):
        qg = _stack_heads(q_ref, g, ts)
        t_row, slope = _row_consts(g, ts)
        pos = past + t_row
        ck = comp_ref[g, 0]
        cv = comp_ref[g, 1]
        cn = cnew_ref[g, 0, 0].astype(F32)[:, :HEAD_DIM]
        vn = cnew_ref[g, 1, 0].astype(F32)[:, :HEAD_DIM]
        s_t = [lax.dot_general(qg, ck[:, :HEAD_DIM], NT_DIMS, preferred_element_type=F32),
               lax.dot_general(qg, ck[:, HEAD_DIM:], NT_DIMS, preferred_element_type=F32),
               jnp.broadcast_to(jnp.sum(qg.astype(F32) * cn, axis=1, keepdims=True), (HPG * ts, W))]
        masks, es = [], []
        for k in range(3):
            end_pos = (idx_tiles[k] + 1) * BLK - 1
            dist = (pos - end_pos).astype(F32)
            mk = dist >= 0
            if k == 2:
                mk = mk & valid_n
            masks.append(mk)
            s_t[k] = jnp.where(mk, s_t[k] - slope * dist, NEG)
        mx = jnp.maximum(jnp.maximum(jnp.max(s_t[0], axis=1, keepdims=True),
                                     jnp.max(s_t[1], axis=1, keepdims=True)),
                         jnp.max(s_t[2], axis=1, keepdims=True))
        for k in range(3):
            es.append(jnp.where(masks[k], jnp.exp(s_t[k] - mx), 0.0))
        den = (jnp.sum(es[0], axis=1, keepdims=True) + jnp.sum(es[1], axis=1, keepdims=True)
               + jnp.sum(es[2], axis=1, keepdims=True))
        inv = 1.0 / jnp.maximum(den, 1e-30)
        ps = [e * inv for e in es]
        o_c = (jnp.dot(ps[0].astype(BF16), cv[:, :HEAD_DIM], preferred_element_type=F32)
               + jnp.dot(ps[1].astype(BF16), cv[:, HEAD_DIM:], preferred_element_type=F32)
               + ps[2][:, 0:1].astype(BF16).astype(F32) * vn)
        for h in range(HPG):
            hh = g * HPG + h
            gcol = gate_ref[0, :, hh:hh + 1]
            out_s[:, hh * HEAD_DIM:(hh + 1) * HEAD_DIM] = gcol * o_c[h * ts:(h + 1) * ts]
        scores = []
        for k in range(3):
            imp = ps[k][0:ts]
            for h in range(1, HPG):
                imp = imp + ps[k][h * ts:(h + 1) * ts]
            idx = idx_tiles[k]
            forced = (idx == 0) | (idx == cur_t) | (idx == cur_t - 1)
            sc = jnp.where(forced, FORCE_SCORE, imp)
            sc = jnp.where(idx > cur_t, -1.0, sc)
            if k == 2:
                sc = jnp.where(valid_n, sc, -2.0)
            scores.append(sc)
        s_new = scores[2][:, 0:1]
        parts = [[jnp.where(s_new > scores[ka], 1, 0), jnp.zeros((ts, W), jnp.int32)] for ka in range(2)]
        for kb in range(2):
            for r in range(W):
                y = pltpu.roll(scores[kb], r, 1) if r else scores[kb]
                yi = idx_fns[kb]((lane - r) & (W - 1))
                for ka in range(2):
                    if ka == kb and r == 0:
                        continue
                    ahead = (y > scores[ka]) | ((y == scores[ka]) & (yi < idx_tiles[ka]))
                    parts[ka][r % 2] = parts[ka][r % 2] + jnp.where(ahead, 1, 0)
        sel = [(parts[k][0] + parts[k][1]) < N_SELECT for k in range(2)]
        rank_new = (jnp.sum(jnp.where(scores[0] >= s_new, 1, 0), axis=1, keepdims=True)
                    + jnp.sum(jnp.where(scores[1] >= s_new, 1, 0), axis=1, keepdims=True))
        wt = jnp.left_shift(1, 2 * tok)
        page_bits = jnp.sum(jnp.where(sel[0], wt, 0) + jnp.where(sel[1], 2 * wt, 0), axis=0, keepdims=True)
        new_bits = jnp.sum(jnp.where((rank_new < N_SELECT) & valid_n, jnp.left_shift(1, tok), 0),
                           axis=0, keepdims=True)
        bits_ref[0, g, 0:1, :] = page_bits
        bits_ref[0, g, 1:2, :] = new_bits
    oc_ref[0] = out_s[...]


def _s_select(q, gates, comp, cnew, *, past):
    Bs, ts, _ = q.shape
    npg = past // PAGE_SIZE
    assert npg == 128, "one lane tile of pages per sequence"
    return pl.pallas_call(
        functools.partial(_s_select_kernel, past=past, ts=ts),
        grid=(Bs,),
        in_specs=[
            pl.BlockSpec((1, ts, ATTN_WIDTH), lambda b: (b, 0, 0)),
            pl.BlockSpec((1, ts, N_GATE_PAD), lambda b: (b, 0, 0)),
            pl.BlockSpec((N_KV, 2, npg, PAGE_SIZE), lambda b: (0, 0, b, 0)),
            pl.BlockSpec((N_KV, 2, 1, 1, PAGE_SIZE), lambda b: (0, 0, b, 0, 0)),
        ],
        out_specs=[
            pl.BlockSpec((1, ts, ATTN_WIDTH), lambda b: (b, 0, 0)),
            pl.BlockSpec((1, N_KV, 2, 128), lambda b: (b, 0, 0, 0)),
        ],
        out_shape=[
            jax.ShapeDtypeStruct((Bs, ts, ATTN_WIDTH), F32),
            jax.ShapeDtypeStruct((Bs, N_KV, 2, 128), jnp.int32),
        ],
        scratch_shapes=[pltpu.VMEM((ts, ATTN_WIDTH), F32)],
        compiler_params=_cparams(("arbitrary",), V7X_VMEM_LIMIT),
        name="s_select",
    )(q, gates, comp, cnew)


def _s_attn_kernel(pt_ref, bits_ref, q_ref, gate_ref, oc_ref, win_ref, knew_ref, slc_ref,
                   o_ref, wout_ref, buf, sem, plist, cnts, out_s, *, past, ts, npg, chunk):
    b = pl.program_id(0)
    g = pl.program_id(1)
    W = 128
    R = HPG * ts
    wk = win_ref.shape[2]
    lin = b * N_KV + g
    n_lin = pl.num_programs(0) * N_KV
    base = lin * 2 * W
    slot = lin % 2

    def page_copy(page, gg, sl, idx):
        return pltpu.make_async_copy(slc_ref.at[page, pl.ds(gg * 2 * HEAD_DIM, 2 * HEAD_DIM)],
                                     buf.at[sl, idx], sem.at[sl])

    def issue_all(ln, sl):
        bb = ln // N_KV
        gg = ln % N_KV

        def scan(p, cnt):
            plist[sl * npg + cnt] = p
            return cnt + (bits_ref[ln * 2 * W + p] != 0).astype(jnp.int32)

        cnt = lax.fori_loop(0, npg, scan, 0, unroll=8)
        cnts[sl] = cnt

        def start(s, c):
            page_copy(pt_ref[bb * npg + plist[sl * npg + s]], gg, sl, s).start()
            return c

        lax.fori_loop(0, cnt, start, 0)

    @pl.when(lin == 0)
    def _():
        buf[...] = jnp.zeros(buf.shape, F32)
        for k in range(2 * npg):
            plist[k] = 0
        issue_all(lin, slot)

    @pl.when(lin + 1 < n_lin)
    def _():
        issue_all(lin + 1, 1 - slot)

    cnt = cnts[slot]

    q_all = q_ref[0].astype(F32)
    qg = jnp.concatenate([q_all[:, h * HEAD_DIM:(h + 1) * HEAD_DIM] for h in range(HPG)], axis=0).astype(BF16)
    row = lax.broadcasted_iota(jnp.int32, (R, 1), 0)
    t_row = row % ts
    head_row = row // ts + g * HPG
    slope = jnp.zeros((R, 1), F32)
    for hh in range(N_HEADS):
        slope = jnp.where(head_row == hh, _slope(hh), slope)
    lane = lax.broadcasted_iota(jnp.int32, (1, W), 1)

    kw = win_ref[0, 0:HEAD_DIM, :].astype(BF16)
    vw = win_ref[0, HEAD_DIM:2 * HEAD_DIM, :].astype(BF16)
    kn = knew_ref[2, 0, 0:HEAD_DIM, :].astype(BF16)
    vn = knew_ref[2, 0, HEAD_DIM:2 * HEAD_DIM, :].astype(BF16)
    i_st = lax.broadcasted_iota(jnp.int32, (1, wk), 1)
    d_st = t_row + wk - i_st
    m_st = (d_st >= 0) & (d_st < WINDOW)
    d_nw = t_row - lane
    m_nw = (d_nw >= 0) & (d_nw < WINDOW) & (lane < ts)
    s_st = jnp.where(m_st, jnp.dot(qg, kw, preferred_element_type=F32) - slope * d_st.astype(F32), NEG)
    s_nw = jnp.where(m_nw, jnp.dot(qg, kn, preferred_element_type=F32) - slope * d_nw.astype(F32), NEG)
    mx = jnp.maximum(jnp.max(s_st, axis=1, keepdims=True), jnp.max(s_nw, axis=1, keepdims=True))
    e_st = jnp.where(m_st, jnp.exp(s_st - mx), 0.0)
    e_nw = jnp.where(m_nw, jnp.exp(s_nw - mx), 0.0)
    den = jnp.sum(e_st, axis=1, keepdims=True) + jnp.sum(e_nw, axis=1, keepdims=True)
    inv = 1.0 / jnp.maximum(den, 1e-30)
    o_w = (lax.dot_general((e_st * inv).astype(BF16), vw, NT_DIMS, preferred_element_type=F32)
           + lax.dot_general((e_nw * inv).astype(BF16), vn, NT_DIMS, preferred_element_type=F32))

    last = wk - W
    shifted = pltpu.roll(win_ref[0], wk - ts, 1)
    newr = pltpu.roll(knew_ref[2, 0], W - ts, 1)
    wout_ref[0, :, 0:last] = shifted[:, 0:last]
    wout_ref[0, :, last:wk] = jnp.where(lane >= W - ts, newr, shifted[:, last:wk])

    bits_new = bits_ref[base + W]
    ksn = knew_ref[1, 0, 0:HEAD_DIM, :].astype(BF16)
    vsn = knew_ref[1, 0, HEAD_DIM:2 * HEAD_DIM, :].astype(BF16)
    d_sn = t_row - lane
    m_sn = (d_sn >= 0) & (lane < ts) & ((jnp.right_shift(bits_new, t_row) & 1) == 1)
    s_sn = jnp.where(m_sn, jnp.dot(qg, ksn, preferred_element_type=F32) - slope * d_sn.astype(F32), NEG)
    m0 = jnp.max(s_sn, axis=1, keepdims=True)
    e0 = jnp.where(m_sn, jnp.exp(s_sn - m0), 0.0)
    l0 = jnp.sum(e0, axis=1, keepdims=True)
    a0 = lax.dot_general(e0.astype(BF16), vsn, NT_DIMS, preferred_element_type=F32)

    def wait_one(s, c):
        page_copy(0, 0, slot, s).wait()
        return c

    lax.fori_loop(0, cnt, wait_one, 0)
    sh = 2 * t_row + (lane >= BLK).astype(jnp.int32)
    sh_c = jnp.concatenate([sh] * chunk, axis=1)
    bslot = buf.at[slot]

    def chunk_step(c, carry):
        m, l, acc = carry
        kts, vts, kpos, bitv = [], [], [], []
        for u in range(chunk):
            s = c * chunk + u
            ok = s < cnt
            p = jnp.where(ok, plist[slot * npg + s], 0)
            bits = jnp.where(ok, bits_ref[base + p], 0)
            kts.append(bslot[s, 0:HEAD_DIM, :])
            vts.append(bslot[s, HEAD_DIM:2 * HEAD_DIM, :])
            kpos.append(p * PAGE_SIZE + lane)
            bitv.append(jnp.zeros((1, W), jnp.int32) + bits)
        kt = jnp.concatenate(kts, axis=1).astype(BF16)
        vt = jnp.concatenate(vts, axis=1).astype(BF16)
        dist = past + t_row - jnp.concatenate(kpos, axis=1)
        mask = ((jnp.right_shift(jnp.concatenate(bitv, axis=1), sh_c) & 1) == 1) & (dist >= 0)
        sc = jnp.where(mask, jnp.dot(qg, kt, preferred_element_type=F32) - slope * dist.astype(F32), NEG)
        m_new = jnp.maximum(m, jnp.max(sc, axis=1, keepdims=True))
        alpha = jnp.exp(m - m_new)
        e = jnp.where(mask, jnp.exp(sc - m_new), 0.0)
        l = alpha * l + jnp.sum(e, axis=1, keepdims=True)
        acc = alpha * acc + lax.dot_general(e.astype(BF16), vt, NT_DIMS, preferred_element_type=F32)
        return m_new, l, acc

    m, l, acc = lax.fori_loop(0, (cnt + chunk - 1) // chunk, chunk_step, (m0, l0, a0))
    o_s = acc / jnp.maximum(l, 1e-30)

    g_vec = jnp.zeros((ts, 1), jnp.int32) + g
    for h in range(HPG):
        rows = slice(h * ts, (h + 1) * ts)
        g_s = jnp.zeros((ts, 1), F32)
        g_w = jnp.zeros((ts, 1), F32)
        for gg in range(N_KV):
            hh = gg * HPG + h
            g_s = jnp.where(g_vec == gg, gate_ref[0, :, N_HEADS + hh:N_HEADS + hh + 1], g_s)
            g_w = jnp.where(g_vec == gg, gate_ref[0, :, 2 * N_HEADS + hh:2 * N_HEADS + hh + 1], g_w)
        out_s[:, h * HEAD_DIM:(h + 1) * HEAD_DIM] = g_s * o_s[rows] + g_w * o_w[rows]
    o_ref[0] = oc_ref[0] + out_s[...]


def _s_attn(page_table, bits, q, gates, oc, win_t, knew, slc_t, *, past):
    Bs, ts, _ = q.shape
    npg = past // PAGE_SIZE
    wk = win_t.shape[2]
    gw = HPG * HEAD_DIM
    grid_spec = pltpu.PrefetchScalarGridSpec(
        num_scalar_prefetch=2,
        grid=(Bs, N_KV),
        in_specs=[
            pl.BlockSpec((1, ts, gw), lambda b, g, *_: (b, 0, g)),
            pl.BlockSpec((1, ts, N_GATE_PAD), lambda b, g, *_: (b, 0, 0)),
            pl.BlockSpec((1, ts, gw), lambda b, g, *_: (b, 0, g)),
            pl.BlockSpec((1, 2 * HEAD_DIM, wk), lambda b, g, *_: (b, g, 0)),
            pl.BlockSpec((3, 1, 2 * HEAD_DIM, PAGE_SIZE), lambda b, g, *_: (0, b, g, 0)),
            pl.BlockSpec(memory_space=pl.ANY),
        ],
        out_specs=[
            pl.BlockSpec((1, ts, gw), lambda b, g, *_: (b, 0, g)),
            pl.BlockSpec((1, 2 * HEAD_DIM, wk), lambda b, g, *_: (b, g, 0)),
        ],
        scratch_shapes=[
            pltpu.VMEM((2, npg, 2 * HEAD_DIM, PAGE_SIZE), F32),
            pltpu.SemaphoreType.DMA((2,)),
            pltpu.SMEM((2 * npg,), jnp.int32),
            pltpu.SMEM((2,), jnp.int32),
            pltpu.VMEM((ts, gw), F32),
        ],
    )
    chunk = DECODE_PAGE_CHUNK
    assert npg % chunk == 0
    return pl.pallas_call(
        functools.partial(_s_attn_kernel, past=past, ts=ts, npg=npg, chunk=chunk),
        grid_spec=grid_spec,
        out_shape=[
            jax.ShapeDtypeStruct((Bs, ts, ATTN_WIDTH), F32),
            jax.ShapeDtypeStruct(win_t.shape, F32),
        ],
        compiler_params=_cparams(("arbitrary", "arbitrary"), V7X_VMEM_LIMIT),
        name="s_attn",
    )(page_table.reshape(-1), bits.reshape(-1), q, gates, oc, win_t, knew, slc_t)


def _pool_kernel(u_ref, halo_ref, wp_ref, ps_ref, o_ref, ext, *, nb, tp, tiles_per_seq, pos_base):
    i = pl.program_id(0)
    ext[:, 0:HALO, :] = halo_ref[...]
    ext[:, HALO:HALO + tp, :] = u_ref[...]
    tile = i * nb + lax.broadcasted_iota(jnp.int32, (nb, tp, 1), 0)
    pos = pos_base + (tile % tiles_per_seq) * tp + lax.broadcasted_iota(jnp.int32, (nb, tp, 1), 1)
    for gi, w in enumerate(POOL_WINDOWS):
        lanes = slice(gi * POOL_GROUP, (gi + 1) * POOL_GROUP)
        own = ext[:, HALO:HALO + tp, lanes]
        tot = ext[:, :, lanes]
        span = 1
        while span < w:
            tot = tot[:, span:, :] + tot[:, :-span, :]
            span *= 2
        tot = tot[:, HALO - w + 1:HALO - w + 1 + tp, :]
        cnt = jnp.minimum(w, pos + 1).astype(F32)
        d = (tot / cnt - own).astype(BF16).reshape(nb * tp, POOL_GROUP)
        y = jnp.dot(d, wp_ref[gi], preferred_element_type=F32) * ps_ref[:, lanes]
        o_ref[:, :, lanes] = y.reshape(nb, tp, POOL_GROUP).astype(o_ref.dtype)


def _pool(u, halo, w_pool, pool_scale, *, nb, tiles_per_seq, pos_base, out_dtype):
    n_tiles, tp, _ = u.shape
    return pl.pallas_call(
        functools.partial(_pool_kernel, nb=nb, tp=tp, tiles_per_seq=tiles_per_seq, pos_base=pos_base),
        grid=(n_tiles // nb,),
        in_specs=[
            pl.BlockSpec((nb, tp, POOL_WIDTH), lambda i: (i, 0, 0)),
            pl.BlockSpec((nb, HALO, POOL_WIDTH), lambda i: (i, 0, 0)),
            pl.BlockSpec(w_pool.shape, lambda i: (0, 0, 0)),
            pl.BlockSpec((1, POOL_WIDTH), lambda i: (0, 0)),
        ],
        out_specs=pl.BlockSpec((nb, tp, POOL_WIDTH), lambda i: (i, 0, 0)),
        out_shape=jax.ShapeDtypeStruct((n_tiles, tp, POOL_WIDTH), out_dtype),
        scratch_shapes=[pltpu.VMEM((nb, HALO + tp, POOL_WIDTH), F32)],
        compiler_params=_cparams(("arbitrary",), V7X_VMEM_LIMIT),
        name="pool",
    )(u, halo, w_pool, pool_scale)


def _rms(x, g):
    return (x * lax.rsqrt(jnp.mean(x * x, axis=-1, keepdims=True) + NORM_EPS)) * g


def _mlp_kernel(x_ref, o_ref, p_ref, wo_ref, gm_ref, wu_ref, wd_ref, gf_ref, y_ref, *, ff_chunk):
    mix = jnp.concatenate([o_ref[...], p_ref[...]], axis=1)
    h = x_ref[...] + jnp.dot(mix, wo_ref[...], preferred_element_type=F32)
    hn = _rms(h, gm_ref[...]).astype(BF16)
    acc = jnp.zeros(h.shape, F32)
    for c in range(D_FF // ff_chunk):
        a = jnp.dot(hn, wu_ref[:, c * ff_chunk:(c + 1) * ff_chunk], preferred_element_type=F32)
        a = jnp.square(jnp.maximum(a, 0.0)).astype(BF16)
        acc = acc + jnp.dot(a, wd_ref[c * ff_chunk:(c + 1) * ff_chunk, :], preferred_element_type=F32)
    y_ref[...] = _rms(h + acc, gf_ref[...])


def _mlp(x, o, p, w_out, g_mlp, w_up, w_down, g_final, *, tm):
    N = x.shape[0]
    const = lambda i: (0, 0)
    resident = lambda a: pl.BlockSpec(a.shape, const, pipeline_mode=pl.Buffered(1))
    return pl.pallas_call(
        functools.partial(_mlp_kernel, ff_chunk=FF_CHUNK),
        grid=(N // tm,),
        in_specs=[
            pl.BlockSpec((tm, D_MODEL), lambda i: (i, 0)),
            pl.BlockSpec((tm, ATTN_WIDTH), lambda i: (i, 0)),
            pl.BlockSpec((tm, POOL_WIDTH), lambda i: (i, 0)),
            resident(w_out), resident(g_mlp), resident(w_up), resident(w_down), resident(g_final),
        ],
        out_specs=pl.BlockSpec((tm, D_MODEL), lambda i: (i, 0)),
        out_shape=jax.ShapeDtypeStruct((N, D_MODEL), F32),
        compiler_params=_cparams(("arbitrary",), V7X_VMEM_LIMIT),
        name="mlp",
    )(x, o, p, w_out, g_mlp, w_up, w_down, g_final)


def _rows_from_kt(kt):
    B, _, S = kt.shape
    return jnp.transpose(kt.reshape(B, N_KV, 2, HEAD_DIM, S), (0, 4, 1, 2, 3))[None]


def _prep_weights(w_in, cmp_pe, cmp_w1, cmp_w2):
    w_t = jnp.transpose(w_in)
    a, b = ATTN_WIDTH, ATTN_WIDTH + 3 * KV_WIDTH
    wq, wkv = w_t[:a], w_t[a:b]
    wg, wu = w_t[b:b + 3 * N_HEADS], w_t[b + 3 * N_HEADS:]
    wg = jnp.pad(wg, ((0, N_GATE_PAD - 3 * N_HEADS), (0, 0)))
    wtok = jnp.concatenate([wq, wu, wg], axis=0).astype(BF16)
    wtok_kv = jnp.concatenate([wq, wu, wg, wkv], axis=0).astype(BF16)
    return wtok, wtok_kv, wkv.astype(BF16), _compress_weights(cmp_pe, cmp_w1, cmp_w2)


def _prompt_path(x, wts, g_attn, w_pool, pool_scale, w_out, g_mlp, w_up, w_down, g_final):
    wtok, _, wkv, (pet, w1bd, w2bd) = wts
    B, T, _ = x.shape
    tm = min(TOKEN_TILE, T)
    q, gates, u, kc_t, ks_t, kw_t, kvtb, pages = _proj(x, g_attn, wtok, wkv, tm=tm, emit_tok_kv=False,
                                                        emit_pages=True)
    n_pages = B * T // PAGE_SIZE
    comp = _compress(jnp.arange(n_pages, dtype=jnp.int32), pages, pet, w1bd, w2bd, P=min(PROMPT_CMP_PAGES, n_pages))
    o = _p_attn(q, gates, comp, kvtb, _key_consts(T, T // PAGE_SIZE), tq=min(ATTN_Q_TILE, T), tk=min(ATTN_K_TILE, T))
    nt = T // tm
    u4 = u.reshape(B, nt, tm, POOL_WIDTH)
    halo = jnp.concatenate([jnp.zeros((B, 1, HALO, POOL_WIDTH), F32), u4[:, :-1, tm - HALO:, :]], axis=1)
    pool = _pool(u.reshape(B * nt, tm, POOL_WIDTH), halo.reshape(B * nt, HALO, POOL_WIDTH),
                 w_pool, pool_scale, nb=1, tiles_per_seq=nt, pos_base=0, out_dtype=BF16)
    N = B * T
    y = _mlp(x.reshape(N, D_MODEL), o.reshape(N, ATTN_WIDTH), pool.reshape(N, POOL_WIDTH),
             w_out, g_mlp, w_up, w_down, g_final, tm=tm)
    wk = min(WINDOW, T)
    return (y.reshape(B, T, D_MODEL), _rows_from_kt(kc_t), _rows_from_kt(ks_t),
            _rows_from_kt(kw_t[:, :, T - wk:]), u[None, :, T - POOL_HIST:])


def _kt_view(rows):
    n, s = rows.shape[:2]
    return jnp.transpose(rows, (0, 2, 3, 4, 1)).reshape(n, KV_WIDTH, s)


def _sample_path(x, cache_cmp, cache_slc, state_win, state_pool, page_table, wts, g_attn,
                 w_pool, pool_scale, w_out, g_mlp, w_up, w_down, g_final):
    _, wtok_kv, wkv, (pet, w1bd, w2bd) = wts
    Bs, ts, _ = x.shape
    N = Bs * ts
    npg = page_table.shape[1]
    past = npg * PAGE_SIZE
    q, gates, u, _, _, _, _, kv_tok, knew = _proj(x.reshape(1, N, D_MODEL), g_attn, wtok_kv, wkv, tm=N,
                                                  emit_tok_kv=True, emit_pages=False, seq_len=ts)
    comp_past = _compress(page_table.reshape(-1), _kt_view(cache_cmp), pet, w1bd, w2bd, P=min(DECODE_CMP_PAGES, Bs * npg))
    comp_new = _compress(jnp.arange(Bs, dtype=jnp.int32), knew[0], pet, w1bd, w2bd, P=Bs)
    qf = q.reshape(Bs, ts, ATTN_WIDTH).astype(F32)
    gts = gates.reshape(Bs, ts, N_GATE_PAD)
    oc, bits = _s_select(qf, gts, comp_past, comp_new.astype(F32).reshape(N_KV, 2, Bs, 1, PAGE_SIZE), past=past)
    o, win_new = _s_attn(page_table, bits, qf, gts, oc, _kt_view(state_win), knew, _kt_view(cache_slc), past=past)
    u3 = u.reshape(Bs, ts, POOL_WIDTH)
    halo = jnp.pad(state_pool, ((0, 0), (HALO - POOL_HIST, 0), (0, 0)))
    pool = _pool(u3, halo, w_pool, pool_scale, nb=Bs, tiles_per_seq=1, pos_base=past, out_dtype=F32)
    y = _mlp(x.reshape(N, D_MODEL), o.reshape(N, ATTN_WIDTH).astype(BF16), pool.reshape(N, POOL_WIDTH).astype(BF16),
             w_out, g_mlp, w_up, w_down, g_final, tm=N)
    kv_rows = kv_tok.reshape(Bs, ts, 3, N_KV, 2, HEAD_DIM)
    s_pool = jnp.concatenate([state_pool, u3], axis=1)[None, :, ts:]
    return (y.reshape(Bs, ts, D_MODEL), kv_rows[None, :, :, 0], kv_rows[None, :, :, 1],
            _rows_from_kt(win_new), s_pool)


def kernel(x_prompt, x_sample, cache_cmp, cache_slc, state_win, state_pool, page_table, g_attn, w_in,
           cmp_pe, cmp_w1, cmp_w2, w_pool, pool_scale, w_out, g_mlp, w_up, w_down, g_final):
    assert g_attn.shape[0] == 1, "single-layer trunk"
    wts = _prep_weights(w_in[0], cmp_pe[0], cmp_w1[0], cmp_w2[0])
    mlp_w = (w_out[0].astype(BF16), g_mlp, w_up[0].astype(BF16), w_down[0].astype(BF16), g_final[None])
    y_p, p_cmp, p_slc, p_win, p_pool = _prompt_path(x_prompt, wts, g_attn, w_pool[0], pool_scale, *mlp_w)
    y_s, s_cmp, s_slc, s_win, s_pool = _sample_path(x_sample, cache_cmp[0], cache_slc[0], state_win[0],
                                                    state_pool[0], page_table, wts, g_attn, w_pool[0],
                                                    pool_scale, *mlp_w)
    return (y_p, y_s, p_cmp, p_slc, p_win, p_pool, s_cmp, s_slc, s_win, s_pool)
```
